```python
import math
import jax, jax.numpy as jnp
from jax import lax
import numpy as np

D_MODEL = 4096
BATCH = 4
SEQ = 4096
DEPTH = 1

N_META = 16
N_HEADS = 32
QK_NOPE = 128
QK_ROPE = 64
QK_DIM = QK_NOPE + QK_ROPE
V_HEAD = 128
Q_LORA = 1024
KV_LORA = 512
ROPE_THETA = 10000.0
Q_BLOCK = 128
NEG_INF = -1e30
S5_WIDTH = 1536
S5_GROUP = 16
S5_GROUPS = S5_WIDTH // S5_GROUP
S5_STATE = 64
DT_MIN = 1e-3
DT_MAX = 1e-1
N_EXPERTS = 32
TOP_K = 4
D_EXPERT = 2048
SWIGLU_ALPHA = 1.702
SWIGLU_LIMIT = 7.0
MOE_BLOCK = 256
EPS = 1e-6

Q_LAT_END = Q_LORA
KV_LAT_END = Q_LAT_END + KV_LORA
K_ROPE_END = KV_LAT_END + QK_ROPE
S5_END = K_ROPE_END + S5_WIDTH
GATE_S5_END = S5_END + D_MODEL
D_IN = GATE_S5_END + D_MODEL

kernel_name = "hybrid_s5_mla_moe_block"


def rmsnorm(x, g):
    xf = x.astype(jnp.float32)
    xf = xf * lax.rsqrt(jnp.mean(xf * xf, axis=-1, keepdims=True) + EPS)
    return xf.astype(x.dtype) * g


def apply_rope(x, cos, sin):
    half = x.shape[-1] // 2
    x1, x2 = x[..., :half], x[..., half:]
    return jnp.concatenate([x1 * cos - x2 * sin, x1 * sin + x2 * cos], axis=-1)


def attend(q_nope, q_rope, q_pos, k_nope, k_rope, v, k_pos):
    s = (jnp.einsum('bqhd,bkhd->bhqk', q_nope, k_nope)
         + jnp.einsum('bqhr,bkr->bhqk', q_rope, k_rope)).astype(jnp.float32) * (QK_DIM ** -0.5)
    mask = k_pos[None, :] <= q_pos[:, None]
    s = jnp.where(mask[None, None], s, NEG_INF)
    p = jax.nn.softmax(s, axis=-1).astype(v.dtype)
    return jnp.einsum('bhqk,bkhd->bqhd', p, v)


def mla_branch(q_lat, kv_lat, k_rope_in, g_q_lat, g_kv_lat, w_uq, w_ukv):
    B, L, _ = q_lat.shape
    pos = jnp.arange(L, dtype=jnp.int32)
    inv_freq = ROPE_THETA ** (-jnp.arange(0, QK_ROPE, 2, dtype=jnp.float32) / QK_ROPE)
    ang = pos.astype(jnp.float32)[:, None] * inv_freq[None, :]
    cos = jnp.cos(ang).astype(q_lat.dtype)
    sin = jnp.sin(ang).astype(q_lat.dtype)
    q = (rmsnorm(q_lat, g_q_lat) @ w_uq).reshape(B, L, N_HEADS, QK_DIM)
    q_nope = q[..., :QK_NOPE]
    q_rope = apply_rope(q[..., QK_NOPE:], cos[:, None, :], sin[:, None, :])
    kv = (rmsnorm(kv_lat, g_kv_lat) @ w_ukv).reshape(B, L, N_HEADS, QK_NOPE + V_HEAD)
    k_nope, v = kv[..., :QK_NOPE], kv[..., QK_NOPE:]
    k_rope = apply_rope(k_rope_in, cos, sin)
    meta_out = attend(q_nope[:, :N_META], q_rope[:, :N_META], pos[:N_META],
                      k_nope[:, :N_META], k_rope[:, :N_META], v[:, :N_META], pos[:N_META])
    n_blocks = (L - N_META) // Q_BLOCK

    def real_block(i):
        start = N_META + i * Q_BLOCK
        qn = lax.dynamic_slice_in_dim(q_nope, start, Q_BLOCK, axis=1)
        qr = lax.dynamic_slice_in_dim(q_rope, start, Q_BLOCK, axis=1)
        q_pos = start + jnp.arange(Q_BLOCK, dtype=jnp.int32)
        return attend(qn, qr, q_pos, k_nope, k_rope, v, pos)

    blocks = lax.map(real_block, jnp.arange(n_blocks, dtype=jnp.int32))
    real = jnp.transpose(blocks, (1, 0, 2, 3, 4)).reshape(B, n_blocks * Q_BLOCK, N_HEADS * V_HEAD)
    return jnp.concatenate([meta_out.reshape(B, N_META, N_HEADS * V_HEAD), real], axis=1)


def s5_branch(u, a_re, a_im, log_dt, b_re, b_im, c_re, c_im, d_skip, w_glu_a, w_glu_b):
    B, L, _ = u.shape
    dt = jnp.exp(log_dt.astype(jnp.float32))[:, None]
    lr, li = a_re.astype(jnp.float32), a_im.astype(jnp.float32)
    mag = jnp.exp(lr * dt)
    ab_re, ab_im = mag * jnp.cos(li * dt), mag * jnp.sin(li * dt)
    den = lr * lr + li * li
    f_re = ((ab_re - 1.0) * lr + ab_im * li) / den
    f_im = (ab_im * lr - (ab_re - 1.0) * li) / den
    br, bi = b_re.astype(jnp.float32), b_im.astype(jnp.float32)
    bb_re = f_re[..., None] * br - f_im[..., None] * bi
    bb_im = f_re[..., None] * bi + f_im[..., None] * br
    ug = u.astype(jnp.float32).reshape(B, L, S5_GROUPS, S5_GROUP)
    bu_re = jnp.einsum('blgc,gpc->blgp', ug, bb_re)
    bu_im = jnp.einsum('blgc,gpc->blgp', ug, bb_im)
    a_re_s = jnp.broadcast_to(ab_re[None, None], (1, L, S5_GROUPS, S5_STATE))
    a_im_s = jnp.broadcast_to(ab_im[None, None], (1, L, S5_GROUPS, S5_STATE))

    def combine(e1, e2):
        ar1, ai1, xr1, xi1 = e1
        ar2, ai2, xr2, xi2 = e2
        return (ar2 * ar1 - ai2 * ai1, ar2 * ai1 + ai2 * ar1,
                ar2 * xr1 - ai2 * xi1 + xr2, ar2 * xi1 + ai2 * xr1 + xi2)

    _, _, xr, xi = lax.associative_scan(combine, (a_re_s, a_im_s, bu_re, bu_im), axis=1)
    y = (jnp.einsum('blgp,gcp->blgc', xr, c_re.astype(jnp.float32))
         - jnp.einsum('blgp,gcp->blgc', xi, c_im.astype(jnp.float32)))
    y = y.reshape(B, L, S5_WIDTH) + d_skip.astype(jnp.float32) * u.astype(jnp.float32)
    z = jax.nn.gelu(y).astype(u.dtype)
    return (z @ w_glu_a) * jax.nn.sigmoid(z @ w_glu_b)


def moe_ffn(h, w_router, b_router, w_gu, b_gu, w_down, b_down, layer):
    B, L, D = h.shape
    T = B * L
    TK = T * TOP_K
    hf = h.reshape(T, D)
    logits = hf.astype(jnp.float32) @ w_router.astype(jnp.float32) + b_router.astype(jnp.float32)
    top_val, top_idx = lax.top_k(logits, TOP_K)
    top_w = jax.nn.softmax(top_val, axis=-1)
    e_flat = top_idx.reshape(TK).astype(jnp.int32)
    tok_flat = jnp.repeat(jnp.arange(T, dtype=jnp.int32), TOP_K)
    w_flat = top_w.reshape(TK)
    order = jnp.argsort(e_flat)
    s_e, s_tok, s_w = e_flat[order], tok_flat[order], w_flat[order]
    counts = jax.ops.segment_sum(jnp.ones((TK,), jnp.int32), e_flat, num_segments=N_EXPERTS)
    padded = ((counts + MOE_BLOCK - 1) // MOE_BLOCK) * MOE_BLOCK
    pad_start = jnp.cumsum(padded) - padded
    grp_start = jnp.cumsum(counts) - counts
    dest = pad_start[s_e] + (jnp.arange(TK, dtype=jnp.int32) - grp_start[s_e])
    n_pad = ((TK + N_EXPERTS * (MOE_BLOCK - 1) + MOE_BLOCK - 1) // MOE_BLOCK) * MOE_BLOCK
    n_blocks = n_pad // MOE_BLOCK
    pad_tok = jnp.full((n_pad,), T, jnp.int32).at[dest].set(s_tok)
    pad_w = jnp.zeros((n_pad,), jnp.float32).at[dest].set(s_w)
    blk_expert = jnp.clip(jnp.searchsorted(jnp.cumsum(padded),
                                           jnp.arange(n_blocks, dtype=jnp.int32) * MOE_BLOCK,
                                           side='right'), 0, N_EXPERTS - 1)
    h_ext = jnp.concatenate([hf, jnp.zeros((1, D), hf.dtype)], axis=0)

    def body(i, out):
        idx = lax.dynamic_slice_in_dim(pad_tok, i * MOE_BLOCK, MOE_BLOCK)
        wts = lax.dynamic_slice_in_dim(pad_w, i * MOE_BLOCK, MOE_BLOCK)
        e = blk_expert[i]
        xb = h_ext[idx]
        gu = xb @ w_gu[layer, e] + b_gu[layer, e]
        gate = jnp.minimum(gu[:, :D_EXPERT], SWIGLU_LIMIT)
        up = jnp.clip(gu[:, D_EXPERT:], -SWIGLU_LIMIT, SWIGLU_LIMIT)
        act = gate * jax.nn.sigmoid(SWIGLU_ALPHA * gate) * (up + 1.0)
        yb = act @ w_down[layer, e] + b_down[layer, e]
        return out.at[idx].add(yb * wts[:, None].astype(yb.dtype))

    out = lax.fori_loop(0, n_blocks, body, jnp.zeros((T + 1, D), h.dtype))
    return out[:T].reshape(B, L, D)


def setup_inputs(seed: int = 0) -> dict:
    key = jax.random.key(seed)
    ks = jax.random.split(key, 28)
    f32 = jnp.float32
    nrm = lambda k, shape, scale: jax.random.normal(k, shape, f32) * scale
    gain = lambda k, shape: 1.0 + 0.02 * jax.random.normal(k, shape, f32)
    a_re = -0.5 + 0.01 * jax.random.normal(ks[9], (DEPTH, S5_GROUPS, S5_STATE), f32)
    a_im = (math.pi * jnp.arange(S5_STATE, dtype=f32))[None, None, :] + 0.01 * jax.random.normal(ks[10], (DEPTH, S5_GROUPS, S5_STATE), f32)
    log_dt = jax.random.uniform(ks[11], (DEPTH, S5_GROUPS), f32, math.log(DT_MIN), math.log(DT_MAX))
    return {
        "x": nrm(ks[0], (BATCH, SEQ, D_MODEL), 1.0),
        "meta_tokens": nrm(ks[1], (N_META, D_MODEL), 1.0),
        "g_mix": gain(ks[2], (DEPTH, D_MODEL)),
        "w_in": nrm(ks[3], (DEPTH, D_MODEL, D_IN), D_MODEL ** -0.5),
        "b_gate": nrm(ks[4], (DEPTH, 2 * D_MODEL), 0.1),
        "g_q_lat": gain(ks[5], (DEPTH, Q_LORA)),
        "g_kv_lat": gain(ks[6], (DEPTH, KV_LORA)),
        "w_uq": nrm(ks[7], (DEPTH, Q_LORA, N_HEADS * QK_DIM), Q_LORA ** -0.5),
        "w_ukv": nrm(ks[8], (DEPTH, KV_LORA, N_HEADS * (QK_NOPE + V_HEAD)), KV_LORA ** -0.5),
        "s5_a_re": a_re,
        "s5_a_im": a_im,
        "s5_log_dt": log_dt,
        "s5_b_re": nrm(ks[12], (DEPTH, S5_GROUPS, S5_STATE, S5_GROUP), (2 * S5_GROUP) ** -0.5),
        "s5_b_im": nrm(ks[13], (DEPTH, S5_GROUPS, S5_STATE, S5_GROUP), (2 * S5_GROUP) ** -0.5),
        "s5_c_re": nrm(ks[14], (DEPTH, S5_GROUPS, S5_GROUP, S5_STATE), S5_STATE ** -0.5),
        "s5_c_im": nrm(ks[15], (DEPTH, S5_GROUPS, S5_GROUP, S5_STATE), S5_STATE ** -0.5),
        "s5_d": nrm(ks[16], (DEPTH, S5_WIDTH), 1.0),
        "w_glu_a": nrm(ks[17], (DEPTH, S5_WIDTH, D_MODEL), S5_WIDTH ** -0.5),
        "w_glu_b": nrm(ks[18], (DEPTH, S5_WIDTH, D_MODEL), S5_WIDTH ** -0.5),
        "w_out": nrm(ks[19], (DEPTH, D_MODEL, D_MODEL), D_MODEL ** -0.5),
        "g_ffn": gain(ks[20], (DEPTH, D_MODEL)),
        "w_router": nrm(ks[21], (DEPTH, D_MODEL, N_EXPERTS), D_MODEL ** -0.5),
        "b_router": nrm(ks[22], (DEPTH, N_EXPERTS), 0.01),
        "w_gu": nrm(ks[23], (DEPTH, N_EXPERTS, D_MODEL, 2 * D_EXPERT), D_MODEL ** -0.5),
        "b_gu": nrm(ks[24], (DEPTH, N_EXPERTS, 2 * D_EXPERT), 0.01),
        "w_down": nrm(ks[25], (DEPTH, N_EXPERTS, D_EXPERT, D_MODEL), D_EXPERT ** -0.5),
        "b_down": nrm(ks[26], (DEPTH, N_EXPERTS, D_MODEL), 0.01),
        "g_final": gain(ks[27], (D_MODEL,)),
    }


def reference(x, meta_tokens, g_mix, w_in, b_gate, g_q_lat, g_kv_lat, w_uq, w_ukv,
              s5_a_re, s5_a_im, s5_log_dt, s5_b_re, s5_b_im, s5_c_re, s5_c_im, s5_d,
              w_glu_a, w_glu_b, w_out, g_ffn, w_router, b_router, w_gu, b_gu, w_down, b_down,
              g_final):
    B = x.shape[0]
    meta = jnp.broadcast_to(meta_tokens[None].astype(x.dtype), (B, N_META, D_MODEL))
    h = jnp.concatenate([meta, x], axis=1)
    for l in range(DEPTH):
        hn = rmsnorm(h, g_mix[l])
        proj = hn @ w_in[l]
        y_mla = mla_branch(proj[..., :Q_LAT_END], proj[..., Q_LAT_END:KV_LAT_END],
                           proj[..., KV_LAT_END:K_ROPE_END], g_q_lat[l], g_kv_lat[l],
                           w_uq[l], w_ukv[l])
        y_s5 = s5_branch(proj[..., K_ROPE_END:S5_END], s5_a_re[l], s5_a_im[l], s5_log_dt[l],
                         s5_b_re[l], s5_b_im[l], s5_c_re[l], s5_c_im[l], s5_d[l],
                         w_glu_a[l], w_glu_b[l])
        gates = jax.nn.sigmoid(proj[..., S5_END:] + b_gate[l])
        mixed = gates[..., :D_MODEL] * y_s5 + gates[..., D_MODEL:] * y_mla
        h = h + mixed @ w_out[l]
        h = h + moe_ffn(rmsnorm(h, g_ffn[l]), w_router[l], b_router[l],
                        w_gu, b_gu, w_down, b_down, l)
    h = rmsnorm(h, g_final)
    return h[:, N_META:]
```

```python
import functools
import math

import jax
import jax.numpy as jnp
from jax import lax
from jax.experimental import pallas as pl
from jax.experimental.pallas import tpu as pltpu

F32 = jnp.float32
BF16 = jnp.bfloat16

EPS = 1e-6
QK_NOPE = 128
QK_ROPE = 64
QK_DIM = QK_NOPE + QK_ROPE
V_HEAD = 128
ROPE_THETA = 10000.0
NEG_INF = -1e30
S5_GROUP = 16
TOP_K = 4
SWIGLU_ALPHA = 1.702
SWIGLU_LIMIT = 7.0

LANES = 128
SUBLANES = 8
HEAD_PAD = 2 * LANES
S5_TILE_GROUPS = LANES // S5_GROUP
VMEM_LIMIT_V7X = 56 * 1024 * 1024


def _cparams(sem, vmem=VMEM_LIMIT_V7X):
    return pltpu.CompilerParams(dimension_semantics=sem, vmem_limit_bytes=vmem)


def _pick(n, pref, mult=LANES):
    if n <= pref:
        return n
    best = None
    for t in range(mult, pref + 1, mult):
        if n % t == 0:
            best = t
    assert best is not None, (n, pref, mult)
    return best


def _norm_mm_kernel(a_ref, g_ref, w_ref, *rest, rope_group):
    if rope_group:
        cos_ref, sin_ref, o_ref, an_ref = rest
    else:
        o_ref, an_ref = rest

    @pl.when(pl.program_id(1) == 0)
    def _():
        xf = a_ref[...].astype(F32)
        ms = jnp.mean(xf * xf, axis=-1, keepdims=True)
        an_ref[...] = (xf * lax.rsqrt(ms + EPS) * g_ref[...]).astype(BF16)

    acc = jnp.dot(an_ref[...], w_ref[...], preferred_element_type=F32)
    if not rope_group:
        o_ref[...] = acc.astype(o_ref.dtype)
        return
    c = cos_ref[...]
    s = sin_ref[...]
    for g0 in range(0, acc.shape[1], rope_group):
        lo = g0 + rope_group - LANES
        if lo > g0:
            o_ref[:, g0:lo] = acc[:, g0:lo].astype(o_ref.dtype)
        slab = acc[:, lo:lo + LANES]
        o_ref[:, lo:lo + LANES] = (slab * c + pltpu.roll(slab, QK_ROPE, 1) * s).astype(o_ref.dtype)


def _norm_mm(a, col_blk, k, gain, w, out_dtype, tm, tn, rope=None, name=None):
    m = a.shape[0]
    n = w.shape[1]
    tm = _pick(m, tm, SUBLANES)
    tn = _pick(n, tn)
    in_specs = [
        pl.BlockSpec((tm, k), lambda i, j: (i, col_blk)),
        pl.BlockSpec((1, k), lambda i, j: (0, 0)),
        pl.BlockSpec((k, tn), lambda i, j: (0, j)),
    ]
    args = [a, gain.reshape(1, k).astype(F32), w]
    group = 0
    if rope is not None:
        cos_t, sin_t, group = rope
        nrb = cos_t.shape[0] // tm
        in_specs += [pl.BlockSpec((tm, LANES), lambda i, j: (i % nrb, 0))] * 2
        args += [cos_t, sin_t]
    return pl.pallas_call(
        functools.partial(_norm_mm_kernel, rope_group=group),
        grid=(m // tm, n // tn),
        in_specs=in_specs,
        out_specs=pl.BlockSpec((tm, tn), lambda i, j: (i, j)),
        out_shape=jax.ShapeDtypeStruct((m, n), out_dtype),
        scratch_shapes=[pltpu.VMEM((tm, k), BF16)],
        compiler_params=_cparams(("parallel", "arbitrary")),
        name=name,
    )(*args)


def _s5_kernel(u_ref, perm_ref, permt_ref, bw_ref, cw_ref, a_ref, g_ref, d_ref, c0_ref,
               z_ref, cout_ref, bu_ref, xb_ref, carry_ref, *, tc, ns):
    t = pl.program_id(2)
    tseg = tc // SUBLANES

    @pl.when(t == 0)
    def _():
        carry_ref[...] = c0_ref[0]

    u_p = jnp.dot(perm_ref[...], u_ref[...], preferred_element_type=F32).astype(BF16)
    bu_ref[...] = jnp.dot(u_p, bw_ref[0], preferred_element_type=F32)

    ar = jnp.broadcast_to(a_ref[0, 0:1, :], (SUBLANES, ns))
    ai = jnp.broadcast_to(a_ref[0, 1:2, :], (SUBLANES, ns))

    def local_scan(i, x):
        xr, xi = x
        r0 = pl.multiple_of(i * SUBLANES, SUBLANES)
        nr = ar * xr - ai * xi + bu_ref[pl.ds(r0, SUBLANES), 0:ns]
        ni = ar * xi + ai * xr + bu_ref[pl.ds(r0, SUBLANES), ns:2 * ns]
        bu_ref[pl.ds(r0, SUBLANES), 0:ns] = nr
        bu_ref[pl.ds(r0, SUBLANES), ns:2 * ns] = ni
        return nr, ni

    zero = jnp.zeros((SUBLANES, ns), F32)
    er, ei = lax.fori_loop(0, tseg, local_scan, (zero, zero))

    c_r = carry_ref[0:1, :]
    c_i = carry_ref[1:2, :]
    sub = lax.broadcasted_iota(jnp.int32, (SUBLANES, ns), 0)
    g1r = g_ref[0, 0:1, :]
    g1i = g_ref[0, 1:2, :]
    fr = er + jnp.where(sub == 0, g1r * c_r - g1i * c_i, 0.0)
    fi = ei + jnp.where(sub == 0, g1r * c_i + g1i * c_r, 0.0)
    for k, d in enumerate((1, 2, 4)):
        gr = g_ref[0, 2 * k:2 * k + 1, :]
        gi = g_ref[0, 2 * k + 1:2 * k + 2, :]
        sr = jnp.where(sub >= d, pltpu.roll(fr, d, 0), 0.0)
        si = jnp.where(sub >= d, pltpu.roll(fi, d, 0), 0.0)
        fr, fi = fr + gr * sr - gi * si, fi + gr * si + gi * sr
    cin_r = jnp.where(sub == 0, c_r, pltpu.roll(fr, 1, 0))
    cin_i = jnp.where(sub == 0, c_i, pltpu.roll(fi, 1, 0))
    carry_ref[0:1, :] = fr[SUBLANES - 1:SUBLANES, :]
    carry_ref[1:2, :] = fi[SUBLANES - 1:SUBLANES, :]

    def add_carry(j, c):
        cr, ci = c
        c1r, c1i = ar * cr - ai * ci, ar * ci + ai * cr
        c2r, c2i = ar * c1r - ai * c1i, ar * c1i + ai * c1r
        r0 = pl.multiple_of(j * 2 * SUBLANES, 2 * SUBLANES)
        r1 = r0 + SUBLANES
        xr = jnp.concatenate([bu_ref[pl.ds(r0, SUBLANES), 0:ns] + c1r,
                              bu_ref[pl.ds(r1, SUBLANES), 0:ns] + c2r], axis=0)
        xi = jnp.concatenate([bu_ref[pl.ds(r0, SUBLANES), ns:2 * ns] + c1i,
                              bu_ref[pl.ds(r1, SUBLANES), ns:2 * ns] + c2i], axis=0)
        xb_ref[pl.ds(r0, 2 * SUBLANES), 0:ns] = xr.astype(BF16)
        xb_ref[pl.ds(r0, 2 * SUBLANES), ns:2 * ns] = xi.astype(BF16)
        return c2r, c2i

    lax.fori_loop(0, tseg // 2, add_carry, (cin_r, cin_i))

    y = jnp.dot(xb_ref[...], cw_ref[0], preferred_element_type=F32) + d_ref[...] * u_p.astype(F32)
    zp = jax.nn.gelu(y).astype(BF16)
    z_ref[...] = jnp.dot(permt_ref[...], zp, preferred_element_type=F32).astype(BF16)

    @pl.when(t == pl.num_programs(2) - 1)
    def _():
        cout_ref[0, 0] = carry_ref[...]


def _s5_scan(u_arr, u_col0, nb, seq, tc, tabs, c0, name):
    bw, cw, a_tab, g_tab, d_skip = tabs
    ntile, _, ns2 = bw.shape
    ns = ns2 // 2
    nt = seq // tc
    tseg = tc // SUBLANES
    r = jnp.arange(tc, dtype=jnp.int32)
    src = (r % SUBLANES) * tseg + r // SUBLANES
    perm = (src[:, None] == jnp.arange(tc, dtype=jnp.int32)[None, :]).astype(BF16)
    kern = functools.partial(_s5_kernel, tc=tc, ns=ns)
    return pl.pallas_call(
        kern,
        grid=(nb, ntile, nt),
        in_specs=[
            pl.BlockSpec((tc, LANES), lambda b, n, t: (b * nt + t, u_col0 + n)),
            pl.BlockSpec((tc, tc), lambda b, n, t: (0, 0)),
            pl.BlockSpec((tc, tc), lambda b, n, t: (0, 0)),
            pl.BlockSpec((1, LANES, 2 * ns), lambda b, n, t: (n, 0, 0)),
            pl.BlockSpec((1, 2 * ns, LANES), lambda b, n, t: (n, 0, 0)),
            pl.BlockSpec((1, 2, ns), lambda b, n, t: (n, 0, 0)),
            pl.BlockSpec((1, 6, ns), lambda b, n, t: (n, 0, 0)),
            pl.BlockSpec((1, LANES), lambda b, n, t: (0, n)),
            pl.BlockSpec((1, 2, ns), lambda b, n, t: (n, 0, 0)),
        ],
        out_specs=[
            pl.BlockSpec((tc, LANES), lambda b, n, t: (b * nt + t, n)),
            pl.BlockSpec((1, 1, 2, ns), lambda b, n, t: (b, n, 0, 0)),
        ],
        out_shape=[
            jax.ShapeDtypeStruct((nb * seq, ntile * LANES), BF16),
            jax.ShapeDtypeStruct((nb, ntile, 2, ns), F32),
        ],
        scratch_shapes=[
            pltpu.VMEM((tc, 2 * ns), F32),
            pltpu.VMEM((tc, 2 * ns), BF16),
            pltpu.VMEM((2, ns), F32),
        ],
        compiler_params=_cparams(("parallel", "parallel", "arbitrary")),
        name=name,
    )(u_arr, perm, perm.T, bw, cw, a_tab, g_tab, d_skip, c0)


def _s5_tables(a_re, a_im, log_dt, b_re, b_im, c_re, c_im, d_skip):
    ngrp, nst = a_re.shape
    ntile = ngrp // S5_TILE_GROUPS
    ns = S5_TILE_GROUPS * nst
    dt = jnp.exp(log_dt.astype(F32))[:, None]
    lr, li = a_re.astype(F32), a_im.astype(F32)

    def a_pow(k):
        mag = jnp.exp(lr * dt * k)
        return mag * jnp.cos(li * dt * k), mag * jnp.sin(li * dt * k)

    ab_re, ab_im = a_pow(1)
    den = lr * lr + li * li
    f_re = ((ab_re - 1.0) * lr + ab_im * li) / den
    f_im = (ab_im * lr - (ab_re - 1.0) * li) / den
    br, bi = b_re.astype(F32), b_im.astype(F32)
    bb_re = f_re[..., None] * br - f_im[..., None] * bi
    bb_im = f_re[..., None] * bi + f_im[..., None] * br
    eye = jnp.eye(S5_TILE_GROUPS, dtype=F32)

    def b_tile(bb):
        bb = bb.reshape(ntile, S5_TILE_GROUPS, nst, S5_GROUP)
        return jnp.einsum("ab,napc->nacbp", eye, bb).reshape(ntile, LANES, ns)

    def c_tile(cc):
        cc = cc.astype(F32).reshape(ntile, S5_TILE_GROUPS, S5_GROUP, nst)
        return jnp.einsum("ab,nacp->nbpac", eye, cc).reshape(ntile, ns, LANES)

    bw = jnp.concatenate([b_tile(bb_re), b_tile(bb_im)], axis=2).astype(BF16)
    cw = jnp.concatenate([c_tile(c_re), -c_tile(c_im)], axis=1).astype(BF16)
    a_tab = jnp.stack([ab_re.reshape(ntile, ns), ab_im.reshape(ntile, ns)], axis=1)

    def g_tab(tseg):
        rows = []
        for d in (1, 2, 4):
            pr, pi = a_pow(tseg * d)
            rows += [pr.reshape(ntile, ns), pi.reshape(ntile, ns)]
        return jnp.stack(rows, axis=1)

    return bw, cw, a_tab, g_tab, d_skip.astype(F32).reshape(1, -1)


def _glu_kernel(z_ref, wa_ref, wb_ref, gs_ref, bg_ref, o_ref):
    z = z_ref[...]
    a = jnp.dot(z, wa_ref[...], preferred_element_type=F32)
    b = jnp.dot(z, wb_ref[...], preferred_element_type=F32)
    gate = jax.nn.sigmoid(gs_ref[...].astype(F32) + bg_ref[...])
    o_ref[...] = (a * jax.nn.sigmoid(b) * gate).astype(o_ref.dtype)


def _glu(z, wa, wb, proj, gs_col0, bias, tm, tn):
    m, k = z.shape
    n = wa.shape[1]
    tm = _pick(m, tm, SUBLANES)
    tn = _pick(math.gcd(n, gs_col0) if gs_col0 else n, tn)
    goff = gs_col0 // tn
    return pl.pallas_call(
        _glu_kernel,
        grid=(m // tm, n // tn),
        in_specs=[
            pl.BlockSpec((tm, k), lambda i, j: (i, 0)),
            pl.BlockSpec((k, tn), lambda i, j: (0, j)),
            pl.BlockSpec((k, tn), lambda i, j: (0, j)),
            pl.BlockSpec((tm, tn), lambda i, j: (i, goff + j)),
            pl.BlockSpec((1, tn), lambda i, j: (0, j)),
        ],
        out_specs=pl.BlockSpec((tm, tn), lambda i, j: (i, j)),
        out_shape=jax.ShapeDtypeStruct((m, n), BF16),
        compiler_params=_cparams(("parallel", "arbitrary")),
        name="glu",
    )(z, wa, wb, proj, bias)


def _attn_kernel(q_ref, kn_ref, v_ref, kr_ref, km_ref, vm_ref, krm_ref, gm_ref, bg_ref, o_ref,
                 *, tq, tk, n_meta, scale):
    qi = pl.program_id(2)
    q = q_ref[...]

    def scores(kn, kr):
        k = jnp.concatenate([kn, kr], axis=1)
        return lax.dot_general(q, k, (((1,), (1,)), ((), ())), preferred_element_type=F32) * scale

    s = scores(km_ref[...], krm_ref[...])
    col = lax.broadcasted_iota(jnp.int32, s.shape, 1)
    s = jnp.where(col < n_meta, s, NEG_INF)
    m = jnp.max(s, axis=1, keepdims=True)
    p = jnp.exp(s - m)
    l = jnp.sum(p, axis=1, keepdims=True)
    acc = jnp.dot(p.astype(BF16), vm_ref[...], preferred_element_type=F32)

    def step(j, carry, masked):
        m, l, acc = carry
        r0 = pl.multiple_of(j * tk, tk)
        s = scores(kn_ref[pl.ds(r0, tk), :], kr_ref[pl.ds(r0, tk), :])
        if masked:
            row = qi * tq + lax.broadcasted_iota(jnp.int32, s.shape, 0)
            colp = j * tk + lax.broadcasted_iota(jnp.int32, s.shape, 1)
            s = jnp.where(colp <= row, s, NEG_INF)
        m_new = jnp.maximum(m, jnp.max(s, axis=1, keepdims=True))
        alpha = jnp.exp(m - m_new)
        p = jnp.exp(s - m_new)
        l = alpha * l + jnp.sum(p, axis=1, keepdims=True)
        acc = alpha * acc + jnp.dot(p.astype(BF16), v_ref[pl.ds(r0, tk), :],
                                    preferred_element_type=F32)
        return m_new, l, acc

    n_full = (qi * tq) // tk
    carry = lax.fori_loop(0, n_full, lambda j, c: step(j, c, False), (m, l, acc))
    for d in range(tq // tk):
        carry = step(n_full + d, carry, True)
    m, l, acc = carry
    gate = jax.nn.sigmoid(gm_ref[...].astype(F32) + bg_ref[...])
    o_ref[...] = (acc / l * gate).astype(o_ref.dtype)


def _attention(q, kv, kr, kvm, krm, proj, gm_col0, bias, nb, seq, nh, n_meta, tq, tk):
    tq = _pick(seq, tq, SUBLANES)
    tk = _pick(tq, tk, SUBLANES)
    nq = seq // tq
    mrows = kvm.shape[0]
    goff = gm_col0 // LANES
    kern = functools.partial(_attn_kernel, tq=tq, tk=tk, n_meta=n_meta, scale=QK_DIM ** -0.5)
    return pl.pallas_call(
        kern,
        grid=(nb, nh, nq),
        in_specs=[
            pl.BlockSpec((tq, HEAD_PAD), lambda b, h, i: (b * nq + i, h)),
            pl.BlockSpec((seq, LANES), lambda b, h, i: (b, 2 * h)),
            pl.BlockSpec((seq, LANES), lambda b, h, i: (b, 2 * h + 1)),
            pl.BlockSpec((seq, LANES), lambda b, h, i: (b, 0)),
            pl.BlockSpec((mrows, LANES), lambda b, h, i: (0, 2 * h)),
            pl.BlockSpec((mrows, LANES), lambda b, h, i: (0, 2 * h + 1)),
            pl.BlockSpec((mrows, LANES), lambda b, h, i: (0, 0)),
            pl.BlockSpec((tq, LANES), lambda b, h, i: (b * nq + i, goff + h)),
            pl.BlockSpec((1, LANES), lambda b, h, i: (0, h)),
        ],
        out_specs=pl.BlockSpec((tq, LANES), lambda b, h, i: (b * nq + i, h)),
        out_shape=jax.ShapeDtypeStruct((nb * seq, nh * V_HEAD), BF16),
        compiler_params=_cparams(("parallel", "parallel", "arbitrary")),
        name="attention",
    )(q, kv, kv, kr, kvm, kvm, krm, proj, bias)


def _outproj_kernel(ys_ref, ym_ref, w_ref, res_ref, o_ref, a_ref):
    @pl.when(pl.program_id(1) == 0)
    def _():
        a_ref[...] = (ys_ref[...].astype(F32) + ym_ref[...].astype(F32)).astype(BF16)

    o_ref[...] = jnp.dot(a_ref[...], w_ref[...], preferred_element_type=F32) + res_ref[...]


def _outproj(ys, ym, w, res, tm, tn):
    m, k = ys.shape
    n = w.shape[1]
    tm = _pick(m, tm, SUBLANES)
    tn = _pick(n, tn)
    return pl.pallas_call(
        _outproj_kernel,
        grid=(m // tm, n // tn),
        in_specs=[
            pl.BlockSpec((tm, k), lambda i, j: (i, 0)),
            pl.BlockSpec((tm, k), lambda i, j: (i, 0)),
            pl.BlockSpec((k, tn), lambda i, j: (0, j)),
            pl.BlockSpec((tm, tn), lambda i, j: (i, j)),
        ],
        out_specs=pl.BlockSpec((tm, tn), lambda i, j: (i, j)),
        out_shape=jax.ShapeDtypeStruct((m, n), F32),
        scratch_shapes=[pltpu.VMEM((tm, k), BF16)],
        compiler_params=_cparams(("parallel", "arbitrary")),
        name="outproj",
    )(ys, ym, w, res)


def _router_kernel(h_ref, g_ref, wr_ref, br_ref, hn_ref, idx_ref, wt_ref):
    xf = h_ref[...]
    ms = jnp.mean(xf * xf, axis=-1, keepdims=True)
    hn = xf * lax.rsqrt(ms + EPS) * g_ref[...]
    hn_ref[...] = hn
    logits = jnp.dot(hn, wr_ref[...], preferred_element_type=F32,
                     precision=lax.Precision.HIGHEST) + br_ref[...]
    ne = logits.shape[1]
    lane = lax.broadcasted_iota(jnp.int32, logits.shape, 1)
    vals, ids = [], []
    for _ in range(TOP_K):
        mx = jnp.max(logits, axis=1, keepdims=True)
        ix = jnp.min(jnp.where(logits == mx, lane, ne), axis=1, keepdims=True)
        vals.append(mx)
        ids.append(ix)
        logits = jnp.where(lane == ix, -jnp.inf, logits)
    v = jnp.concatenate(vals, axis=1)
    e = jnp.exp(v - v[:, 0:1])
    idx_ref[...] = jnp.concatenate(ids, axis=1)
    wt_ref[...] = e / jnp.sum(e, axis=1, keepdims=True)


def _router(h, gain, w_router, b_router, tm):
    m, d = h.shape
    ne = w_router.shape[1]
    tm = _pick(m, tm, SUBLANES)
    return pl.pallas_call(
        _router_kernel,
        grid=(m // tm,),
        in_specs=[
            pl.BlockSpec((tm, d), lambda i: (i, 0)),
            pl.BlockSpec((1, d), lambda i: (0, 0)),
            pl.BlockSpec((d, ne), lambda i: (0, 0)),
            pl.BlockSpec((1, ne), lambda i: (0, 0)),
        ],
        out_specs=[
            pl.BlockSpec((tm, d), lambda i: (i, 0)),
            pl.BlockSpec((tm, TOP_K), lambda i: (i, 0)),
            pl.BlockSpec((tm, TOP_K), lambda i: (i, 0)),
        ],
        out_shape=[
            jax.ShapeDtypeStruct((m, d), F32),
            jax.ShapeDtypeStruct((m, TOP_K), jnp.int32),
            jax.ShapeDtypeStruct((m, TOP_K), F32),
        ],
        compiler_params=_cparams(("parallel",)),
        name="router",
    )(h, gain.reshape(1, d).astype(F32), w_router.astype(F32), b_router.reshape(1, ne).astype(F32))


def _gather_kernel(idx_ref, x_hbm, o_ref, buf, sem, *, rows):
    i = pl.program_id(0)
    n = pl.num_programs(0)

    def issue(step, slot):
        def body(r, _):
            tok = idx_ref[step, r]
            pltpu.make_async_copy(x_hbm.at[pl.ds(tok, 1)], buf.at[slot, pl.ds(r, 1)],
                                  sem.at[slot]).start()
            return 0
        lax.fori_loop(0, rows, body, 0)

    @pl.when(i == 0)
    def _():
        issue(0, 0)

    @pl.when(i + 1 < n)
    def _():
        issue(i + 1, (i + 1) % 2)

    slot = i % 2
    pltpu.make_async_copy(x_hbm.at[pl.ds(0, rows)], buf.at[slot], sem.at[slot]).wait()
    o_ref[...] = buf[slot].astype(o_ref.dtype)


def _gather_rows(x, idx2d, out_dtype):
    nsteps, rows = idx2d.shape
    d = x.shape[1]
    return pl.pallas_call(
        functools.partial(_gather_kernel, rows=rows),
        grid_spec=pltpu.PrefetchScalarGridSpec(
            num_scalar_prefetch=1,
            grid=(nsteps,),
            in_specs=[pl.BlockSpec(memory_space=pl.ANY)],
            out_specs=pl.BlockSpec((rows, d), lambda i, idx: (i, 0)),
            scratch_shapes=[pltpu.VMEM((2, rows, d), x.dtype), pltpu.SemaphoreType.DMA((2,))],
        ),
        out_shape=jax.ShapeDtypeStruct((nsteps * rows, d), out_dtype),
        compiler_params=_cparams(("arbitrary",)),
        name="gather",
    )(idx2d, x)


def _moe_kernel(be_ref, na_ref, x_ref, wg_ref, wu_ref, bg_ref, bu_ref, wd_ref, bd_ref, pw_ref,
                o_ref, *, tcol):
    i = pl.program_id(0)
    c = pl.program_id(1)
    nc = pl.num_programs(1)

    @pl.when(i < na_ref[0])
    def _():
        xb = x_ref[...]
        g = jnp.dot(xb, wg_ref[0], preferred_element_type=F32) + bg_ref[0]
        u = jnp.dot(xb, wu_ref[0], preferred_element_type=F32) + bu_ref[0]
        gate = jnp.minimum(g, SWIGLU_LIMIT)
        up = jnp.clip(u, -SWIGLU_LIMIT, SWIGLU_LIMIT)
        act = (gate * jax.nn.sigmoid(SWIGLU_ALPHA * gate) * (up + 1.0)).astype(BF16)

        @pl.when(c == 0)
        def _():
            o_ref[...] = jnp.zeros_like(o_ref)

        for n0 in range(0, o_ref.shape[1], tcol):
            o_ref[:, n0:n0 + tcol] += jnp.dot(act, wd_ref[0, :, n0:n0 + tcol],
                                              preferred_element_type=F32)

        @pl.when(c == nc - 1)
        def _():
            o_ref[...] = (o_ref[...] + bd_ref[0]) * pw_ref[...]

    @pl.when(jnp.logical_and(i >= na_ref[0], c == 0))
    def _():
        o_ref[...] = jnp.zeros_like(o_ref)


def _moe(xs, blk_expert, n_active, w_gu, b_gu, w_down, b_down, pad_w, bm, tn):
    npad, d = xs.shape
    ne, de, _ = w_down.shape
    tn = _pick(de, tn)
    nc = de // tn
    nblk = npad // bm

    def blk(i, na):
        return jnp.minimum(i, na[0] - 1)

    in_specs = [
        pl.BlockSpec((bm, d), lambda i, c, be, na: (blk(i, na), 0)),
        pl.BlockSpec((1, d, tn), lambda i, c, be, na: (be[blk(i, na)], 0, c)),
        pl.BlockSpec((1, d, tn), lambda i, c, be, na: (be[blk(i, na)], 0, nc + c)),
        pl.BlockSpec((1, 1, tn), lambda i, c, be, na: (be[blk(i, na)], 0, c)),
        pl.BlockSpec((1, 1, tn), lambda i, c, be, na: (be[blk(i, na)], 0, nc + c)),
        pl.BlockSpec((1, tn, d), lambda i, c, be, na: (be[blk(i, na)], c, 0)),
        pl.BlockSpec((1, 1, d), lambda i, c, be, na: (be[blk(i, na)], 0, 0)),
        pl.BlockSpec((bm, 1), lambda i, c, be, na: (blk(i, na), 0)),
    ]
    return pl.pallas_call(
        functools.partial(_moe_kernel, tcol=_pick(d, 512)),
        grid_spec=pltpu.PrefetchScalarGridSpec(
            num_scalar_prefetch=2,
            grid=(nblk, nc),
            in_specs=in_specs,
            out_specs=pl.BlockSpec((bm, d), lambda i, c, be, na: (i, 0)),
        ),
        out_shape=jax.ShapeDtypeStruct((npad, d), F32),
        compiler_params=_cparams(("arbitrary", "arbitrary")),
        name="moe",
    )(blk_expert, n_active, xs, w_gu, w_gu, b_gu, b_gu, w_down, b_down, pad_w)


def _combine_kernel(idx_ref, y_hbm, h_ref, g_ref, o_ref, buf, sem, *, rows, tt):
    i = pl.program_id(0)
    n = pl.num_programs(0)

    def issue(step, slot):
        def body(r, _):
            src = idx_ref[step, r]
            pltpu.make_async_copy(y_hbm.at[pl.ds(src, 1)], buf.at[slot, pl.ds(r, 1)],
                                  sem.at[slot]).start()
            return 0
        lax.fori_loop(0, rows, body, 0)

    @pl.when(i == 0)
    def _():
        issue(0, 0)

    @pl.when(i + 1 < n)
    def _():
        issue(i + 1, (i + 1) % 2)

    slot = i % 2
    pltpu.make_async_copy(y_hbm.at[pl.ds(0, rows)], buf.at[slot], sem.at[slot]).wait()
    acc = h_ref[...]
    for k in range(TOP_K):
        acc = acc + buf[slot, k * tt:(k + 1) * tt, :]
    ms = jnp.mean(acc * acc, axis=-1, keepdims=True)
    o_ref[...] = acc * lax.rsqrt(ms + EPS) * g_ref[...]


def _combine(ys, idx2d, h, gain, tt):
    nsteps, rows = idx2d.shape
    d = h.shape[1]
    return pl.pallas_call(
        functools.partial(_combine_kernel, rows=rows, tt=tt),
        grid_spec=pltpu.PrefetchScalarGridSpec(
            num_scalar_prefetch=1,
            grid=(nsteps,),
            in_specs=[
                pl.BlockSpec(memory_space=pl.ANY),
                pl.BlockSpec((tt, d), lambda i, idx: (i, 0)),
                pl.BlockSpec((1, d), lambda i, idx: (0, 0)),
            ],
            out_specs=pl.BlockSpec((tt, d), lambda i, idx: (i, 0)),
            scratch_shapes=[pltpu.VMEM((2, rows, d), F32), pltpu.SemaphoreType.DMA((2,))],
        ),
        out_shape=jax.ShapeDtypeStruct(h.shape, F32),
        compiler_params=_cparams(("arbitrary",)),
        name="combine",
    )(idx2d, ys, h, gain.reshape(1, d).astype(F32))


def _dispatch_indices(top_idx, top_w, ne, bm):
    t = top_idx.shape[0]
    tk = t * TOP_K
    e_flat = top_idx.reshape(tk)
    onehot = (e_flat[:, None] == jnp.arange(ne, dtype=jnp.int32)[None, :]).astype(jnp.int32)
    csum = jnp.cumsum(onehot, axis=0)
    rank = jnp.sum(onehot * csum, axis=1) - 1
    counts = csum[-1]
    padded = ((counts + bm - 1) // bm) * bm
    pad_end = jnp.cumsum(padded)
    dest = (pad_end - padded)[e_flat] + rank
    nblk = (tk + ne * (bm - 1) + bm - 1) // bm
    npad = nblk * bm
    tok_flat = jnp.arange(tk, dtype=jnp.int32) // TOP_K
    pad_tok = jnp.zeros((npad,), jnp.int32).at[dest].set(tok_flat)
    pad_w = jnp.zeros((npad,), F32).at[dest].set(top_w.reshape(tk))
    blk_expert = jnp.clip(
        jnp.searchsorted(pad_end, jnp.arange(nblk, dtype=jnp.int32) * bm, side="right"),
        0, ne - 1).astype(jnp.int32)
    n_active = (pad_end[-1] // bm).astype(jnp.int32).reshape(1)
    return dest.reshape(t, TOP_K), pad_tok, pad_w.reshape(npad, 1), blk_expert, n_active


def _rope_tables(n_pos):
    pos = jnp.arange(n_pos, dtype=F32)
    inv_freq = ROPE_THETA ** (-jnp.arange(0, QK_ROPE, 2, dtype=F32) / QK_ROPE)
    ang = pos[:, None] * inv_freq[None, :]
    cos, sin = jnp.cos(ang), jnp.sin(ang)
    zero = jnp.zeros((n_pos, LANES - QK_ROPE), F32)
    return jnp.concatenate([cos, cos, zero], axis=1), jnp.concatenate([-sin, sin, zero], axis=1)


def _with_swapped_rope(w_rope):
    half = QK_ROPE // 2
    return jnp.concatenate([w_rope, w_rope[..., half:], w_rope[..., :half]], axis=-1)


def kernel(x, meta_tokens, g_mix, w_in, b_gate, g_q_lat, g_kv_lat, w_uq, w_ukv, s5_a_re, s5_a_im, s5_log_dt, s5_b_re, s5_b_im, s5_c_re, s5_c_im, s5_d, w_glu_a, w_glu_b, w_out, g_ffn, w_router, b_router, w_gu, b_gu, w_down, b_down, g_final):
    assert w_in.shape[0] == 1, "single-layer block"
    nb, seq, d = x.shape
    t = nb * seq
    n_meta = meta_tokens.shape[0]
    ql, kvl = g_q_lat.shape[-1], g_kv_lat.shape[-1]
    sw = s5_d.shape[-1]
    nh = w_uq.shape[-1] // QK_DIM
    ne = w_router.shape[-1]
    assert ql % kvl == 0 and (ql + kvl) % LANES == 0 and sw % LANES == 0 and d % LANES == 0
    assert n_meta % (2 * SUBLANES) == 0 and n_meta <= LANES

    w0 = w_in[0]
    c_kv = ql + kvl
    c_kr = c_kv + QK_ROPE
    w_main = jnp.concatenate([w0[:, :c_kv], w0[:, c_kr:]], axis=1).astype(BF16)
    w_kr = _with_swapped_rope(w0[:, c_kv:c_kr]).astype(BF16)
    o_s5 = c_kv
    o_gs = o_s5 + sw
    o_gm = o_gs + d
    wq3 = w_uq[0].reshape(ql, nh, QK_DIM)
    w_q = jnp.concatenate([wq3[..., :QK_NOPE], _with_swapped_rope(wq3[..., QK_NOPE:])],
                          axis=-1).reshape(ql, nh * HEAD_PAD).astype(BF16)
    w_kv = w_ukv[0].astype(BF16)
    cos_t, sin_t = _rope_tables(n_meta + seq)
    rope_meta = (cos_t[:n_meta], sin_t[:n_meta])
    rope_real = (cos_t[n_meta:], sin_t[n_meta:])
    s5_tabs = _s5_tables(s5_a_re[0], s5_a_im[0], s5_log_dt[0], s5_b_re[0], s5_b_im[0],
                         s5_c_re[0], s5_c_im[0], s5_d[0])
    bw, cw, a_tab, g_tab_fn, d_skip = s5_tabs

    x2 = x.reshape(t, d)
    tm = 512

    proj_m = _norm_mm(meta_tokens, 0, d, g_mix[0], w_main, BF16, n_meta, 512, name="proj_meta")
    kr_m = _norm_mm(meta_tokens, 0, d, g_mix[0], w_kr, BF16, n_meta, LANES,
                    rope=(*rope_meta, LANES), name="krope_meta")
    kv_m = _norm_mm(proj_m, ql // kvl, kvl, g_kv_lat[0], w_kv, BF16, n_meta, 1024, name="kv_meta")
    kvm = jnp.pad(kv_m, ((0, LANES - n_meta), (0, 0)))
    krm = jnp.pad(kr_m, ((0, LANES - n_meta), (0, 0)))
    ntile = sw // LANES
    ns = bw.shape[2] // 2
    _, c_meta = _s5_scan(proj_m, o_s5 // LANES, 1, n_meta, n_meta,
                         (bw, cw, a_tab, g_tab_fn(n_meta // SUBLANES), d_skip),
                         jnp.zeros((ntile, 2, ns), F32), "s5_meta")

    proj = _norm_mm(x2, 0, d, g_mix[0], w_main, BF16, tm, 512, name="proj")
    kr = _norm_mm(x2, 0, d, g_mix[0], w_kr, BF16, tm, LANES, rope=(*rope_real, LANES), name="krope")
    q = _norm_mm(proj, 0, ql, g_q_lat[0], w_q, BF16, tm, 1024, rope=(*rope_real, HEAD_PAD), name="q")
    kv = _norm_mm(proj, ql // kvl, kvl, g_kv_lat[0], w_kv, BF16, tm, 1024, name="kv")

    tc = _pick(seq, 512, 2 * SUBLANES)
    z, _ = _s5_scan(proj, o_s5 // LANES, nb, seq, tc,
                    (bw, cw, a_tab, g_tab_fn(tc // SUBLANES), d_skip), c_meta[0], "s5")
    bias = b_gate[0].astype(F32).reshape(1, 2 * d)
    ys = _glu(z, w_glu_a[0].astype(BF16), w_glu_b[0].astype(BF16), proj, o_gs, bias[:, :d], 1024, 512)
    ym = _attention(q, kv, kr, kvm, krm, proj, o_gm, bias[:, d:], nb, seq, nh, n_meta, 512, 512)
    h = _outproj(ys, ym, w_out[0].astype(BF16), x2, tm, 512)

    hn, top_idx, top_w = _router(h, g_ffn[0], w_router[0], b_router[0], 256)
    bm = 512 if t * TOP_K >= 512 * ne else 128
    dest, pad_tok, pad_w, blk_expert, n_active = _dispatch_indices(top_idx, top_w, ne, bm)
    grows = min(256, bm)
    xs = _gather_rows(hn, pad_tok.reshape(-1, grows), BF16)
    de = w_down.shape[2]
    ysort = _moe(xs, blk_expert, n_active, w_gu[0].astype(BF16), b_gu[0].reshape(ne, 1, 2 * de),
                 w_down[0].astype(BF16), b_down[0].reshape(ne, 1, d), pad_w, bm, 256)
    tt = 64
    idx_c = dest.reshape(t // tt, tt, TOP_K).transpose(0, 2, 1).reshape(t // tt, TOP_K * tt)
    out = _combine(ysort, idx_c, h, g_final, tt)
    return out.reshape(nb, seq, d)
```

```python
import functools
import math

import jax
import jax.numpy as jnp
from jax import lax
from jax.experimental import pallas as pl
from jax.experimental.pallas import tpu as pltpu

F32 = jnp.float32
BF16 = jnp.bfloat16

EPS = 1e-6
QK_NOPE = 128
QK_ROPE = 64
QK_DIM = QK_NOPE + QK_ROPE
V_HEAD = 128
ROPE_THETA = 10000.0
NEG_INF = -1e30
S5_GROUP = 16
TOP_K = 4
SWIGLU_ALPHA = 1.702
SWIGLU_LIMIT = 7.0

LANES = 128
SUBLANES = 8
HEAD_PAD = 2 * LANES
S5_TILE_GROUPS = LANES // S5_GROUP
VMEM_LIMIT_V7X = 56 * 1024 * 1024


def _cparams(sem, vmem=VMEM_LIMIT_V7X):
    return pltpu.CompilerParams(dimension_semantics=sem, vmem_limit_bytes=vmem)


def _x_slab_rows(d):
    return d // (2 * LANES)


def _y_slab_rows(d):
    return d // (2 * LANES)


def _pick(n, pref, mult=LANES):
    if n <= pref:
        return n
    best = None
    for t in range(mult, pref + 1, mult):
        if n % t == 0:
            best = t
    assert best is not None, (n, pref, mult)
    return best


def _norm_mm_kernel(a_ref, g_ref, w_ref, *rest, rope_group, out_scale):
    if rope_group:
        cos_ref, sin_ref, o_ref, an_ref = rest
    else:
        o_ref, an_ref = rest

    @pl.when(pl.program_id(1) == 0)
    def _():
        xf = a_ref[...].astype(F32)
        ms = jnp.mean(xf * xf, axis=-1, keepdims=True)
        an_ref[...] = (xf * lax.rsqrt(ms + EPS) * g_ref[...]).astype(BF16)

    acc = jnp.dot(an_ref[...], w_ref[...], preferred_element_type=F32)
    if out_scale is not None:
        acc = acc * out_scale
    if not rope_group:
        o_ref[...] = acc.astype(o_ref.dtype)
        return
    c = cos_ref[...]
    s = sin_ref[...]
    for g0 in range(0, acc.shape[1], rope_group):
        lo = g0 + rope_group - LANES
        if lo > g0:
            o_ref[:, g0:lo] = acc[:, g0:lo].astype(o_ref.dtype)
        slab = acc[:, lo:lo + LANES]
        o_ref[:, lo:lo + LANES] = (slab * c + pltpu.roll(slab, QK_ROPE, 1) * s).astype(o_ref.dtype)


def _norm_mm(a, col_blk, k, gain, w, out_dtype, tm, tn, rope=None, out_scale=None, name=None):
    m = a.shape[0]
    n = w.shape[1]
    tm = _pick(m, tm, SUBLANES)
    tn = _pick(n, tn)
    in_specs = [
        pl.BlockSpec((tm, k), lambda i, j: (i, col_blk)),
        pl.BlockSpec((1, k), lambda i, j: (0, 0)),
        pl.BlockSpec((k, tn), lambda i, j: (0, j)),
    ]
    args = [a, gain.reshape(1, k).astype(F32), w]
    group = 0
    if rope is not None:
        cos_t, sin_t, group = rope
        nrb = cos_t.shape[0] // tm
        in_specs += [pl.BlockSpec((tm, LANES), lambda i, j: (i % nrb, 0))] * 2
        args += [cos_t, sin_t]
    return pl.pallas_call(
        functools.partial(_norm_mm_kernel, rope_group=group, out_scale=out_scale),
        grid=(m // tm, n // tn),
        in_specs=in_specs,
        out_specs=pl.BlockSpec((tm, tn), lambda i, j: (i, j)),
        out_shape=jax.ShapeDtypeStruct((m, n), out_dtype),
        scratch_shapes=[pltpu.VMEM((tm, k), BF16)],
        compiler_params=_cparams(("parallel", "arbitrary")),
        name=name,
    )(*args)


def _s5_kernel(u_ref, perm_ref, permt_ref, bw_ref, cw_ref, a_ref, g_ref, d_ref, c0_ref,
               z_ref, cout_ref, bu_ref, xb_ref, carry_ref, *, tc, ns):
    t = pl.program_id(2)
    tseg = tc // SUBLANES

    @pl.when(t == 0)
    def _():
        carry_ref[...] = c0_ref[0]

    u_p = jnp.dot(perm_ref[...], u_ref[...], preferred_element_type=F32).astype(BF16)
    bu_ref[...] = jnp.dot(u_p, bw_ref[0], preferred_element_type=F32)

    ar = jnp.broadcast_to(a_ref[0, 0:1, :], (SUBLANES, ns))
    ai = jnp.broadcast_to(a_ref[0, 1:2, :], (SUBLANES, ns))

    def local_scan(i, x):
        xr, xi = x
        r0 = pl.multiple_of(i * SUBLANES, SUBLANES)
        nr = ar * xr - ai * xi + bu_ref[pl.ds(r0, SUBLANES), 0:ns]
        ni = ar * xi + ai * xr + bu_ref[pl.ds(r0, SUBLANES), ns:2 * ns]
        bu_ref[pl.ds(r0, SUBLANES), 0:ns] = nr
        bu_ref[pl.ds(r0, SUBLANES), ns:2 * ns] = ni
        return nr, ni

    zero = jnp.zeros((SUBLANES, ns), F32)
    er, ei = lax.fori_loop(0, tseg, local_scan, (zero, zero))

    c_r = carry_ref[0:1, :]
    c_i = carry_ref[1:2, :]
    sub = lax.broadcasted_iota(jnp.int32, (SUBLANES, ns), 0)
    g1r = g_ref[0, 0:1, :]
    g1i = g_ref[0, 1:2, :]
    fr = er + jnp.where(sub == 0, g1r * c_r - g1i * c_i, 0.0)
    fi = ei + jnp.where(sub == 0, g1r * c_i + g1i * c_r, 0.0)
    for k, d in enumerate((1, 2, 4)):
        gr = g_ref[0, 2 * k:2 * k + 1, :]
        gi = g_ref[0, 2 * k + 1:2 * k + 2, :]
        sr = jnp.where(sub >= d, pltpu.roll(fr, d, 0), 0.0)
        si = jnp.where(sub >= d, pltpu.roll(fi, d, 0), 0.0)
        fr, fi = fr + gr * sr - gi * si, fi + gr * si + gi * sr
    cin_r = jnp.where(sub == 0, c_r, pltpu.roll(fr, 1, 0))
    cin_i = jnp.where(sub == 0, c_i, pltpu.roll(fi, 1, 0))
    carry_ref[0:1, :] = fr[SUBLANES - 1:SUBLANES, :]
    carry_ref[1:2, :] = fi[SUBLANES - 1:SUBLANES, :]

    def add_carry(j, c):
        cr, ci = c
        c1r, c1i = ar * cr - ai * ci, ar * ci + ai * cr
        c2r, c2i = ar * c1r - ai * c1i, ar * c1i + ai * c1r
        r0 = pl.multiple_of(j * 2 * SUBLANES, 2 * SUBLANES)
        r1 = r0 + SUBLANES
        xr = jnp.concatenate([bu_ref[pl.ds(r0, SUBLANES), 0:ns] + c1r,
                              bu_ref[pl.ds(r1, SUBLANES), 0:ns] + c2r], axis=0)
        xi = jnp.concatenate([bu_ref[pl.ds(r0, SUBLANES), ns:2 * ns] + c1i,
                              bu_ref[pl.ds(r1, SUBLANES), ns:2 * ns] + c2i], axis=0)
        xb_ref[pl.ds(r0, 2 * SUBLANES), 0:ns] = xr.astype(BF16)
        xb_ref[pl.ds(r0, 2 * SUBLANES), ns:2 * ns] = xi.astype(BF16)
        return c2r, c2i

    lax.fori_loop(0, tseg // 2, add_carry, (cin_r, cin_i))

    y = jnp.dot(xb_ref[...], cw_ref[0], preferred_element_type=F32) + d_ref[...] * u_p.astype(F32)
    zp = jax.nn.gelu(y).astype(BF16)
    z_ref[...] = jnp.dot(permt_ref[...], zp, preferred_element_type=F32).astype(BF16)

    @pl.when(t == pl.num_programs(2) - 1)
    def _():
        cout_ref[0, 0] = carry_ref[...]


def _s5_scan(u_arr, u_col0, nb, seq, tc, tabs, c0, name):
    bw, cw, a_tab, g_tab, d_skip = tabs
    ntile, _, ns2 = bw.shape
    ns = ns2 // 2
    nt = seq // tc
    tseg = tc // SUBLANES
    r = jnp.arange(tc, dtype=jnp.int32)
    src = (r % SUBLANES) * tseg + r // SUBLANES
    perm = (src[:, None] == jnp.arange(tc, dtype=jnp.int32)[None, :]).astype(BF16)
    kern = functools.partial(_s5_kernel, tc=tc, ns=ns)
    return pl.pallas_call(
        kern,
        grid=(nb, ntile, nt),
        in_specs=[
            pl.BlockSpec((tc, LANES), lambda b, n, t: (b * nt + t, u_col0 + n)),
            pl.BlockSpec((tc, tc), lambda b, n, t: (0, 0)),
            pl.BlockSpec((tc, tc), lambda b, n, t: (0, 0)),
            pl.BlockSpec((1, LANES, 2 * ns), lambda b, n, t: (n, 0, 0)),
            pl.BlockSpec((1, 2 * ns, LANES), lambda b, n, t: (n, 0, 0)),
            pl.BlockSpec((1, 2, ns), lambda b, n, t: (n, 0, 0)),
            pl.BlockSpec((1, 6, ns), lambda b, n, t: (n, 0, 0)),
            pl.BlockSpec((1, LANES), lambda b, n, t: (0, n)),
            pl.BlockSpec((1, 2, ns), lambda b, n, t: (n, 0, 0)),
        ],
        out_specs=[
            pl.BlockSpec((tc, LANES), lambda b, n, t: (b * nt + t, n)),
            pl.BlockSpec((1, 1, 2, ns), lambda b, n, t: (b, n, 0, 0)),
        ],
        out_shape=[
            jax.ShapeDtypeStruct((nb * seq, ntile * LANES), BF16),
            jax.ShapeDtypeStruct((nb, ntile, 2, ns), F32),
        ],
        scratch_shapes=[
            pltpu.VMEM((tc, 2 * ns), F32),
            pltpu.VMEM((tc, 2 * ns), BF16),
            pltpu.VMEM((2, ns), F32),
        ],
        compiler_params=_cparams(("parallel", "parallel", "arbitrary")),
        name=name,
    )(u_arr, perm, perm.T, bw, cw, a_tab, g_tab, d_skip, c0)


def _s5_tables(a_re, a_im, log_dt, b_re, b_im, c_re, c_im, d_skip):
    ngrp, nst = a_re.shape
    ntile = ngrp // S5_TILE_GROUPS
    ns = S5_TILE_GROUPS * nst
    dt = jnp.exp(log_dt.astype(F32))[:, None]
    lr, li = a_re.astype(F32), a_im.astype(F32)

    def a_pow(k):
        mag = jnp.exp(lr * dt * k)
        return mag * jnp.cos(li * dt * k), mag * jnp.sin(li * dt * k)

    ab_re, ab_im = a_pow(1)
    den = lr * lr + li * li
    f_re = ((ab_re - 1.0) * lr + ab_im * li) / den
    f_im = (ab_im * lr - (ab_re - 1.0) * li) / den
    br, bi = b_re.astype(F32), b_im.astype(F32)
    bb_re = f_re[..., None] * br - f_im[..., None] * bi
    bb_im = f_re[..., None] * bi + f_im[..., None] * br
    eye = jnp.eye(S5_TILE_GROUPS, dtype=F32)

    def b_tile(bb):
        bb = bb.reshape(ntile, S5_TILE_GROUPS, nst, S5_GROUP)
        return jnp.einsum("ab,napc->nacbp", eye, bb).reshape(ntile, LANES, ns)

    def c_tile(cc):
        cc = cc.astype(F32).reshape(ntile, S5_TILE_GROUPS, S5_GROUP, nst)
        return jnp.einsum("ab,nacp->nbpac", eye, cc).reshape(ntile, ns, LANES)

    bw = jnp.concatenate([b_tile(bb_re), b_tile(bb_im)], axis=2).astype(BF16)
    cw = jnp.concatenate([c_tile(c_re), -c_tile(c_im)], axis=1).astype(BF16)
    a_tab = jnp.stack([ab_re.reshape(ntile, ns), ab_im.reshape(ntile, ns)], axis=1)

    def g_tab(tseg):
        rows = []
        for d in (1, 2, 4):
            pr, pi = a_pow(tseg * d)
            rows += [pr.reshape(ntile, ns), pi.reshape(ntile, ns)]
        return jnp.stack(rows, axis=1)

    return bw, cw, a_tab, g_tab, d_skip.astype(F32).reshape(1, -1)


def _glu_kernel(z_ref, wa_ref, wb_ref, gs_ref, bg_ref, o_ref):
    z = z_ref[...]
    a = jnp.dot(z, wa_ref[...], preferred_element_type=F32)
    b = jnp.dot(z, wb_ref[...], preferred_element_type=F32)
    gate = jax.nn.sigmoid(gs_ref[...].astype(F32) + bg_ref[...])
    o_ref[...] = (a * jax.nn.sigmoid(b) * gate).astype(o_ref.dtype)


def _glu(z, wa, wb, proj, gs_col0, bias, tm, tn):
    m, k = z.shape
    n = wa.shape[1]
    tm = _pick(m, tm, SUBLANES)
    tn = _pick(math.gcd(n, gs_col0) if gs_col0 else n, tn)
    goff = gs_col0 // tn
    return pl.pallas_call(
        _glu_kernel,
        grid=(m // tm, n // tn),
        in_specs=[
            pl.BlockSpec((tm, k), lambda i, j: (i, 0)),
            pl.BlockSpec((k, tn), lambda i, j: (0, j)),
            pl.BlockSpec((k, tn), lambda i, j: (0, j)),
            pl.BlockSpec((tm, tn), lambda i, j: (i, goff + j)),
            pl.BlockSpec((1, tn), lambda i, j: (0, j)),
        ],
        out_specs=pl.BlockSpec((tm, tn), lambda i, j: (i, j)),
        out_shape=jax.ShapeDtypeStruct((m, n), BF16),
        compiler_params=_cparams(("parallel", "arbitrary")),
        name="glu",
    )(z, wa, wb, proj, bias)


def _attn_kernel(q_ref, kn_ref, v_ref, kr_ref, km_ref, vm_ref, krm_ref, gm_ref, bg_ref, o_ref,
                 *, tq, tk, nsub, n_meta):
    qi = pl.program_id(2)
    ts = tq // nsub
    qs = [q_ref[i * ts:(i + 1) * ts, :] for i in range(nsub)]

    def scores(q, kn, kr):
        k = jnp.concatenate([kn, kr], axis=1)
        return lax.dot_general(q, k, (((1,), (1,)), ((), ())), preferred_element_type=F32)

    def update(carry, s, v):
        m, l, acc = carry
        m_new = jnp.maximum(m, jnp.max(s, axis=1, keepdims=True))
        alpha = jnp.exp2(m - m_new)
        p = jnp.exp2(s - m_new)
        l = alpha * l + jnp.sum(p, axis=1, keepdims=True)
        acc = alpha * acc + jnp.dot(p.astype(BF16), v, preferred_element_type=F32)
        return m_new, l, acc

    km = jnp.concatenate([km_ref[...], krm_ref[...]], axis=1)
    carries = []
    for i in range(nsub):
        s = lax.dot_general(qs[i], km, (((1,), (1,)), ((), ())), preferred_element_type=F32)
        col = lax.broadcasted_iota(jnp.int32, s.shape, 1)
        s = jnp.where(col < n_meta, s, NEG_INF)
        m = jnp.max(s, axis=1, keepdims=True)
        p = jnp.exp2(s - m)
        carries.append((m, jnp.sum(p, axis=1, keepdims=True),
                        jnp.dot(p.astype(BF16), vm_ref[...], preferred_element_type=F32)))

    def step(j, carries, masked):
        r0 = pl.multiple_of(j * tk, tk)
        kn = kn_ref[pl.ds(r0, tk), :]
        kr = kr_ref[pl.ds(r0, tk), :]
        v = v_ref[pl.ds(r0, tk), :]
        out = []
        for i in range(nsub):
            s = scores(qs[i], kn, kr)
            if masked:
                row = qi * tq + i * ts + lax.broadcasted_iota(jnp.int32, s.shape, 0)
                colp = j * tk + lax.broadcasted_iota(jnp.int32, s.shape, 1)
                s = jnp.where(colp <= row, s, NEG_INF)
            out.append(update(carries[i], s, v))
        return tuple(out)

    n_full = (qi * tq) // tk
    carries = lax.fori_loop(0, n_full, lambda j, c: step(j, c, False), tuple(carries))
    for d in range(tq // tk):
        carries = step(n_full + d, carries, True)
    for i in range(nsub):
        m, l, acc = carries[i]
        gate = jax.nn.sigmoid(gm_ref[i * ts:(i + 1) * ts, :].astype(F32) + bg_ref[...])
        o_ref[i * ts:(i + 1) * ts, :] = (acc / l * gate).astype(o_ref.dtype)


def _attention(q, kv, kr, kvm, krm, proj, gm_col0, bias, nb, seq, nh, n_meta, tq, tk, nsub):
    tq = _pick(seq, tq, SUBLANES)
    tk = _pick(tq, tk, SUBLANES)
    nq = seq // tq
    mrows = kvm.shape[0]
    goff = gm_col0 // LANES
    kern = functools.partial(_attn_kernel, tq=tq, tk=tk, nsub=nsub, n_meta=n_meta)
    return pl.pallas_call(
        kern,
        grid=(nb, nh, nq),
        in_specs=[
            pl.BlockSpec((tq, HEAD_PAD), lambda b, h, i: (b * nq + i, h)),
            pl.BlockSpec((seq, LANES), lambda b, h, i: (b, 2 * h)),
            pl.BlockSpec((seq, LANES), lambda b, h, i: (b, 2 * h + 1)),
            pl.BlockSpec((seq, LANES), lambda b, h, i: (b, 0)),
            pl.BlockSpec((mrows, LANES), lambda b, h, i: (0, 2 * h)),
            pl.BlockSpec((mrows, LANES), lambda b, h, i: (0, 2 * h + 1)),
            pl.BlockSpec((mrows, LANES), lambda b, h, i: (0, 0)),
            pl.BlockSpec((tq, LANES), lambda b, h, i: (b * nq + i, goff + h)),
            pl.BlockSpec((1, LANES), lambda b, h, i: (0, h)),
        ],
        out_specs=pl.BlockSpec((tq, LANES), lambda b, h, i: (b * nq + i, h)),
        out_shape=jax.ShapeDtypeStruct((nb * seq, nh * V_HEAD), BF16),
        compiler_params=_cparams(("parallel", "parallel", "arbitrary")),
        name="attention",
    )(q, kv, kv, kr, kvm, kvm, krm, proj, bias)


def _outproj_kernel(ys_ref, ym_ref, w_ref, res_ref, o_ref, a_ref):
    @pl.when(pl.program_id(1) == 0)
    def _():
        a_ref[...] = (ys_ref[...].astype(F32) + ym_ref[...].astype(F32)).astype(BF16)

    o_ref[...] = jnp.dot(a_ref[...], w_ref[...], preferred_element_type=F32) + res_ref[...]


def _outproj(ys, ym, w, res, tm, tn):
    m, k = ys.shape
    n = w.shape[1]
    tm = _pick(m, tm, SUBLANES)
    tn = _pick(n, tn)
    return pl.pallas_call(
        _outproj_kernel,
        grid=(m // tm, n // tn),
        in_specs=[
            pl.BlockSpec((tm, k), lambda i, j: (i, 0)),
            pl.BlockSpec((tm, k), lambda i, j: (i, 0)),
            pl.BlockSpec((k, tn), lambda i, j: (0, j)),
            pl.BlockSpec((tm, tn), lambda i, j: (i, j)),
        ],
        out_specs=pl.BlockSpec((tm, tn), lambda i, j: (i, j)),
        out_shape=jax.ShapeDtypeStruct((m, n), F32),
        scratch_shapes=[pltpu.VMEM((tm, k), BF16)],
        compiler_params=_cparams(("parallel", "arbitrary")),
        name="outproj",
    )(ys, ym, w, res)


def _router_kernel(h_ref, g_ref, wr_ref, br_ref, hp_ref, idx_ref, wt_ref, rank_ref, cnt_ref,
                   run_ref, *, xr):
    @pl.when(pl.program_id(0) == 0)
    def _():
        run_ref[...] = jnp.zeros_like(run_ref)

    xf = h_ref[...]
    tm, d = xf.shape
    ms = jnp.mean(xf * xf, axis=-1, keepdims=True)
    hn = xf * lax.rsqrt(ms + EPS) * g_ref[...]

    half = d // 2
    bits = pltpu.bitcast(hn.astype(BF16).astype(F32), jnp.uint32)
    packed = (bits[:, :half] >> 16) | (bits[:, half:] & jnp.uint32(0xFFFF0000))
    for s in range(xr):
        hp_ref[pl.ds(s, tm, stride=xr), :] = packed[:, s * LANES:(s + 1) * LANES]

    logits = jnp.dot(hn, wr_ref[...], preferred_element_type=F32,
                     precision=lax.Precision.HIGHEST) + br_ref[...]
    ne = logits.shape[1]
    lane = lax.broadcasted_iota(jnp.int32, logits.shape, 1)
    vals, ids = [], []
    for _ in range(TOP_K):
        mx = jnp.max(logits, axis=1, keepdims=True)
        ix = jnp.min(jnp.where(logits == mx, lane, ne), axis=1, keepdims=True)
        vals.append(mx)
        ids.append(ix)
        logits = jnp.where(lane == ix, -jnp.inf, logits)
    v = jnp.concatenate(vals, axis=1)
    e = jnp.exp(v - v[:, 0:1])
    idx_ref[...] = jnp.concatenate(ids, axis=1)
    wt_ref[...] = e / jnp.sum(e, axis=1, keepdims=True)

    tri = (lax.broadcasted_iota(jnp.int32, (tm, tm), 0)
           > lax.broadcasted_iota(jnp.int32, (tm, tm), 1)).astype(BF16)
    base = run_ref[...]
    ranks = []
    for k in range(TOP_K):
        oh = (lane == ids[k]).astype(F32)
        prefix = jnp.dot(tri, oh.astype(BF16), preferred_element_type=F32)
        ranks.append(jnp.sum(oh * (base + prefix), axis=1, keepdims=True))
        base = base + jnp.sum(oh, axis=0, keepdims=True)
    run_ref[...] = base
    rank_ref[...] = jnp.concatenate(ranks, axis=1).astype(jnp.int32)
    cnt_ref[...] = base.astype(jnp.int32)


def _router(h, gain, w_router, b_router, tm):
    m, d = h.shape
    ne = w_router.shape[1]
    tm = _pick(m, tm, SUBLANES)
    xr = _x_slab_rows(d)
    return pl.pallas_call(
        functools.partial(_router_kernel, xr=xr),
        grid=(m // tm,),
        in_specs=[
            pl.BlockSpec((tm, d), lambda i: (i, 0)),
            pl.BlockSpec((1, d), lambda i: (0, 0)),
            pl.BlockSpec((d, ne), lambda i: (0, 0)),
            pl.BlockSpec((1, ne), lambda i: (0, 0)),
        ],
        out_specs=[
            pl.BlockSpec((tm * xr, LANES), lambda i: (i, 0)),
            pl.BlockSpec((tm, TOP_K), lambda i: (i, 0)),
            pl.BlockSpec((tm, TOP_K), lambda i: (i, 0)),
            pl.BlockSpec((tm, TOP_K), lambda i: (i, 0)),
            pl.BlockSpec((1, ne), lambda i: (0, 0)),
        ],
        out_shape=[
            jax.ShapeDtypeStruct((m * xr, LANES), jnp.uint32),
            jax.ShapeDtypeStruct((m, TOP_K), jnp.int32),
            jax.ShapeDtypeStruct((m, TOP_K), F32),
            jax.ShapeDtypeStruct((m, TOP_K), jnp.int32),
            jax.ShapeDtypeStruct((1, ne), jnp.int32),
        ],
        scratch_shapes=[pltpu.VMEM((1, ne), F32)],
        compiler_params=_cparams(("arbitrary",)),
        name="router",
    )(h, gain.reshape(1, d).astype(F32), w_router.astype(F32), b_router.reshape(1, ne).astype(F32))


def _dispatch_kernel(dest_ref, zlo_ref, zhi_ref, tot_ref, hp_hbm, xs_hbm, zbuf, sem, zsem,
                     *, per_step, npad, zrows, sr):
    i = pl.program_id(0)
    n = pl.num_programs(0)
    ne = zlo_ref.shape[0]

    def zero_row(r):
        return pltpu.make_async_copy(zbuf.at[pl.ds(0, sr)], xs_hbm.at[pl.ds(r * sr, sr)],
                                     zsem.at[0])

    def zero_chunk(c):
        return pltpu.make_async_copy(zbuf, xs_hbm.at[pl.ds(c * (zrows * sr), zrows * sr)],
                                     zsem.at[0])

    def for_zero_fills(fn_row, fn_chunk):
        def per_expert(e, _):
            lax.fori_loop(zlo_ref[e], zhi_ref[e], lambda r, _: (fn_row(r), 0)[1], 0)
            return 0
        lax.fori_loop(0, ne, per_expert, 0)
        lax.fori_loop(tot_ref[0] // zrows, npad // zrows, lambda c, _: (fn_chunk(c), 0)[1], 0)

    @pl.when(i == 0)
    def _():
        zbuf[...] = jnp.zeros_like(zbuf)
        for_zero_fills(lambda r: zero_row(r).start(), lambda c: zero_chunk(c).start())

    tok0 = i * (per_step // TOP_K)

    def body(lt, _):
        src = hp_hbm.at[pl.ds((tok0 + lt) * sr, sr)]
        for k in range(TOP_K):
            pltpu.make_async_copy(src, xs_hbm.at[pl.ds(dest_ref[i, lt * TOP_K + k] * sr, sr)],
                                  sem.at[i % 2]).start()
        return 0
    lax.fori_loop(0, per_step // TOP_K, body, 0)

    def wait_step(slot):
        pltpu.make_async_copy(hp_hbm.at[pl.ds(0, per_step * sr)],
                              xs_hbm.at[pl.ds(0, per_step * sr)], sem.at[slot]).wait()

    @pl.when(i > 0)
    def _():
        wait_step((i + 1) % 2)

    @pl.when(i == n - 1)
    def _():
        wait_step(i % 2)
        for_zero_fills(lambda r: zero_row(r).wait(), lambda c: zero_chunk(c).wait())


def _dispatch(hp, dest2d, zero_lo, zero_hi, total, npad, sr):
    nsteps, per_step = dest2d.shape
    w = hp.shape[1]
    zrows = 64
    assert npad % zrows == 0
    return pl.pallas_call(
        functools.partial(_dispatch_kernel, per_step=per_step, npad=npad, zrows=zrows, sr=sr),
        grid_spec=pltpu.PrefetchScalarGridSpec(
            num_scalar_prefetch=4,
            grid=(nsteps,),
            in_specs=[pl.BlockSpec(memory_space=pl.ANY)],
            out_specs=pl.BlockSpec(memory_space=pl.ANY),
            scratch_shapes=[pltpu.VMEM((zrows * sr, w), hp.dtype),
                            pltpu.SemaphoreType.DMA((2,)), pltpu.SemaphoreType.DMA((1,))],
        ),
        out_shape=jax.ShapeDtypeStruct((npad * sr, w), hp.dtype),
        compiler_params=_cparams(("arbitrary",)),
        name="dispatch",
    )(dest2d, zero_lo, zero_hi, total, hp)


def _moe_kernel(be_ref, na_ref, x_ref, wg_ref, wu_ref, bg_ref, bu_ref, wd_ref, bd_ref,
                o_ref, xb_ref, act_ref, *, bm, nc1, nc2, xr, yr):
    i = pl.program_id(0)
    c = pl.program_id(1)
    active = i < na_ref[0]
    tn = act_ref.shape[2]

    @pl.when(jnp.logical_and(active, c == 0))
    def _():
        w = LANES
        half = xr * w
        for s in range(xr):
            word = x_ref[pl.ds(s, bm, stride=xr), :]
            xb_ref[:, s * w:(s + 1) * w] = pltpu.bitcast(word << 16, F32).astype(BF16)
            xb_ref[:, half + s * w:half + (s + 1) * w] = pltpu.bitcast(
                word & jnp.uint32(0xFFFF0000), F32).astype(BF16)

    @pl.when(jnp.logical_and(active, c < nc1))
    def _():
        xb = xb_ref[...]
        g = jnp.dot(xb, wg_ref[0], preferred_element_type=F32) + bg_ref[0]
        u = jnp.dot(xb, wu_ref[0], preferred_element_type=F32) + bu_ref[0]
        gate = jnp.minimum(g, SWIGLU_LIMIT)
        up = jnp.clip(u, -SWIGLU_LIMIT, SWIGLU_LIMIT)
        act_ref[c] = (gate * jax.nn.sigmoid(SWIGLU_ALPHA * gate) * (up + 1.0)).astype(BF16)

    @pl.when(jnp.logical_and(active, c >= nc1))
    def _():
        y = bd_ref[0] + jnp.dot(act_ref[0], wd_ref[0, 0:tn, :], preferred_element_type=F32)
        for cc in range(1, nc1):
            y += jnp.dot(act_ref[cc], wd_ref[0, cc * tn:(cc + 1) * tn, :],
                         preferred_element_type=F32)
        half = y.shape[1] // 2
        lo = pltpu.bitcast(y[:, :half].astype(BF16).astype(F32), jnp.uint32) >> 16
        hi = pltpu.bitcast(y[:, half:].astype(BF16).astype(F32), jnp.uint32) & jnp.uint32(0xFFFF0000)
        word = lo | hi
        per = half // LANES
        for j in range(per):
            o_ref[pl.ds((c - nc1) * per + j, bm, stride=yr), :] = word[:, j * LANES:(j + 1) * LANES]

    @pl.when(jnp.logical_and(jnp.logical_not(active), c == 0))
    def _():
        o_ref[...] = jnp.zeros_like(o_ref)


def _moe(xs, blk_expert, n_active, w_gu, b_gu, w_down, b_down, bm, tn, tcol):
    w = xs.shape[1]
    ne, de, d = w_down.shape
    xr, yr = _x_slab_rows(d), _y_slab_rows(d)
    npad = xs.shape[0] // xr
    tn = _pick(de, tn)
    nc1 = de // tn
    tcol = _pick(d, tcol)
    nc2 = d // tcol
    nblk = npad // bm

    def blk(i, na):
        return jnp.minimum(i, na[0] - 1)

    def c1(c):
        return jnp.minimum(c, nc1 - 1)

    def c2(c):
        return jnp.maximum(c - nc1, 0)

    in_specs = [
        pl.BlockSpec((bm * xr, w), lambda i, c, be, na: (blk(i, na), 0)),
        pl.BlockSpec((1, d, tn), lambda i, c, be, na: (be[blk(i, na)], 0, c1(c))),
        pl.BlockSpec((1, d, tn), lambda i, c, be, na: (be[blk(i, na)], 0, nc1 + c1(c))),
        pl.BlockSpec((1, 1, tn), lambda i, c, be, na: (be[blk(i, na)], 0, c1(c))),
        pl.BlockSpec((1, 1, tn), lambda i, c, be, na: (be[blk(i, na)], 0, nc1 + c1(c))),
        pl.BlockSpec((1, de, tcol), lambda i, c, be, na: (be[blk(i, na)], 0, c2(c))),
        pl.BlockSpec((1, 1, tcol), lambda i, c, be, na: (be[blk(i, na)], 0, c2(c))),
    ]
    return pl.pallas_call(
        functools.partial(_moe_kernel, bm=bm, nc1=nc1, nc2=nc2, xr=xr, yr=yr),
        grid_spec=pltpu.PrefetchScalarGridSpec(
            num_scalar_prefetch=2,
            grid=(nblk, nc1 + nc2),
            in_specs=in_specs,
            out_specs=pl.BlockSpec((bm * yr, w), lambda i, c, be, na: (i, 0)),
            scratch_shapes=[pltpu.VMEM((bm, d), BF16), pltpu.VMEM((nc1, bm, tn), BF16)],
        ),
        out_shape=jax.ShapeDtypeStruct((npad * yr, w), jnp.uint32),
        compiler_params=_cparams(("arbitrary", "arbitrary")),
        name="moe",
    )(blk_expert, n_active, xs, w_gu, w_gu, b_gu, b_gu, w_down, b_down)


def _combine_kernel(idx_ref, y_hbm, h_ref, wt_ref, g_ref, o_ref, buf, sem, *, rows, tt, yr, per):
    i = pl.program_id(0)
    n = pl.num_programs(0)
    w = buf.shape[2]

    def issue(step, slot):
        def body(r, _):
            src = idx_ref[step, r]
            pltpu.make_async_copy(y_hbm.at[pl.ds(src * yr, yr)], buf.at[slot, pl.ds(r * yr, yr)],
                                  sem.at[slot]).start()
            return 0
        lax.fori_loop(0, rows, body, 0)

    @pl.when(i == 0)
    def _():
        issue(0, 0)

    @pl.when(i + 1 < n)
    def _():
        issue(i + 1, (i + 1) % 2)

    slot = i % 2
    pltpu.make_async_copy(y_hbm.at[pl.ds(0, rows * yr)], buf.at[slot], sem.at[slot]).wait()
    cur = buf.at[slot]
    wts = wt_ref[...]
    los, his = [], []
    for s in range(yr):
        lo = hi = None
        for k in range(TOP_K):
            word = cur[pl.ds(k * tt * yr + s, tt, stride=yr), :]
            wk = wts[:, k:k + 1]
            tl = wk * pltpu.bitcast(word << 16, F32)
            th = wk * pltpu.bitcast(word & jnp.uint32(0xFFFF0000), F32)
            lo = tl if lo is None else lo + tl
            hi = th if hi is None else hi + th
        los.append(lo)
        his.append(hi)
    pieces = []
    for n0 in range(0, yr, per):
        pieces += los[n0:n0 + per] + his[n0:n0 + per]
    acc = h_ref[...] + jnp.concatenate(pieces, axis=1)
    ms = jnp.mean(acc * acc, axis=-1, keepdims=True)
    o_ref[...] = acc * lax.rsqrt(ms + EPS) * g_ref[...]


def _combine(ys, idx2d, h, top_w, gain, tt, tcol):
    nsteps, rows = idx2d.shape
    d = h.shape[1]
    w = ys.shape[1]
    yr = _y_slab_rows(d)
    return pl.pallas_call(
        functools.partial(_combine_kernel, rows=rows, tt=tt, yr=yr, per=tcol // (2 * LANES)),
        grid_spec=pltpu.PrefetchScalarGridSpec(
            num_scalar_prefetch=1,
            grid=(nsteps,),
            in_specs=[
                pl.BlockSpec(memory_space=pl.ANY),
                pl.BlockSpec((tt, d), lambda i, idx: (i, 0)),
                pl.BlockSpec((tt, TOP_K), lambda i, idx: (i, 0)),
                pl.BlockSpec((1, d), lambda i, idx: (0, 0)),
            ],
            out_specs=pl.BlockSpec((tt, d), lambda i, idx: (i, 0)),
            scratch_shapes=[pltpu.VMEM((2, rows * yr, w), jnp.uint32),
                            pltpu.SemaphoreType.DMA((2,))],
        ),
        out_shape=jax.ShapeDtypeStruct(h.shape, F32),
        compiler_params=_cparams(("arbitrary",)),
        name="combine",
    )(idx2d, ys, h, top_w, gain.reshape(1, d).astype(F32))


def _dispatch_plan(top_idx, rank, counts, bm, nblk):
    ne = counts.shape[0]
    padded = ((counts + bm - 1) // bm) * bm
    pad_end = jnp.cumsum(padded)
    pad_start = pad_end - padded
    onehot = top_idx[..., None] == jnp.arange(ne, dtype=jnp.int32)
    dest = rank + jnp.sum(jnp.where(onehot, pad_start, 0), axis=-1)
    blk_expert = jnp.clip(
        jnp.searchsorted(pad_end, jnp.arange(nblk, dtype=jnp.int32) * bm, side="right"),
        0, ne - 1).astype(jnp.int32)
    total = pad_end[-1:].astype(jnp.int32)
    return (dest.astype(jnp.int32), blk_expert, total // bm,
            (pad_start + counts).astype(jnp.int32), pad_end.astype(jnp.int32), total)


def _rope_tables(n_pos):
    pos = jnp.arange(n_pos, dtype=F32)
    inv_freq = ROPE_THETA ** (-jnp.arange(0, QK_ROPE, 2, dtype=F32) / QK_ROPE)
    ang = pos[:, None] * inv_freq[None, :]
    cos, sin = jnp.cos(ang), jnp.sin(ang)
    zero = jnp.zeros((n_pos, LANES - QK_ROPE), F32)
    return jnp.concatenate([cos, cos, zero], axis=1), jnp.concatenate([-sin, sin, zero], axis=1)


def _with_swapped_rope(w_rope):
    half = QK_ROPE // 2
    return jnp.concatenate([w_rope, w_rope[..., half:], w_rope[..., :half]], axis=-1)


def kernel(x, meta_tokens, g_mix, w_in, b_gate, g_q_lat, g_kv_lat, w_uq, w_ukv, s5_a_re, s5_a_im, s5_log_dt, s5_b_re, s5_b_im, s5_c_re, s5_c_im, s5_d, w_glu_a, w_glu_b, w_out, g_ffn, w_router, b_router, w_gu, b_gu, w_down, b_down, g_final):
    assert w_in.shape[0] == 1, "single-layer block"
    nb, seq, d = x.shape
    t = nb * seq
    n_meta = meta_tokens.shape[0]
    ql, kvl = g_q_lat.shape[-1], g_kv_lat.shape[-1]
    sw = s5_d.shape[-1]
    nh = w_uq.shape[-1] // QK_DIM
    ne = w_router.shape[-1]
    assert ql % kvl == 0 and (ql + kvl) % LANES == 0 and sw % LANES == 0 and d % LANES == 0
    assert n_meta % (2 * SUBLANES) == 0 and n_meta <= LANES

    w0 = w_in[0]
    c_kv = ql + kvl
    c_kr = c_kv + QK_ROPE
    w_main = jnp.concatenate([w0[:, :c_kv], w0[:, c_kr:]], axis=1).astype(BF16)
    w_kr = _with_swapped_rope(w0[:, c_kv:c_kr]).astype(BF16)
    o_s5 = c_kv
    o_gs = o_s5 + sw
    o_gm = o_gs + d
    wq3 = w_uq[0].reshape(ql, nh, QK_DIM)
    w_q = jnp.concatenate([wq3[..., :QK_NOPE], _with_swapped_rope(wq3[..., QK_NOPE:])],
                          axis=-1).reshape(ql, nh * HEAD_PAD).astype(BF16)
    w_kv = w_ukv[0].astype(BF16)
    cos_t, sin_t = _rope_tables(n_meta + seq)
    rope_meta = (cos_t[:n_meta], sin_t[:n_meta])
    rope_real = (cos_t[n_meta:], sin_t[n_meta:])
    s5_tabs = _s5_tables(s5_a_re[0], s5_a_im[0], s5_log_dt[0], s5_b_re[0], s5_b_im[0],
                         s5_c_re[0], s5_c_im[0], s5_d[0])
    bw, cw, a_tab, g_tab_fn, d_skip = s5_tabs

    x2 = x.reshape(t, d)
    tm = 512

    proj_m = _norm_mm(meta_tokens, 0, d, g_mix[0], w_main, BF16, n_meta, 512, name="proj_meta")
    kr_m = _norm_mm(meta_tokens, 0, d, g_mix[0], w_kr, BF16, n_meta, LANES,
                    rope=(*rope_meta, LANES), name="krope_meta")
    kv_m = _norm_mm(proj_m, ql // kvl, kvl, g_kv_lat[0], w_kv, BF16, n_meta, 1024, name="kv_meta")
    kvm = jnp.pad(kv_m, ((0, LANES - n_meta), (0, 0)))
    krm = jnp.pad(kr_m, ((0, LANES - n_meta), (0, 0)))
    ntile = sw // LANES
    ns = bw.shape[2] // 2
    _, c_meta = _s5_scan(proj_m, o_s5 // LANES, 1, n_meta, n_meta,
                         (bw, cw, a_tab, g_tab_fn(n_meta // SUBLANES), d_skip),
                         jnp.zeros((ntile, 2, ns), F32), "s5_meta")

    proj = _norm_mm(x2, 0, d, g_mix[0], w_main, BF16, tm, 512, name="proj")
    kr = _norm_mm(x2, 0, d, g_mix[0], w_kr, BF16, tm, LANES, rope=(*rope_real, LANES), name="krope")
    q = _norm_mm(proj, 0, ql, g_q_lat[0], w_q, BF16, tm, 1024, rope=(*rope_real, HEAD_PAD),
                 out_scale=math.log2(math.e) * QK_DIM ** -0.5, name="q")
    kv = _norm_mm(proj, ql // kvl, kvl, g_kv_lat[0], w_kv, BF16, tm, 1024, name="kv")

    tc = _pick(seq, 512, 2 * SUBLANES)
    z, _ = _s5_scan(proj, o_s5 // LANES, nb, seq, tc,
                    (bw, cw, a_tab, g_tab_fn(tc // SUBLANES), d_skip), c_meta[0], "s5")
    bias = b_gate[0].astype(F32).reshape(1, 2 * d)
    ys = _glu(z, w_glu_a[0].astype(BF16), w_glu_b[0].astype(BF16), proj, o_gs, bias[:, :d], 1024, 512)
    ym = _attention(q, kv, kr, kvm, krm, proj, o_gm, bias[:, d:], nb, seq, nh, n_meta, 1024, 512, 1)
    h = _outproj(ys, ym, w_out[0].astype(BF16), x2, tm, 512)

    assert _x_slab_rows(d) % SUBLANES == 0
    hp, top_idx, top_w, rank, counts = _router(h, g_ffn[0], w_router[0], b_router[0], 256)
    bm = 512 if t * TOP_K >= 512 * ne else 128
    nblk = (t * TOP_K + ne * (bm - 1) + bm - 1) // bm
    dest, blk_expert, n_active, zero_lo, zero_hi, total = _dispatch_plan(
        top_idx, rank, counts[0], bm, nblk)
    per_step = 512
    xs = _dispatch(hp, dest.reshape(t * TOP_K // per_step, per_step), zero_lo, zero_hi, total,
                   nblk * bm, _x_slab_rows(d))
    de = w_down.shape[2]
    tcol = _pick(d, 512)
    ysort = _moe(xs, blk_expert, n_active, w_gu[0].astype(BF16), b_gu[0].reshape(ne, 1, 2 * de),
                 w_down[0].astype(BF16), b_down[0].reshape(ne, 1, d), bm, 256, tcol)
    tt = 64
    idx_c = dest.reshape(t // tt, tt, TOP_K).transpose(0, 2, 1).reshape(t // tt, TOP_K * tt)
    out = _combine(ysort, idx_c, h, top_w, g_final, tt, tcol)
    return out.reshape(nb, seq, d)
```

```python
import functools
import math

import jax
import jax.numpy as jnp
from jax import lax
from jax.experimental import pallas as pl
from jax.experimental.pallas import tpu as pltpu

F32 = jnp.float32
BF16 = jnp.bfloat16

EPS = 1e-6
QK_NOPE = 128
QK_ROPE = 64
QK_DIM = QK_NOPE + QK_ROPE
V_HEAD = 128
ROPE_THETA = 10000.0
NEG_INF = -1e30
S5_GROUP = 16
TOP_K = 4
SWIGLU_ALPHA = 1.702
SWIGLU_LIMIT = 7.0

LANES = 128
SUBLANES = 8
HEAD_PAD = 2 * LANES
S5_TILE_GROUPS = LANES // S5_GROUP
VMEM_LIMIT_V7X = 56 * 1024 * 1024


def _cparams(sem, vmem=VMEM_LIMIT_V7X):
    return pltpu.CompilerParams(dimension_semantics=sem, vmem_limit_bytes=vmem)


def _x_slab_rows(d):
    return d // (2 * LANES)


def _y_slab_rows(d):
    return d // (2 * LANES)


def _pick(n, pref, mult=LANES):
    if n <= pref:
        return n
    best = None
    for t in range(mult, pref + 1, mult):
        if n % t == 0:
            best = t
    assert best is not None, (n, pref, mult)
    return best


def _norm_mm_kernel(a_ref, g_ref, w_ref, *rest, rope_group, out_scale):
    if rope_group:
        cos_ref, sin_ref, o_ref, an_ref = rest
    else:
        o_ref, an_ref = rest

    @pl.when(pl.program_id(1) == 0)
    def _():
        xf = a_ref[...].astype(F32)
        ms = jnp.mean(xf * xf, axis=-1, keepdims=True)
        an_ref[...] = (xf * lax.rsqrt(ms + EPS) * g_ref[...]).astype(BF16)

    acc = jnp.dot(an_ref[...], w_ref[0], preferred_element_type=F32)
    if out_scale is not None:
        acc = acc * out_scale
    if not rope_group:
        o_ref[...] = acc.astype(o_ref.dtype)
        return
    c = cos_ref[...]
    s = sin_ref[...]
    for g0 in range(0, acc.shape[1], rope_group):
        lo = g0 + rope_group - LANES
        if lo > g0:
            o_ref[:, g0:lo] = acc[:, g0:lo].astype(o_ref.dtype)
        slab = acc[:, lo:lo + LANES]
        o_ref[:, lo:lo + LANES] = (slab * c + pltpu.roll(slab, QK_ROPE, 1) * s).astype(o_ref.dtype)


def _col_tiles(w, tn):
    k, n = w.shape
    tn = _pick(n, tn)
    return w.astype(BF16).reshape(k, n // tn, tn).transpose(1, 0, 2)


def _norm_mm(a, col_blk, gain, wt, out_dtype, tm, rope=None, out_scale=None, name=None):
    m = a.shape[0]
    nj, k, tn = wt.shape
    n = nj * tn
    tm = _pick(m, tm, SUBLANES)
    in_specs = [
        pl.BlockSpec((tm, k), lambda i, j: (i, col_blk)),
        pl.BlockSpec((1, k), lambda i, j: (0, 0)),
        pl.BlockSpec((1, k, tn), lambda i, j: (j, 0, 0)),
    ]
    args = [a, gain.reshape(1, k).astype(F32), wt]
    group = 0
    if rope is not None:
        cos_t, sin_t, group = rope
        nrb = cos_t.shape[0] // tm
        in_specs += [pl.BlockSpec((tm, LANES), lambda i, j: (i % nrb, 0))] * 2
        args += [cos_t, sin_t]
    return pl.pallas_call(
        functools.partial(_norm_mm_kernel, rope_group=group, out_scale=out_scale),
        grid=(m // tm, n // tn),
        in_specs=in_specs,
        out_specs=pl.BlockSpec((tm, tn), lambda i, j: (i, j)),
        out_shape=jax.ShapeDtypeStruct((m, n), out_dtype),
        scratch_shapes=[pltpu.VMEM((tm, k), BF16)],
        compiler_params=_cparams(("parallel", "arbitrary")),
        name=name,
    )(*args)


def _s5_kernel(u_ref, perm_ref, permt_ref, bw_ref, cw_ref, a_ref, g_ref, d_ref, c0_ref,
               z_ref, cout_ref, bu_ref, xb_ref, carry_ref, *, tc, ns):
    t = pl.program_id(2)
    tseg = tc // SUBLANES

    @pl.when(t == 0)
    def _():
        carry_ref[...] = c0_ref[0]

    u_p = jnp.dot(perm_ref[...], u_ref[...], preferred_element_type=F32).astype(BF16)
    bu_ref[...] = jnp.dot(u_p, bw_ref[0], preferred_element_type=F32)

    ar = jnp.broadcast_to(a_ref[0, 0:1, :], (SUBLANES, ns))
    ai = jnp.broadcast_to(a_ref[0, 1:2, :], (SUBLANES, ns))

    def local_scan(i, x):
        xr, xi = x
        r0 = pl.multiple_of(i * SUBLANES, SUBLANES)
        nr = ar * xr - ai * xi + bu_ref[pl.ds(r0, SUBLANES), 0:ns]
        ni = ar * xi + ai * xr + bu_ref[pl.ds(r0, SUBLANES), ns:2 * ns]
        bu_ref[pl.ds(r0, SUBLANES), 0:ns] = nr
        bu_ref[pl.ds(r0, SUBLANES), ns:2 * ns] = ni
        return nr, ni

    zero = jnp.zeros((SUBLANES, ns), F32)
    er, ei = lax.fori_loop(0, tseg, local_scan, (zero, zero))

    c_r = carry_ref[0:1, :]
    c_i = carry_ref[1:2, :]
    sub = lax.broadcasted_iota(jnp.int32, (SUBLANES, ns), 0)
    g1r = g_ref[0, 0:1, :]
    g1i = g_ref[0, 1:2, :]
    fr = er + jnp.where(sub == 0, g1r * c_r - g1i * c_i, 0.0)
    fi = ei + jnp.where(sub == 0, g1r * c_i + g1i * c_r, 0.0)
    for k, d in enumerate((1, 2, 4)):
        gr = g_ref[0, 2 * k:2 * k + 1, :]
        gi = g_ref[0, 2 * k + 1:2 * k + 2, :]
        sr = jnp.where(sub >= d, pltpu.roll(fr, d, 0), 0.0)
        si = jnp.where(sub >= d, pltpu.roll(fi, d, 0), 0.0)
        fr, fi = fr + gr * sr - gi * si, fi + gr * si + gi * sr
    cin_r = jnp.where(sub == 0, c_r, pltpu.roll(fr, 1, 0))
    cin_i = jnp.where(sub == 0, c_i, pltpu.roll(fi, 1, 0))
    carry_ref[0:1, :] = fr[SUBLANES - 1:SUBLANES, :]
    carry_ref[1:2, :] = fi[SUBLANES - 1:SUBLANES, :]

    def add_carry(j, c):
        cr, ci = c
        c1r, c1i = ar * cr - ai * ci, ar * ci + ai * cr
        c2r, c2i = ar * c1r - ai * c1i, ar * c1i + ai * c1r
        r0 = pl.multiple_of(j * 2 * SUBLANES, 2 * SUBLANES)
        r1 = r0 + SUBLANES
        xr = jnp.concatenate([bu_ref[pl.ds(r0, SUBLANES), 0:ns] + c1r,
                              bu_ref[pl.ds(r1, SUBLANES), 0:ns] + c2r], axis=0)
        xi = jnp.concatenate([bu_ref[pl.ds(r0, SUBLANES), ns:2 * ns] + c1i,
                              bu_ref[pl.ds(r1, SUBLANES), ns:2 * ns] + c2i], axis=0)
        xb_ref[pl.ds(r0, 2 * SUBLANES), 0:ns] = xr.astype(BF16)
        xb_ref[pl.ds(r0, 2 * SUBLANES), ns:2 * ns] = xi.astype(BF16)
        return c2r, c2i

    lax.fori_loop(0, tseg // 2, add_carry, (cin_r, cin_i))

    y = jnp.dot(xb_ref[...], cw_ref[0], preferred_element_type=F32) + d_ref[...] * u_p.astype(F32)
    zp = jax.nn.gelu(y).astype(BF16)
    z_ref[...] = jnp.dot(permt_ref[...], zp, preferred_element_type=F32).astype(BF16)

    @pl.when(t == pl.num_programs(2) - 1)
    def _():
        cout_ref[0, 0] = carry_ref[...]


def _s5_scan(u_arr, u_col0, nb, seq, tc, tabs, c0, name):
    bw, cw, a_tab, g_tab, d_skip = tabs
    ntile, _, ns2 = bw.shape
    ns = ns2 // 2
    nt = seq // tc
    tseg = tc // SUBLANES
    r = jnp.arange(tc, dtype=jnp.int32)
    src = (r % SUBLANES) * tseg + r // SUBLANES
    perm = (src[:, None] == jnp.arange(tc, dtype=jnp.int32)[None, :]).astype(BF16)
    kern = functools.partial(_s5_kernel, tc=tc, ns=ns)
    return pl.pallas_call(
        kern,
        grid=(nb, ntile, nt),
        in_specs=[
            pl.BlockSpec((tc, LANES), lambda b, n, t: (b * nt + t, u_col0 + n)),
            pl.BlockSpec((tc, tc), lambda b, n, t: (0, 0)),
            pl.BlockSpec((tc, tc), lambda b, n, t: (0, 0)),
            pl.BlockSpec((1, LANES, 2 * ns), lambda b, n, t: (n, 0, 0)),
            pl.BlockSpec((1, 2 * ns, LANES), lambda b, n, t: (n, 0, 0)),
            pl.BlockSpec((1, 2, ns), lambda b, n, t: (n, 0, 0)),
            pl.BlockSpec((1, 6, ns), lambda b, n, t: (n, 0, 0)),
            pl.BlockSpec((1, LANES), lambda b, n, t: (0, n)),
            pl.BlockSpec((1, 2, ns), lambda b, n, t: (n, 0, 0)),
        ],
        out_specs=[
            pl.BlockSpec((tc, LANES), lambda b, n, t: (b * nt + t, n)),
            pl.BlockSpec((1, 1, 2, ns), lambda b, n, t: (b, n, 0, 0)),
        ],
        out_shape=[
            jax.ShapeDtypeStruct((nb * seq, ntile * LANES), BF16),
            jax.ShapeDtypeStruct((nb, ntile, 2, ns), F32),
        ],
        scratch_shapes=[
            pltpu.VMEM((tc, 2 * ns), F32),
            pltpu.VMEM((tc, 2 * ns), BF16),
            pltpu.VMEM((2, ns), F32),
        ],
        compiler_params=_cparams(("parallel", "parallel", "arbitrary")),
        name=name,
    )(u_arr, perm, perm.T, bw, cw, a_tab, g_tab, d_skip, c0)


def _s5_tables(a_re, a_im, log_dt, b_re, b_im, c_re, c_im, d_skip):
    ngrp, nst = a_re.shape
    ntile = ngrp // S5_TILE_GROUPS
    ns = S5_TILE_GROUPS * nst
    dt = jnp.exp(log_dt.astype(F32))[:, None]
    lr, li = a_re.astype(F32), a_im.astype(F32)

    def a_pow(k):
        mag = jnp.exp(lr * dt * k)
        return mag * jnp.cos(li * dt * k), mag * jnp.sin(li * dt * k)

    ab_re, ab_im = a_pow(1)
    den = lr * lr + li * li
    f_re = ((ab_re - 1.0) * lr + ab_im * li) / den
    f_im = (ab_im * lr - (ab_re - 1.0) * li) / den
    br, bi = b_re.astype(F32), b_im.astype(F32)
    bb_re = f_re[..., None] * br - f_im[..., None] * bi
    bb_im = f_re[..., None] * bi + f_im[..., None] * br
    eye = jnp.eye(S5_TILE_GROUPS, dtype=F32)

    def b_tile(bb):
        bb = bb.reshape(ntile, S5_TILE_GROUPS, nst, S5_GROUP)
        return jnp.einsum("ab,napc->nacbp", eye, bb).reshape(ntile, LANES, ns)

    def c_tile(cc):
        cc = cc.astype(F32).reshape(ntile, S5_TILE_GROUPS, S5_GROUP, nst)
        return jnp.einsum("ab,nacp->nbpac", eye, cc).reshape(ntile, ns, LANES)

    bw = jnp.concatenate([b_tile(bb_re), b_tile(bb_im)], axis=2).astype(BF16)
    cw = jnp.concatenate([c_tile(c_re), -c_tile(c_im)], axis=1).astype(BF16)
    a_tab = jnp.stack([ab_re.reshape(ntile, ns), ab_im.reshape(ntile, ns)], axis=1)

    def g_tab(tseg):
        rows = []
        for d in (1, 2, 4):
            pr, pi = a_pow(tseg * d)
            rows += [pr.reshape(ntile, ns), pi.reshape(ntile, ns)]
        return jnp.stack(rows, axis=1)

    return bw, cw, a_tab, g_tab, d_skip.astype(F32).reshape(1, -1)


def _glu_kernel(z_ref, wa_ref, wb_ref, gs_ref, bg_ref, o_ref):
    z = z_ref[...]
    a = jnp.dot(z, wa_ref[0], preferred_element_type=F32)
    b = jnp.dot(z, wb_ref[0], preferred_element_type=F32)
    gate = jax.nn.sigmoid(gs_ref[...].astype(F32) + bg_ref[...])
    o_ref[...] = (a * jax.nn.sigmoid(b) * gate).astype(o_ref.dtype)


def _glu(z, wa, wb, proj, gs_col0, bias, tm, tn):
    m, k = z.shape
    n = wa.shape[1]
    tm = _pick(m, tm, SUBLANES)
    tn = _pick(math.gcd(n, gs_col0) if gs_col0 else n, tn)
    goff = gs_col0 // tn
    wa, wb = _col_tiles(wa, tn), _col_tiles(wb, tn)
    return pl.pallas_call(
        _glu_kernel,
        grid=(m // tm, n // tn),
        in_specs=[
            pl.BlockSpec((tm, k), lambda i, j: (i, 0)),
            pl.BlockSpec((1, k, tn), lambda i, j: (j, 0, 0)),
            pl.BlockSpec((1, k, tn), lambda i, j: (j, 0, 0)),
            pl.BlockSpec((tm, tn), lambda i, j: (i, goff + j)),
            pl.BlockSpec((1, tn), lambda i, j: (0, j)),
        ],
        out_specs=pl.BlockSpec((tm, tn), lambda i, j: (i, j)),
        out_shape=jax.ShapeDtypeStruct((m, n), BF16),
        compiler_params=_cparams(("parallel", "arbitrary")),
        name="glu",
    )(z, wa, wb, proj, bias)


def _attn_kernel(q_ref, kn_ref, v_ref, kr_ref, km_ref, vm_ref, krm_ref, gm_ref, bg_ref, o_ref,
                 *, tq, tk, nsub, n_meta):
    qi = pl.program_id(2)
    ts = tq // nsub
    qs = [q_ref[i * ts:(i + 1) * ts, :] for i in range(nsub)]

    def scores(q, kn, kr):
        k = jnp.concatenate([kn, kr], axis=1)
        return lax.dot_general(q, k, (((1,), (1,)), ((), ())), preferred_element_type=F32)

    def update(carry, s, v):
        m, l, acc = carry
        m_new = jnp.maximum(m, jnp.max(s, axis=1, keepdims=True))
        alpha = jnp.exp2(m - m_new)
        p = jnp.exp2(s - m_new)
        l = alpha * l + jnp.sum(p, axis=1, keepdims=True)
        acc = alpha * acc + jnp.dot(p.astype(BF16), v, preferred_element_type=F32)
        return m_new, l, acc

    km = jnp.concatenate([km_ref[...], krm_ref[...]], axis=1)
    carries = []
    for i in range(nsub):
        s = lax.dot_general(qs[i], km, (((1,), (1,)), ((), ())), preferred_element_type=F32)
        col = lax.broadcasted_iota(jnp.int32, s.shape, 1)
        s = jnp.where(col < n_meta, s, NEG_INF)
        m = jnp.max(s, axis=1, keepdims=True)
        p = jnp.exp2(s - m)
        carries.append((m, jnp.sum(p, axis=1, keepdims=True),
                        jnp.dot(p.astype(BF16), vm_ref[...], preferred_element_type=F32)))

    def step(j, carries, masked):
        r0 = pl.multiple_of(j * tk, tk)
        kn = kn_ref[pl.ds(r0, tk), :]
        kr = kr_ref[pl.ds(r0, tk), :]
        v = v_ref[pl.ds(r0, tk), :]
        out = []
        for i in range(nsub):
            s = scores(qs[i], kn, kr)
            if masked:
                row = qi * tq + i * ts + lax.broadcasted_iota(jnp.int32, s.shape, 0)
                colp = j * tk + lax.broadcasted_iota(jnp.int32, s.shape, 1)
                s = jnp.where(colp <= row, s, NEG_INF)
            out.append(update(carries[i], s, v))
        return tuple(out)

    n_full = (qi * tq) // tk
    carries = lax.fori_loop(0, n_full, lambda j, c: step(j, c, False), tuple(carries))
    for d in range(tq // tk):
        carries = step(n_full + d, carries, True)
    for i in range(nsub):
        m, l, acc = carries[i]
        gate = jax.nn.sigmoid(gm_ref[i * ts:(i + 1) * ts, :].astype(F32) + bg_ref[...])
        o_ref[i * ts:(i + 1) * ts, :] = (acc / l * gate).astype(o_ref.dtype)


def _attention(q, kv, kr, kvm, krm, proj, gm_col0, bias, nb, seq, nh, n_meta, tq, tk, nsub):
    tq = _pick(seq, tq, SUBLANES)
    tk = _pick(tq, tk, SUBLANES)
    nq = seq // tq
    mrows = kvm.shape[0]
    goff = gm_col0 // LANES
    kern = functools.partial(_attn_kernel, tq=tq, tk=tk, nsub=nsub, n_meta=n_meta)
    return pl.pallas_call(
        kern,
        grid=(nb, nh, nq),
        in_specs=[
            pl.BlockSpec((tq, HEAD_PAD), lambda b, h, i: (b * nq + i, h)),
            pl.BlockSpec((seq, LANES), lambda b, h, i: (b, 2 * h)),
            pl.BlockSpec((seq, LANES), lambda b, h, i: (b, 2 * h + 1)),
            pl.BlockSpec((seq, LANES), lambda b, h, i: (b, 0)),
            pl.BlockSpec((mrows, LANES), lambda b, h, i: (0, 2 * h)),
            pl.BlockSpec((mrows, LANES), lambda b, h, i: (0, 2 * h + 1)),
            pl.BlockSpec((mrows, LANES), lambda b, h, i: (0, 0)),
            pl.BlockSpec((tq, LANES), lambda b, h, i: (b * nq + i, goff + h)),
            pl.BlockSpec((1, LANES), lambda b, h, i: (0, h)),
        ],
        out_specs=pl.BlockSpec((tq, LANES), lambda b, h, i: (b * nq + i, h)),
        out_shape=jax.ShapeDtypeStruct((nb * seq, nh * V_HEAD), BF16),
        compiler_params=_cparams(("parallel", "parallel", "arbitrary")),
        name="attention",
    )(q, kv, kv, kr, kvm, kvm, krm, proj, bias)


def _outproj_kernel(ys_ref, ym_ref, w_ref, res_ref, o_ref, a_ref):
    @pl.when(pl.program_id(1) == 0)
    def _():
        a_ref[...] = (ys_ref[...].astype(F32) + ym_ref[...].astype(F32)).astype(BF16)

    o_ref[...] = jnp.dot(a_ref[...], w_ref[0], preferred_element_type=F32) + res_ref[...]


def _outproj(ys, ym, w, res, tm, tn):
    m, k = ys.shape
    n = w.shape[1]
    tm = _pick(m, tm, SUBLANES)
    tn = _pick(n, tn)
    w = _col_tiles(w, tn)
    return pl.pallas_call(
        _outproj_kernel,
        grid=(m // tm, n // tn),
        in_specs=[
            pl.BlockSpec((tm, k), lambda i, j: (i, 0)),
            pl.BlockSpec((tm, k), lambda i, j: (i, 0)),
            pl.BlockSpec((1, k, tn), lambda i, j: (j, 0, 0)),
            pl.BlockSpec((tm, tn), lambda i, j: (i, j)),
        ],
        out_specs=pl.BlockSpec((tm, tn), lambda i, j: (i, j)),
        out_shape=jax.ShapeDtypeStruct((m, n), F32),
        scratch_shapes=[pltpu.VMEM((tm, k), BF16)],
        compiler_params=_cparams(("parallel", "arbitrary")),
        name="outproj",
    )(ys, ym, w, res)


def _router_kernel(h_ref, g_ref, wr_ref, br_ref, hp_ref, idx_ref, wt_ref, rank_ref, cnt_ref,
                   run_ref, *, xr):
    @pl.when(pl.program_id(0) == 0)
    def _():
        run_ref[...] = jnp.zeros_like(run_ref)

    xf = h_ref[...]
    tm, d = xf.shape
    ms = jnp.mean(xf * xf, axis=-1, keepdims=True)
    hn = xf * lax.rsqrt(ms + EPS) * g_ref[...]

    half = d // 2
    bits = pltpu.bitcast(hn.astype(BF16).astype(F32), jnp.uint32)
    packed = (bits[:, :half] >> 16) | (bits[:, half:] & jnp.uint32(0xFFFF0000))
    for s in range(xr):
        hp_ref[pl.ds(s, tm, stride=xr), :] = packed[:, s * LANES:(s + 1) * LANES]

    logits = jnp.dot(hn, wr_ref[...], preferred_element_type=F32,
                     precision=lax.Precision.HIGHEST) + br_ref[...]
    ne = logits.shape[1]
    lane = lax.broadcasted_iota(jnp.int32, logits.shape, 1)
    vals, ids = [], []
    for _ in range(TOP_K):
        mx = jnp.max(logits, axis=1, keepdims=True)
        ix = jnp.min(jnp.where(logits == mx, lane, ne), axis=1, keepdims=True)
        vals.append(mx)
        ids.append(ix)
        logits = jnp.where(lane == ix, -jnp.inf, logits)
    v = jnp.concatenate(vals, axis=1)
    e = jnp.exp(v - v[:, 0:1])
    idx_ref[...] = jnp.concatenate(ids, axis=1)
    wt_ref[...] = e / jnp.sum(e, axis=1, keepdims=True)

    tri = (lax.broadcasted_iota(jnp.int32, (tm, tm), 0)
           > lax.broadcasted_iota(jnp.int32, (tm, tm), 1)).astype(BF16)
    base = run_ref[...]
    ranks = []
    for k in range(TOP_K):
        oh = (lane == ids[k]).astype(F32)
        prefix = jnp.dot(tri, oh.astype(BF16), preferred_element_type=F32)
        ranks.append(jnp.sum(oh * (base + prefix), axis=1, keepdims=True))
        base = base + jnp.sum(oh, axis=0, keepdims=True)
    run_ref[...] = base
    rank_ref[...] = jnp.concatenate(ranks, axis=1).astype(jnp.int32)
    cnt_ref[...] = base.astype(jnp.int32)


def _router(h, gain, w_router, b_router, tm):
    m, d = h.shape
    ne = w_router.shape[1]
    tm = _pick(m, tm, SUBLANES)
    xr = _x_slab_rows(d)
    return pl.pallas_call(
        functools.partial(_router_kernel, xr=xr),
        grid=(m // tm,),
        in_specs=[
            pl.BlockSpec((tm, d), lambda i: (i, 0)),
            pl.BlockSpec((1, d), lambda i: (0, 0)),
            pl.BlockSpec((d, ne), lambda i: (0, 0)),
            pl.BlockSpec((1, ne), lambda i: (0, 0)),
        ],
        out_specs=[
            pl.BlockSpec((tm * xr, LANES), lambda i: (i, 0)),
            pl.BlockSpec((tm, TOP_K), lambda i: (i, 0)),
            pl.BlockSpec((tm, TOP_K), lambda i: (i, 0)),
            pl.BlockSpec((tm, TOP_K), lambda i: (i, 0)),
            pl.BlockSpec((1, ne), lambda i: (0, 0)),
        ],
        out_shape=[
            jax.ShapeDtypeStruct((m * xr, LANES), jnp.uint32),
            jax.ShapeDtypeStruct((m, TOP_K), jnp.int32),
            jax.ShapeDtypeStruct((m, TOP_K), F32),
            jax.ShapeDtypeStruct((m, TOP_K), jnp.int32),
            jax.ShapeDtypeStruct((1, ne), jnp.int32),
        ],
        scratch_shapes=[pltpu.VMEM((1, ne), F32)],
        compiler_params=_cparams(("arbitrary",)),
        name="router",
    )(h, gain.reshape(1, d).astype(F32), w_router.astype(F32), b_router.reshape(1, ne).astype(F32))


def _dispatch_kernel(dest_ref, zlo_ref, zhi_ref, tot_ref, hp_ref, xs_hbm, stage, zbuf, sem, zsem,
                     *, per_step, npad, zrows, sr):
    i = pl.program_id(0)
    n = pl.num_programs(0)
    ne = zlo_ref.shape[0]
    tt = per_step // TOP_K

    def zero_row(r):
        return pltpu.make_async_copy(zbuf.at[pl.ds(0, sr)], xs_hbm.at[pl.ds(r * sr, sr)],
                                     zsem.at[0])

    def zero_chunk(c):
        return pltpu.make_async_copy(zbuf, xs_hbm.at[pl.ds(c * (zrows * sr), zrows * sr)],
                                     zsem.at[0])

    def for_zero_fills(fn_row, fn_chunk):
        def per_expert(e, _):
            lax.fori_loop(zlo_ref[e], zhi_ref[e], lambda r, _: (fn_row(r), 0)[1], 0)
            return 0
        lax.fori_loop(0, ne, per_expert, 0)
        lax.fori_loop(tot_ref[0] // zrows, npad // zrows, lambda c, _: (fn_chunk(c), 0)[1], 0)

    @pl.when(i == 0)
    def _():
        zbuf[...] = jnp.zeros_like(zbuf)
        for_zero_fills(lambda r: zero_row(r).start(), lambda c: zero_chunk(c).start())

    slot = i % 2
    stage[slot] = hp_ref[...]

    def body(lt, _):
        src = stage.at[slot, pl.ds(lt * sr, sr)]
        for k in range(TOP_K):
            pltpu.make_async_copy(src, xs_hbm.at[pl.ds(dest_ref[i, lt * TOP_K + k] * sr, sr)],
                                  sem.at[slot]).start(priority=k % 2)
        return 0
    lax.fori_loop(0, tt, body, 0)

    def wait_step(s):
        for _ in range(TOP_K):
            pltpu.make_async_copy(stage.at[s], xs_hbm.at[pl.ds(0, tt * sr)], sem.at[s]).wait()

    @pl.when(i > 0)
    def _():
        wait_step((i + 1) % 2)

    @pl.when(i == n - 1)
    def _():
        wait_step(i % 2)
        for_zero_fills(lambda r: zero_row(r).wait(), lambda c: zero_chunk(c).wait())


def _dispatch(hp, dest2d, zero_lo, zero_hi, total, npad, sr):
    nsteps, per_step = dest2d.shape
    w = hp.shape[1]
    zrows = 64
    assert npad % zrows == 0
    return pl.pallas_call(
        functools.partial(_dispatch_kernel, per_step=per_step, npad=npad, zrows=zrows, sr=sr),
        grid_spec=pltpu.PrefetchScalarGridSpec(
            num_scalar_prefetch=4,
            grid=(nsteps,),
            in_specs=[pl.BlockSpec((per_step // TOP_K * sr, w), lambda i, *_: (i, 0))],
            out_specs=pl.BlockSpec(memory_space=pl.ANY),
            scratch_shapes=[pltpu.VMEM((2, per_step // TOP_K * sr, w), hp.dtype),
                            pltpu.VMEM((zrows * sr, w), hp.dtype),
                            pltpu.SemaphoreType.DMA((2,)), pltpu.SemaphoreType.DMA((1,))],
        ),
        out_shape=jax.ShapeDtypeStruct((npad * sr, w), hp.dtype),
        compiler_params=_cparams(("arbitrary",)),
        name="dispatch",
    )(dest2d, zero_lo, zero_hi, total, hp)


def _moe_kernel(be_ref, na_ref, x_ref, wgu_ref, bgu_ref, wd_ref, bd_ref,
                o_ref, xb_ref, act_ref, *, bm, nc1, nc2, xr, yr):
    i = pl.program_id(0)
    c = pl.program_id(1)
    active = i < na_ref[0]
    tn = act_ref.shape[2]

    @pl.when(jnp.logical_and(active, c == 0))
    def _():
        w = LANES
        half = xr * w
        for s in range(xr):
            word = x_ref[pl.ds(s, bm, stride=xr), :]
            xb_ref[:, s * w:(s + 1) * w] = pltpu.bitcast(word << 16, F32).astype(BF16)
            xb_ref[:, half + s * w:half + (s + 1) * w] = pltpu.bitcast(
                word & jnp.uint32(0xFFFF0000), F32).astype(BF16)

    @pl.when(jnp.logical_and(active, c < nc1))
    def _():
        gu = jnp.dot(xb_ref[...], wgu_ref[0, 0], preferred_element_type=F32) + bgu_ref[0, 0]
        gate = jnp.minimum(gu[:, :tn], SWIGLU_LIMIT)
        up = jnp.clip(gu[:, tn:], -SWIGLU_LIMIT, SWIGLU_LIMIT)
        act_ref[c] = (gate * jax.nn.sigmoid(SWIGLU_ALPHA * gate) * (up + 1.0)).astype(BF16)

    @pl.when(jnp.logical_and(active, c >= nc1))
    def _():
        y = bd_ref[0, 0] + jnp.dot(act_ref[0], wd_ref[0, 0, 0:tn, :], preferred_element_type=F32)
        for cc in range(1, nc1):
            y += jnp.dot(act_ref[cc], wd_ref[0, 0, cc * tn:(cc + 1) * tn, :],
                         preferred_element_type=F32)
        half = y.shape[1] // 2
        lo = pltpu.bitcast(y[:, :half].astype(BF16).astype(F32), jnp.uint32) >> 16
        hi = pltpu.bitcast(y[:, half:].astype(BF16).astype(F32), jnp.uint32) & jnp.uint32(0xFFFF0000)
        word = lo | hi
        per = half // LANES
        for j in range(per):
            o_ref[pl.ds((c - nc1) * per + j, bm, stride=yr), :] = word[:, j * LANES:(j + 1) * LANES]

    @pl.when(jnp.logical_and(jnp.logical_not(active), c == 0))
    def _():
        o_ref[...] = jnp.zeros_like(o_ref)


def _moe_weights(w_gu, b_gu, w_down, b_down, tn, tcol):
    ne, d, de2 = w_gu.shape
    de = de2 // 2
    nc1, nc2 = de // tn, d // tcol

    def pair(a):
        r = a.shape[1]
        a = a.reshape(ne, r, 2, nc1, tn).transpose(0, 3, 1, 2, 4)
        return a.reshape(ne, nc1, r, 2 * tn)

    wgu = pair(w_gu.astype(BF16))
    bgu = pair(b_gu.astype(F32).reshape(ne, 1, de2))
    wd = w_down.astype(BF16).reshape(ne, de, nc2, tcol).transpose(0, 2, 1, 3)
    bd = b_down.astype(F32).reshape(ne, nc2, 1, tcol)
    return wgu, bgu, wd, bd


def _moe(xs, blk_expert, n_active, wgu, bgu, wd, bd, bm):
    w = xs.shape[1]
    ne, nc1, d, tn2 = wgu.shape
    tn = tn2 // 2
    _, nc2, de, tcol = wd.shape
    xr, yr = _x_slab_rows(d), _y_slab_rows(d)
    npad = xs.shape[0] // xr
    nblk = npad // bm

    def blk(i, na):
        return jnp.minimum(i, na[0] - 1)

    def c1(c):
        return jnp.minimum(c, nc1 - 1)

    def c2(c):
        return jnp.maximum(c - nc1, 0)

    in_specs = [
        pl.BlockSpec((bm * xr, w), lambda i, c, be, na: (blk(i, na), 0)),
        pl.BlockSpec((1, 1, d, tn2), lambda i, c, be, na: (be[blk(i, na)], c1(c), 0, 0)),
        pl.BlockSpec((1, 1, 1, tn2), lambda i, c, be, na: (be[blk(i, na)], c1(c), 0, 0)),
        pl.BlockSpec((1, 1, de, tcol), lambda i, c, be, na: (be[blk(i, na)], c2(c), 0, 0)),
        pl.BlockSpec((1, 1, 1, tcol), lambda i, c, be, na: (be[blk(i, na)], c2(c), 0, 0)),
    ]
    return pl.pallas_call(
        functools.partial(_moe_kernel, bm=bm, nc1=nc1, nc2=nc2, xr=xr, yr=yr),
        grid_spec=pltpu.PrefetchScalarGridSpec(
            num_scalar_prefetch=2,
            grid=(nblk, nc1 + nc2),
            in_specs=in_specs,
            out_specs=pl.BlockSpec((bm * yr, w), lambda i, c, be, na: (i, 0)),
            scratch_shapes=[pltpu.VMEM((bm, d), BF16), pltpu.VMEM((nc1, bm, tn), BF16)],
        ),
        out_shape=jax.ShapeDtypeStruct((npad * yr, w), jnp.uint32),
        compiler_params=_cparams(("arbitrary", "arbitrary")),
        name="moe",
    )(blk_expert, n_active, xs, wgu, bgu, wd, bd)


def _combine_kernel(idx_ref, y_hbm, h_ref, wt_ref, g_ref, o_ref, buf, sem, *, rows, tt, yr, per):
    i = pl.program_id(0)
    n = pl.num_programs(0)
    w = buf.shape[2]

    def issue(step, slot):
        def body(r4, _):
            for j in range(4):
                r = r4 * 4 + j
                src = idx_ref[step, r]
                pltpu.make_async_copy(y_hbm.at[pl.ds(src * yr, yr)],
                                      buf.at[slot, pl.ds(r * yr, yr)],
                                      sem.at[slot]).start(priority=j % 2)
            return 0
        lax.fori_loop(0, rows // 4, body, 0)

    @pl.when(i == 0)
    def _():
        issue(0, 0)

    @pl.when(i + 1 < n)
    def _():
        issue(i + 1, (i + 1) % 2)

    slot = i % 2
    pltpu.make_async_copy(y_hbm.at[pl.ds(0, rows * yr)], buf.at[slot], sem.at[slot]).wait()
    cur = buf.at[slot]
    wts = wt_ref[...]
    los, his = [], []
    for s in range(yr):
        lo = hi = None
        for k in range(TOP_K):
            word = cur[pl.ds(k * tt * yr + s, tt, stride=yr), :]
            wk = wts[:, k:k + 1]
            tl = wk * pltpu.bitcast(word << 16, F32)
            th = wk * pltpu.bitcast(word & jnp.uint32(0xFFFF0000), F32)
            lo = tl if lo is None else lo + tl
            hi = th if hi is None else hi + th
        los.append(lo)
        his.append(hi)
    pieces = []
    for n0 in range(0, yr, per):
        pieces += los[n0:n0 + per] + his[n0:n0 + per]
    acc = h_ref[...] + jnp.concatenate(pieces, axis=1)
    ms = jnp.mean(acc * acc, axis=-1, keepdims=True)
    o_ref[...] = acc * lax.rsqrt(ms + EPS) * g_ref[...]


def _combine(ys, idx2d, h, top_w, gain, tt, tcol):
    nsteps, rows = idx2d.shape
    d = h.shape[1]
    w = ys.shape[1]
    yr = _y_slab_rows(d)
    return pl.pallas_call(
        functools.partial(_combine_kernel, rows=rows, tt=tt, yr=yr, per=tcol // (2 * LANES)),
        grid_spec=pltpu.PrefetchScalarGridSpec(
            num_scalar_prefetch=1,
            grid=(nsteps,),
            in_specs=[
                pl.BlockSpec(memory_space=pl.ANY),
                pl.BlockSpec((tt, d), lambda i, idx: (i, 0)),
                pl.BlockSpec((tt, TOP_K), lambda i, idx: (i, 0)),
                pl.BlockSpec((1, d), lambda i, idx: (0, 0)),
            ],
            out_specs=pl.BlockSpec((tt, d), lambda i, idx: (i, 0)),
            scratch_shapes=[pltpu.VMEM((2, rows * yr, w), jnp.uint32),
                            pltpu.SemaphoreType.DMA((2,))],
        ),
        out_shape=jax.ShapeDtypeStruct(h.shape, F32),
        compiler_params=_cparams(("arbitrary",)),
        name="combine",
    )(idx2d, ys, h, top_w, gain.reshape(1, d).astype(F32))


def _dispatch_plan(top_idx, rank, counts, bm, nblk):
    ne = counts.shape[0]
    padded = ((counts + bm - 1) // bm) * bm
    pad_end = jnp.cumsum(padded)
    pad_start = pad_end - padded
    onehot = top_idx[..., None] == jnp.arange(ne, dtype=jnp.int32)
    dest = rank + jnp.sum(jnp.where(onehot, pad_start, 0), axis=-1)
    blk_expert = jnp.clip(
        jnp.searchsorted(pad_end, jnp.arange(nblk, dtype=jnp.int32) * bm, side="right"),
        0, ne - 1).astype(jnp.int32)
    total = pad_end[-1:].astype(jnp.int32)
    return (dest.astype(jnp.int32), blk_expert, total // bm,
            (pad_start + counts).astype(jnp.int32), pad_end.astype(jnp.int32), total)


def _rope_tables(n_pos):
    pos = jnp.arange(n_pos, dtype=F32)
    inv_freq = ROPE_THETA ** (-jnp.arange(0, QK_ROPE, 2, dtype=F32) / QK_ROPE)
    ang = pos[:, None] * inv_freq[None, :]
    cos, sin = jnp.cos(ang), jnp.sin(ang)
    zero = jnp.zeros((n_pos, LANES - QK_ROPE), F32)
    return jnp.concatenate([cos, cos, zero], axis=1), jnp.concatenate([-sin, sin, zero], axis=1)


def _with_swapped_rope(w_rope):
    half = QK_ROPE // 2
    return jnp.concatenate([w_rope, w_rope[..., half:], w_rope[..., :half]], axis=-1)


def kernel(x, meta_tokens, g_mix, w_in, b_gate, g_q_lat, g_kv_lat, w_uq, w_ukv, s5_a_re, s5_a_im, s5_log_dt, s5_b_re, s5_b_im, s5_c_re, s5_c_im, s5_d, w_glu_a, w_glu_b, w_out, g_ffn, w_router, b_router, w_gu, b_gu, w_down, b_down, g_final):
    assert w_in.shape[0] == 1, "single-layer block"
    nb, seq, d = x.shape
    t = nb * seq
    n_meta = meta_tokens.shape[0]
    ql, kvl = g_q_lat.shape[-1], g_kv_lat.shape[-1]
    sw = s5_d.shape[-1]
    nh = w_uq.shape[-1] // QK_DIM
    ne = w_router.shape[-1]
    assert ql % kvl == 0 and (ql + kvl) % LANES == 0 and sw % LANES == 0 and d % LANES == 0
    assert n_meta % (2 * SUBLANES) == 0 and n_meta <= LANES

    w0 = w_in[0]
    c_kv = ql + kvl
    c_kr = c_kv + QK_ROPE
    w_main = _col_tiles(jnp.concatenate([w0[:, :c_kv], w0[:, c_kr:]], axis=1), 512)
    w_kr = _col_tiles(_with_swapped_rope(w0[:, c_kv:c_kr]), LANES)
    o_s5 = c_kv
    o_gs = o_s5 + sw
    o_gm = o_gs + d
    wq3 = w_uq[0].reshape(ql, nh, QK_DIM)
    w_q = _col_tiles(jnp.concatenate([wq3[..., :QK_NOPE], _with_swapped_rope(wq3[..., QK_NOPE:])],
                                     axis=-1).reshape(ql, nh * HEAD_PAD), 1024)
    w_kv = _col_tiles(w_ukv[0], 1024)
    cos_t, sin_t = _rope_tables(n_meta + seq)
    rope_meta = (cos_t[:n_meta], sin_t[:n_meta])
    rope_real = (cos_t[n_meta:], sin_t[n_meta:])
    s5_tabs = _s5_tables(s5_a_re[0], s5_a_im[0], s5_log_dt[0], s5_b_re[0], s5_b_im[0],
                         s5_c_re[0], s5_c_im[0], s5_d[0])
    bw, cw, a_tab, g_tab_fn, d_skip = s5_tabs

    x2 = x.reshape(t, d)
    tm = 512

    proj_m = _norm_mm(meta_tokens, 0, g_mix[0], w_main, BF16, n_meta, name="proj_meta")
    kr_m = _norm_mm(meta_tokens, 0, g_mix[0], w_kr, BF16, n_meta,
                    rope=(*rope_meta, LANES), name="krope_meta")
    kv_m = _norm_mm(proj_m, ql // kvl, g_kv_lat[0], w_kv, BF16, n_meta, name="kv_meta")
    kvm = jnp.pad(kv_m, ((0, LANES - n_meta), (0, 0)))
    krm = jnp.pad(kr_m, ((0, LANES - n_meta), (0, 0)))
    ntile = sw // LANES
    ns = bw.shape[2] // 2
    _, c_meta = _s5_scan(proj_m, o_s5 // LANES, 1, n_meta, n_meta,
                         (bw, cw, a_tab, g_tab_fn(n_meta // SUBLANES), d_skip),
                         jnp.zeros((ntile, 2, ns), F32), "s5_meta")

    proj = _norm_mm(x2, 0, g_mix[0], w_main, BF16, tm, name="proj")
    kr = _norm_mm(x2, 0, g_mix[0], w_kr, BF16, tm, rope=(*rope_real, LANES), name="krope")
    q = _norm_mm(proj, 0, g_q_lat[0], w_q, BF16, tm, rope=(*rope_real, HEAD_PAD),
                 out_scale=math.log2(math.e) * QK_DIM ** -0.5, name="q")
    kv = _norm_mm(proj, ql // kvl, g_kv_lat[0], w_kv, BF16, tm, name="kv")

    tc = _pick(seq, 512, 2 * SUBLANES)
    z, _ = _s5_scan(proj, o_s5 // LANES, nb, seq, tc,
                    (bw, cw, a_tab, g_tab_fn(tc // SUBLANES), d_skip), c_meta[0], "s5")
    bias = b_gate[0].astype(F32).reshape(1, 2 * d)
    ys = _glu(z, w_glu_a[0], w_glu_b[0], proj, o_gs, bias[:, :d], 1024, 512)
    ym = _attention(q, kv, kr, kvm, krm, proj, o_gm, bias[:, d:], nb, seq, nh, n_meta, 1024, 512, 1)
    h = _outproj(ys, ym, w_out[0], x2, tm, 512)

    assert _x_slab_rows(d) % SUBLANES == 0
    hp, top_idx, top_w, rank, counts = _router(h, g_ffn[0], w_router[0], b_router[0], 256)
    bm = 512 if t * TOP_K >= 512 * ne else 128
    nblk = (t * TOP_K + ne * (bm - 1) + bm - 1) // bm
    dest, blk_expert, n_active, zero_lo, zero_hi, total = _dispatch_plan(
        top_idx, rank, counts[0], bm, nblk)
    per_step = 512
    xs = _dispatch(hp, dest.reshape(t * TOP_K // per_step, per_step), zero_lo, zero_hi, total,
                   nblk * bm, _x_slab_rows(d))
    tcol = _pick(d, 512)
    wgu, bgu, wd, bd = _moe_weights(w_gu[0], b_gu[0], w_down[0], b_down[0],
                                    _pick(w_down.shape[2], 256), tcol)
    ysort = _moe(xs, blk_expert, n_active, wgu, bgu, wd, bd, bm)
    tt = 64
    idx_c = dest.reshape(t // tt, tt, TOP_K).transpose(0, 2, 1).reshape(t // tt, TOP_K * tt)
    out = _combine(ysort, idx_c, h, top_w, g_final, tt, tcol)
    return out.reshape(nb, seq, d)
```

```python
import functools
import math

import jax
import jax.numpy as jnp
from jax import lax
from jax.experimental import pallas as pl
from jax.experimental.pallas import tpu as pltpu

F32 = jnp.float32
BF16 = jnp.bfloat16

EPS = 1e-6
QK_NOPE = 128
QK_ROPE = 64
QK_DIM = QK_NOPE + QK_ROPE
V_HEAD = 128
ROPE_THETA = 10000.0
NEG_INF = -1e30
S5_GROUP = 16
TOP_K = 4
SWIGLU_ALPHA = 1.702
SWIGLU_LIMIT = 7.0

LANES = 128
SUBLANES = 8
HEAD_PAD = 2 * LANES
S5_TILE_GROUPS = LANES // S5_GROUP
VMEM_LIMIT_V7X = 56 * 1024 * 1024


def _cparams(sem, vmem=VMEM_LIMIT_V7X):
    return pltpu.CompilerParams(dimension_semantics=sem, vmem_limit_bytes=vmem)


def _x_slab_rows(d):
    return d // (2 * LANES)


def _y_slab_rows(d):
    return d // (2 * LANES)


def _pick(n, pref, mult=LANES):
    if n <= pref:
        return n
    best = None
    for t in range(mult, pref + 1, mult):
        if n % t == 0:
            best = t
    assert best is not None, (n, pref, mult)
    return best


def _norm_mm_kernel(a_ref, g_ref, w_ref, *rest, rope_group, out_scale):
    if rope_group:
        cos_ref, sin_ref, o_ref, an_ref = rest
    else:
        o_ref, an_ref = rest

    @pl.when(pl.program_id(1) == 0)
    def _():
        xf = a_ref[...].astype(F32)
        ms = jnp.mean(xf * xf, axis=-1, keepdims=True)
        an_ref[...] = (xf * lax.rsqrt(ms + EPS) * g_ref[...]).astype(BF16)

    acc = jnp.dot(an_ref[...], w_ref[...], preferred_element_type=F32)
    if out_scale is not None:
        acc = acc * out_scale
    if not rope_group:
        o_ref[...] = acc.astype(o_ref.dtype)
        return
    c = cos_ref[...]
    s = sin_ref[...]
    for g0 in range(0, acc.shape[1], rope_group):
        lo = g0 + rope_group - LANES
        if lo > g0:
            o_ref[:, g0:lo] = acc[:, g0:lo].astype(o_ref.dtype)
        slab = acc[:, lo:lo + LANES]
        o_ref[:, lo:lo + LANES] = (slab * c + pltpu.roll(slab, QK_ROPE, 1) * s).astype(o_ref.dtype)


def _norm_mm(a, col_blk, gain, w, out_dtype, tm, tn, rope=None, out_scale=None, name=None):
    m = a.shape[0]
    k, n = w.shape
    tm = _pick(m, tm, SUBLANES)
    tn = _pick(n, tn)
    in_specs = [
        pl.BlockSpec((tm, k), lambda i, j: (i, col_blk)),
        pl.BlockSpec((1, k), lambda i, j: (0, 0)),
        pl.BlockSpec((k, tn), lambda i, j: (0, j)),
    ]
    args = [a, gain.reshape(1, k).astype(F32), w]
    group = 0
    if rope is not None:
        cos_t, sin_t, group = rope
        nrb = cos_t.shape[0] // tm
        in_specs += [pl.BlockSpec((tm, LANES), lambda i, j: (i % nrb, 0))] * 2
        args += [cos_t, sin_t]
    return pl.pallas_call(
        functools.partial(_norm_mm_kernel, rope_group=group, out_scale=out_scale),
        grid=(m // tm, n // tn),
        in_specs=in_specs,
        out_specs=pl.BlockSpec((tm, tn), lambda i, j: (i, j)),
        out_shape=jax.ShapeDtypeStruct((m, n), out_dtype),
        scratch_shapes=[pltpu.VMEM((tm, k), BF16)],
        compiler_params=_cparams(("parallel", "arbitrary")),
        name=name,
    )(*args)


def _s5_kernel(u_ref, perm_ref, permt_ref, bw_ref, cw_ref, a_ref, g_ref, d_ref, c0_ref,
               z_ref, cout_ref, bu_ref, xb_ref, carry_ref, *, tc, ns):
    t = pl.program_id(2)
    tseg = tc // SUBLANES

    @pl.when(t == 0)
    def _():
        carry_ref[...] = c0_ref[0]

    u_p = jnp.dot(perm_ref[...], u_ref[...], preferred_element_type=F32).astype(BF16)
    bu_ref[...] = jnp.dot(u_p, bw_ref[0], preferred_element_type=F32)

    ar = jnp.broadcast_to(a_ref[0, 0:1, :], (SUBLANES, ns))
    ai = jnp.broadcast_to(a_ref[0, 1:2, :], (SUBLANES, ns))

    def local_scan(i, x):
        xr, xi = x
        r0 = pl.multiple_of(i * SUBLANES, SUBLANES)
        nr = ar * xr - ai * xi + bu_ref[pl.ds(r0, SUBLANES), 0:ns]
        ni = ar * xi + ai * xr + bu_ref[pl.ds(r0, SUBLANES), ns:2 * ns]
        bu_ref[pl.ds(r0, SUBLANES), 0:ns] = nr
        bu_ref[pl.ds(r0, SUBLANES), ns:2 * ns] = ni
        return nr, ni

    zero = jnp.zeros((SUBLANES, ns), F32)
    er, ei = lax.fori_loop(0, tseg, local_scan, (zero, zero))

    c_r = carry_ref[0:1, :]
    c_i = carry_ref[1:2, :]
    sub = lax.broadcasted_iota(jnp.int32, (SUBLANES, ns), 0)
    g1r = g_ref[0, 0:1, :]
    g1i = g_ref[0, 1:2, :]
    fr = er + jnp.where(sub == 0, g1r * c_r - g1i * c_i, 0.0)
    fi = ei + jnp.where(sub == 0, g1r * c_i + g1i * c_r, 0.0)
    for k, d in enumerate((1, 2, 4)):
        gr = g_ref[0, 2 * k:2 * k + 1, :]
        gi = g_ref[0, 2 * k + 1:2 * k + 2, :]
        sr = jnp.where(sub >= d, pltpu.roll(fr, d, 0), 0.0)
        si = jnp.where(sub >= d, pltpu.roll(fi, d, 0), 0.0)
        fr, fi = fr + gr * sr - gi * si, fi + gr * si + gi * sr
    cin_r = jnp.where(sub == 0, c_r, pltpu.roll(fr, 1, 0))
    cin_i = jnp.where(sub == 0, c_i, pltpu.roll(fi, 1, 0))
    carry_ref[0:1, :] = fr[SUBLANES - 1:SUBLANES, :]
    carry_ref[1:2, :] = fi[SUBLANES - 1:SUBLANES, :]

    def add_carry(j, c):
        cr, ci = c
        c1r, c1i = ar * cr - ai * ci, ar * ci + ai * cr
        c2r, c2i = ar * c1r - ai * c1i, ar * c1i + ai * c1r
        r0 = pl.multiple_of(j * 2 * SUBLANES, 2 * SUBLANES)
        r1 = r0 + SUBLANES
        xr = jnp.concatenate([bu_ref[pl.ds(r0, SUBLANES), 0:ns] + c1r,
                              bu_ref[pl.ds(r1, SUBLANES), 0:ns] + c2r], axis=0)
        xi = jnp.concatenate([bu_ref[pl.ds(r0, SUBLANES), ns:2 * ns] + c1i,
                              bu_ref[pl.ds(r1, SUBLANES), ns:2 * ns] + c2i], axis=0)
        xb_ref[pl.ds(r0, 2 * SUBLANES), 0:ns] = xr.astype(BF16)
        xb_ref[pl.ds(r0, 2 * SUBLANES), ns:2 * ns] = xi.astype(BF16)
        return c2r, c2i

    lax.fori_loop(0, tseg // 2, add_carry, (cin_r, cin_i))

    y = jnp.dot(xb_ref[...], cw_ref[0], preferred_element_type=F32) + d_ref[...] * u_p.astype(F32)
    zp = jax.nn.gelu(y).astype(BF16)
    z_ref[...] = jnp.dot(permt_ref[...], zp, preferred_element_type=F32).astype(BF16)

    @pl.when(t == pl.num_programs(2) - 1)
    def _():
        cout_ref[0, 0] = carry_ref[...]


def _s5_scan(u_arr, u_col0, nb, seq, tc, tabs, c0, name):
    bw, cw, a_tab, g_tab, d_skip = tabs
    ntile, _, ns2 = bw.shape
    ns = ns2 // 2
    nt = seq // tc
    tseg = tc // SUBLANES
    r = jnp.arange(tc, dtype=jnp.int32)
    src = (r % SUBLANES) * tseg + r // SUBLANES
    perm = (src[:, None] == jnp.arange(tc, dtype=jnp.int32)[None, :]).astype(BF16)
    kern = functools.partial(_s5_kernel, tc=tc, ns=ns)
    return pl.pallas_call(
        kern,
        grid=(nb, ntile, nt),
        in_specs=[
            pl.BlockSpec((tc, LANES), lambda b, n, t: (b * nt + t, u_col0 + n)),
            pl.BlockSpec((tc, tc), lambda b, n, t: (0, 0)),
            pl.BlockSpec((tc, tc), lambda b, n, t: (0, 0)),
            pl.BlockSpec((1, LANES, 2 * ns), lambda b, n, t: (n, 0, 0)),
            pl.BlockSpec((1, 2 * ns, LANES), lambda b, n, t: (n, 0, 0)),
            pl.BlockSpec((1, 2, ns), lambda b, n, t: (n, 0, 0)),
            pl.BlockSpec((1, 6, ns), lambda b, n, t: (n, 0, 0)),
            pl.BlockSpec((1, LANES), lambda b, n, t: (0, n)),
            pl.BlockSpec((1, 2, ns), lambda b, n, t: (n, 0, 0)),
        ],
        out_specs=[
            pl.BlockSpec((tc, LANES), lambda b, n, t: (b * nt + t, n)),
            pl.BlockSpec((1, 1, 2, ns), lambda b, n, t: (b, n, 0, 0)),
        ],
        out_shape=[
            jax.ShapeDtypeStruct((nb * seq, ntile * LANES), BF16),
            jax.ShapeDtypeStruct((nb, ntile, 2, ns), F32),
        ],
        scratch_shapes=[
            pltpu.VMEM((tc, 2 * ns), F32),
            pltpu.VMEM((tc, 2 * ns), BF16),
            pltpu.VMEM((2, ns), F32),
        ],
        compiler_params=_cparams(("parallel", "parallel", "arbitrary")),
        name=name,
    )(u_arr, perm, perm.T, bw, cw, a_tab, g_tab, d_skip, c0)


def _s5_tables(a_re, a_im, log_dt, b_re, b_im, c_re, c_im, d_skip):
    ngrp, nst = a_re.shape
    ntile = ngrp // S5_TILE_GROUPS
    ns = S5_TILE_GROUPS * nst
    dt = jnp.exp(log_dt.astype(F32))[:, None]
    lr, li = a_re.astype(F32), a_im.astype(F32)

    def a_pow(k):
        mag = jnp.exp(lr * dt * k)
        return mag * jnp.cos(li * dt * k), mag * jnp.sin(li * dt * k)

    ab_re, ab_im = a_pow(1)
    den = lr * lr + li * li
    f_re = ((ab_re - 1.0) * lr + ab_im * li) / den
    f_im = (ab_im * lr - (ab_re - 1.0) * li) / den
    br, bi = b_re.astype(F32), b_im.astype(F32)
    bb_re = f_re[..., None] * br - f_im[..., None] * bi
    bb_im = f_re[..., None] * bi + f_im[..., None] * br
    eye = jnp.eye(S5_TILE_GROUPS, dtype=F32)

    def b_tile(bb):
        bb = bb.reshape(ntile, S5_TILE_GROUPS, nst, S5_GROUP)
        return jnp.einsum("ab,napc->nacbp", eye, bb).reshape(ntile, LANES, ns)

    def c_tile(cc):
        cc = cc.astype(F32).reshape(ntile, S5_TILE_GROUPS, S5_GROUP, nst)
        return jnp.einsum("ab,nacp->nbpac", eye, cc).reshape(ntile, ns, LANES)

    bw = jnp.concatenate([b_tile(bb_re), b_tile(bb_im)], axis=2).astype(BF16)
    cw = jnp.concatenate([c_tile(c_re), -c_tile(c_im)], axis=1).astype(BF16)
    a_tab = jnp.stack([ab_re.reshape(ntile, ns), ab_im.reshape(ntile, ns)], axis=1)

    def g_tab(tseg):
        rows = []
        for d in (1, 2, 4):
            pr, pi = a_pow(tseg * d)
            rows += [pr.reshape(ntile, ns), pi.reshape(ntile, ns)]
        return jnp.stack(rows, axis=1)

    return bw, cw, a_tab, g_tab, d_skip.astype(F32).reshape(1, -1)


def _glu_kernel(z_ref, wa_ref, wb_ref, gs_ref, bg_ref, o_ref):
    z = z_ref[...]
    a = jnp.dot(z, wa_ref[...], preferred_element_type=F32)
    b = jnp.dot(z, wb_ref[...], preferred_element_type=F32)
    gate = jax.nn.sigmoid(gs_ref[...].astype(F32) + bg_ref[...])
    o_ref[...] = (a * jax.nn.sigmoid(b) * gate).astype(o_ref.dtype)


def _glu(z, wa, wb, proj, gs_col0, bias, tm, tn):
    m, k = z.shape
    n = wa.shape[1]
    tm = _pick(m, tm, SUBLANES)
    tn = _pick(math.gcd(n, gs_col0) if gs_col0 else n, tn)
    goff = gs_col0 // tn
    return pl.pallas_call(
        _glu_kernel,
        grid=(m // tm, n // tn),
        in_specs=[
            pl.BlockSpec((tm, k), lambda i, j: (i, 0)),
            pl.BlockSpec((k, tn), lambda i, j: (0, j)),
            pl.BlockSpec((k, tn), lambda i, j: (0, j)),
            pl.BlockSpec((tm, tn), lambda i, j: (i, goff + j)),
            pl.BlockSpec((1, tn), lambda i, j: (0, j)),
        ],
        out_specs=pl.BlockSpec((tm, tn), lambda i, j: (i, j)),
        out_shape=jax.ShapeDtypeStruct((m, n), BF16),
        compiler_params=_cparams(("parallel", "arbitrary")),
        name="glu",
    )(z, wa, wb, proj, bias)


def _attn_kernel(q_ref, kn_ref, v_ref, kr_ref, km_ref, vm_ref, krm_ref, gm_ref, bg_ref, o_ref,
                 *, tq, tk, nsub, n_meta):
    qi = pl.program_id(2)
    ts = tq // nsub
    qs = [q_ref[i * ts:(i + 1) * ts, :] for i in range(nsub)]

    def scores(q, kn, kr):
        k = jnp.concatenate([kn, kr], axis=1)
        return lax.dot_general(q, k, (((1,), (1,)), ((), ())), preferred_element_type=F32)

    def update(carry, s, v):
        m, l, acc = carry
        m_new = jnp.maximum(m, jnp.max(s, axis=1, keepdims=True))
        alpha = jnp.exp2(m - m_new)
        p = jnp.exp2(s - m_new)
        l = alpha * l + jnp.sum(p, axis=1, keepdims=True)
        acc = alpha * acc + jnp.dot(p.astype(BF16), v, preferred_element_type=F32)
        return m_new, l, acc

    km = jnp.concatenate([km_ref[...], krm_ref[...]], axis=1)
    carries = []
    for i in range(nsub):
        s = lax.dot_general(qs[i], km, (((1,), (1,)), ((), ())), preferred_element_type=F32)
        col = lax.broadcasted_iota(jnp.int32, s.shape, 1)
        s = jnp.where(col < n_meta, s, NEG_INF)
        m = jnp.max(s, axis=1, keepdims=True)
        p = jnp.exp2(s - m)
        carries.append((m, jnp.sum(p, axis=1, keepdims=True),
                        jnp.dot(p.astype(BF16), vm_ref[...], preferred_element_type=F32)))

    def step(j, carries, masked):
        r0 = pl.multiple_of(j * tk, tk)
        kn = kn_ref[pl.ds(r0, tk), :]
        kr = kr_ref[pl.ds(r0, tk), :]
        v = v_ref[pl.ds(r0, tk), :]
        out = []
        for i in range(nsub):
            s = scores(qs[i], kn, kr)
            if masked:
                row = qi * tq + i * ts + lax.broadcasted_iota(jnp.int32, s.shape, 0)
                colp = j * tk + lax.broadcasted_iota(jnp.int32, s.shape, 1)
                s = jnp.where(colp <= row, s, NEG_INF)
            out.append(update(carries[i], s, v))
        return tuple(out)

    n_full = (qi * tq) // tk
    carries = lax.fori_loop(0, n_full, lambda j, c: step(j, c, False), tuple(carries))
    for d in range(tq // tk):
        carries = step(n_full + d, carries, True)
    for i in range(nsub):
        m, l, acc = carries[i]
        gate = jax.nn.sigmoid(gm_ref[i * ts:(i + 1) * ts, :].astype(F32) + bg_ref[...])
        o_ref[i * ts:(i + 1) * ts, :] = (acc / l * gate).astype(o_ref.dtype)


def _attention(q, kv, kr, kvm, krm, proj, gm_col0, bias, nb, seq, nh, n_meta, tq, tk, nsub):
    tq = _pick(seq, tq, SUBLANES)
    tk = _pick(tq, tk, SUBLANES)
    nq = seq // tq
    mrows = kvm.shape[0]
    goff = gm_col0 // LANES
    kern = functools.partial(_attn_kernel, tq=tq, tk=tk, nsub=nsub, n_meta=n_meta)
    return pl.pallas_call(
        kern,
        grid=(nb, nh, nq),
        in_specs=[
            pl.BlockSpec((tq, HEAD_PAD), lambda b, h, i: (b * nq + i, h)),
            pl.BlockSpec((seq, LANES), lambda b, h, i: (b, 2 * h)),
            pl.BlockSpec((seq, LANES), lambda b, h, i: (b, 2 * h + 1)),
            pl.BlockSpec((seq, LANES), lambda b, h, i: (b, 0)),
            pl.BlockSpec((mrows, LANES), lambda b, h, i: (0, 2 * h)),
            pl.BlockSpec((mrows, LANES), lambda b, h, i: (0, 2 * h + 1)),
            pl.BlockSpec((mrows, LANES), lambda b, h, i: (0, 0)),
            pl.BlockSpec((tq, LANES), lambda b, h, i: (b * nq + i, goff + h)),
            pl.BlockSpec((1, LANES), lambda b, h, i: (0, h)),
        ],
        out_specs=pl.BlockSpec((tq, LANES), lambda b, h, i: (b * nq + i, h)),
        out_shape=jax.ShapeDtypeStruct((nb * seq, nh * V_HEAD), BF16),
        compiler_params=_cparams(("parallel", "parallel", "arbitrary")),
        name="attention",
    )(q, kv, kv, kr, kvm, kvm, krm, proj, bias)


def _outproj_kernel(ys_ref, ym_ref, w_ref, res_ref, o_ref, a_ref):
    @pl.when(pl.program_id(1) == 0)
    def _():
        a_ref[...] = (ys_ref[...].astype(F32) + ym_ref[...].astype(F32)).astype(BF16)

    o_ref[...] = jnp.dot(a_ref[...], w_ref[...], preferred_element_type=F32) + res_ref[...]


def _outproj(ys, ym, w, res, tm, tn):
    m, k = ys.shape
    n = w.shape[1]
    tm = _pick(m, tm, SUBLANES)
    tn = _pick(n, tn)
    return pl.pallas_call(
        _outproj_kernel,
        grid=(m // tm, n // tn),
        in_specs=[
            pl.BlockSpec((tm, k), lambda i, j: (i, 0)),
            pl.BlockSpec((tm, k), lambda i, j: (i, 0)),
            pl.BlockSpec((k, tn), lambda i, j: (0, j)),
            pl.BlockSpec((tm, tn), lambda i, j: (i, j)),
        ],
        out_specs=pl.BlockSpec((tm, tn), lambda i, j: (i, j)),
        out_shape=jax.ShapeDtypeStruct((m, n), F32),
        scratch_shapes=[pltpu.VMEM((tm, k), BF16)],
        compiler_params=_cparams(("parallel", "arbitrary")),
        name="outproj",
    )(ys, ym, w, res)


def _router_kernel(h_ref, g_ref, wr_ref, br_ref, hp_ref, idx_ref, wt_ref, rank_ref, cnt_ref,
                   run_ref, *, xr):
    @pl.when(pl.program_id(0) == 0)
    def _():
        run_ref[...] = jnp.zeros_like(run_ref)

    xf = h_ref[...]
    tm, d = xf.shape
    ms = jnp.mean(xf * xf, axis=-1, keepdims=True)
    hn = xf * lax.rsqrt(ms + EPS) * g_ref[...]

    half = d // 2
    bits = pltpu.bitcast(hn.astype(BF16).astype(F32), jnp.uint32)
    packed = (bits[:, :half] >> 16) | (bits[:, half:] & jnp.uint32(0xFFFF0000))
    for s in range(xr):
        hp_ref[pl.ds(s, tm, stride=xr), :] = packed[:, s * LANES:(s + 1) * LANES]

    logits = jnp.dot(hn, wr_ref[...], preferred_element_type=F32,
                     precision=lax.Precision.HIGHEST) + br_ref[...]
    ne = logits.shape[1]
    lane = lax.broadcasted_iota(jnp.int32, logits.shape, 1)
    vals, ids = [], []
    for _ in range(TOP_K):
        mx = jnp.max(logits, axis=1, keepdims=True)
        ix = jnp.min(jnp.where(logits == mx, lane, ne), axis=1, keepdims=True)
        vals.append(mx)
        ids.append(ix)
        logits = jnp.where(lane == ix, -jnp.inf, logits)
    v = jnp.concatenate(vals, axis=1)
    e = jnp.exp(v - v[:, 0:1])
    idx_ref[...] = jnp.concatenate(ids, axis=1)
    wt_ref[...] = e / jnp.sum(e, axis=1, keepdims=True)

    tri = (lax.broadcasted_iota(jnp.int32, (tm, tm), 0)
           > lax.broadcasted_iota(jnp.int32, (tm, tm), 1)).astype(BF16)
    base = run_ref[...]
    ranks = []
    for k in range(TOP_K):
        oh = (lane == ids[k]).astype(F32)
        prefix = jnp.dot(tri, oh.astype(BF16), preferred_element_type=F32)
        ranks.append(jnp.sum(oh * (base + prefix), axis=1, keepdims=True))
        base = base + jnp.sum(oh, axis=0, keepdims=True)
    run_ref[...] = base
    rank_ref[...] = jnp.concatenate(ranks, axis=1).astype(jnp.int32)
    cnt_ref[...] = base.astype(jnp.int32)


def _router(h, gain, w_router, b_router, tm):
    m, d = h.shape
    ne = w_router.shape[1]
    tm = _pick(m, tm, SUBLANES)
    xr = _x_slab_rows(d)
    return pl.pallas_call(
        functools.partial(_router_kernel, xr=xr),
        grid=(m // tm,),
        in_specs=[
            pl.BlockSpec((tm, d), lambda i: (i, 0)),
            pl.BlockSpec((1, d), lambda i: (0, 0)),
            pl.BlockSpec((d, ne), lambda i: (0, 0)),
            pl.BlockSpec((1, ne), lambda i: (0, 0)),
        ],
        out_specs=[
            pl.BlockSpec((tm * xr, LANES), lambda i: (i, 0)),
            pl.BlockSpec((tm, TOP_K), lambda i: (i, 0)),
            pl.BlockSpec((tm, TOP_K), lambda i: (i, 0)),
            pl.BlockSpec((tm, TOP_K), lambda i: (i, 0)),
            pl.BlockSpec((1, ne), lambda i: (0, 0)),
        ],
        out_shape=[
            jax.ShapeDtypeStruct((m * xr, LANES), jnp.uint32),
            jax.ShapeDtypeStruct((m, TOP_K), jnp.int32),
            jax.ShapeDtypeStruct((m, TOP_K), F32),
            jax.ShapeDtypeStruct((m, TOP_K), jnp.int32),
            jax.ShapeDtypeStruct((1, ne), jnp.int32),
        ],
        scratch_shapes=[pltpu.VMEM((1, ne), F32)],
        compiler_params=_cparams(("arbitrary",)),
        name="router",
    )(h, gain.reshape(1, d).astype(F32), w_router.astype(F32), b_router.reshape(1, ne).astype(F32))


def _dispatch_kernel(dest_ref, zlo_ref, zhi_ref, tot_ref, hp_ref, xs_hbm, stage, zbuf, sem, zsem,
                     *, per_step, npad, zrows, sr):
    i = pl.program_id(0)
    n = pl.num_programs(0)
    ne = zlo_ref.shape[0]
    tt = per_step // TOP_K

    def zero_row(r):
        return pltpu.make_async_copy(zbuf.at[pl.ds(0, sr)], xs_hbm.at[pl.ds(r * sr, sr)],
                                     zsem.at[0])

    def zero_chunk(c):
        return pltpu.make_async_copy(zbuf, xs_hbm.at[pl.ds(c * (zrows * sr), zrows * sr)],
                                     zsem.at[0])

    def for_zero_fills(fn_row, fn_chunk):
        def per_expert(e, _):
            lax.fori_loop(zlo_ref[e], zhi_ref[e], lambda r, _: (fn_row(r), 0)[1], 0)
            return 0
        lax.fori_loop(0, ne, per_expert, 0)
        lax.fori_loop(tot_ref[0] // zrows, npad // zrows, lambda c, _: (fn_chunk(c), 0)[1], 0)

    @pl.when(i == 0)
    def _():
        zbuf[...] = jnp.zeros_like(zbuf)
        for_zero_fills(lambda r: zero_row(r).start(), lambda c: zero_chunk(c).start())

    slot = i % 2
    stage[slot] = hp_ref[...]

    def body(lt, _):
        src = stage.at[slot, pl.ds(lt * sr, sr)]
        for k in range(TOP_K):
            pltpu.make_async_copy(src, xs_hbm.at[pl.ds(dest_ref[i, lt * TOP_K + k] * sr, sr)],
                                  sem.at[slot]).start(priority=k % 2)
        return 0
    lax.fori_loop(0, tt, body, 0)

    def wait_step(s):
        for _ in range(TOP_K):
            pltpu.make_async_copy(stage.at[s], xs_hbm.at[pl.ds(0, tt * sr)], sem.at[s]).wait()

    @pl.when(i > 0)
    def _():
        wait_step((i + 1) % 2)

    @pl.when(i == n - 1)
    def _():
        wait_step(i % 2)
        for_zero_fills(lambda r: zero_row(r).wait(), lambda c: zero_chunk(c).wait())


def _dispatch(hp, dest2d, zero_lo, zero_hi, total, npad, sr):
    nsteps, per_step = dest2d.shape
    w = hp.shape[1]
    zrows = 64
    assert npad % zrows == 0
    return pl.pallas_call(
        functools.partial(_dispatch_kernel, per_step=per_step, npad=npad, zrows=zrows, sr=sr),
        grid_spec=pltpu.PrefetchScalarGridSpec(
            num_scalar_prefetch=4,
            grid=(nsteps,),
            in_specs=[pl.BlockSpec((per_step // TOP_K * sr, w), lambda i, *_: (i, 0))],
            out_specs=pl.BlockSpec(memory_space=pl.ANY),
            scratch_shapes=[pltpu.VMEM((2, per_step // TOP_K * sr, w), hp.dtype),
                            pltpu.VMEM((zrows * sr, w), hp.dtype),
                            pltpu.SemaphoreType.DMA((2,)), pltpu.SemaphoreType.DMA((1,))],
        ),
        out_shape=jax.ShapeDtypeStruct((npad * sr, w), hp.dtype),
        compiler_params=_cparams(("arbitrary",)),
        name="dispatch",
    )(dest2d, zero_lo, zero_hi, total, hp)


def _moe_kernel(be_ref, na_ref, x_ref, wg_ref, wu_ref, bg_ref, bu_ref, wd_ref, bd_ref,
                o_ref, xb_ref, act_ref, *, bm, nc1, nc2, xr, yr):
    i = pl.program_id(0)
    c = pl.program_id(1)
    active = i < na_ref[0]
    tn = act_ref.shape[2]

    @pl.when(jnp.logical_and(active, c == 0))
    def _():
        w = LANES
        half = xr * w
        for s in range(xr):
            word = x_ref[pl.ds(s, bm, stride=xr), :]
            xb_ref[:, s * w:(s + 1) * w] = pltpu.bitcast(word << 16, F32).astype(BF16)
            xb_ref[:, half + s * w:half + (s + 1) * w] = pltpu.bitcast(
                word & jnp.uint32(0xFFFF0000), F32).astype(BF16)

    @pl.when(jnp.logical_and(active, c < nc1))
    def _():
        xb = xb_ref[...]
        g = jnp.dot(xb, wg_ref[0], preferred_element_type=F32) + bg_ref[0]
        u = jnp.dot(xb, wu_ref[0], preferred_element_type=F32) + bu_ref[0]
        gate = jnp.minimum(g, SWIGLU_LIMIT)
        up = jnp.clip(u, -SWIGLU_LIMIT, SWIGLU_LIMIT)
        act_ref[c] = (gate * jax.nn.sigmoid(SWIGLU_ALPHA * gate) * (up + 1.0)).astype(BF16)

    @pl.when(jnp.logical_and(active, c >= nc1))
    def _():
        y = bd_ref[0] + jnp.dot(act_ref[0], wd_ref[0, 0:tn, :], preferred_element_type=F32)
        for cc in range(1, nc1):
            y += jnp.dot(act_ref[cc], wd_ref[0, cc * tn:(cc + 1) * tn, :],
                         preferred_element_type=F32)
        half = y.shape[1] // 2
        lo = pltpu.bitcast(y[:, :half].astype(BF16).astype(F32), jnp.uint32) >> 16
        hi = pltpu.bitcast(y[:, half:].astype(BF16).astype(F32), jnp.uint32) & jnp.uint32(0xFFFF0000)
        word = lo | hi
        per = half // LANES
        for j in range(per):
            o_ref[pl.ds((c - nc1) * per + j, bm, stride=yr), :] = word[:, j * LANES:(j + 1) * LANES]

    @pl.when(jnp.logical_and(jnp.logical_not(active), c == 0))
    def _():
        o_ref[...] = jnp.zeros_like(o_ref)


def _moe(xs, blk_expert, n_active, w_gu, b_gu, w_down, b_down, bm, tn, tcol):
    w = xs.shape[1]
    ne, de, d = w_down.shape
    xr, yr = _x_slab_rows(d), _y_slab_rows(d)
    npad = xs.shape[0] // xr
    nc1 = de // tn
    nc2 = d // tcol
    nblk = npad // bm

    def blk(i, na):
        return jnp.minimum(i, na[0] - 1)

    def c1(c):
        return jnp.minimum(c, nc1 - 1)

    def c2(c):
        return jnp.maximum(c - nc1, 0)

    in_specs = [
        pl.BlockSpec((bm * xr, w), lambda i, c, be, na: (blk(i, na), 0)),
        pl.BlockSpec((1, d, tn), lambda i, c, be, na: (be[blk(i, na)], 0, c1(c))),
        pl.BlockSpec((1, d, tn), lambda i, c, be, na: (be[blk(i, na)], 0, nc1 + c1(c))),
        pl.BlockSpec((1, 1, tn), lambda i, c, be, na: (be[blk(i, na)], 0, c1(c))),
        pl.BlockSpec((1, 1, tn), lambda i, c, be, na: (be[blk(i, na)], 0, nc1 + c1(c))),
        pl.BlockSpec((1, de, tcol), lambda i, c, be, na: (be[blk(i, na)], 0, c2(c))),
        pl.BlockSpec((1, 1, tcol), lambda i, c, be, na: (be[blk(i, na)], 0, c2(c))),
    ]
    return pl.pallas_call(
        functools.partial(_moe_kernel, bm=bm, nc1=nc1, nc2=nc2, xr=xr, yr=yr),
        grid_spec=pltpu.PrefetchScalarGridSpec(
            num_scalar_prefetch=2,
            grid=(nblk, nc1 + nc2),
            in_specs=in_specs,
            out_specs=pl.BlockSpec((bm * yr, w), lambda i, c, be, na: (i, 0)),
            scratch_shapes=[pltpu.VMEM((bm, d), BF16), pltpu.VMEM((nc1, bm, tn), BF16)],
        ),
        out_shape=jax.ShapeDtypeStruct((npad * yr, w), jnp.uint32),
        compiler_params=_cparams(("arbitrary", "arbitrary")),
        name="moe",
    )(blk_expert, n_active, xs, w_gu, w_gu, b_gu, b_gu, w_down, b_down)


def _combine_kernel(idx_ref, y_hbm, h_ref, wt_ref, g_ref, o_ref, buf, sem, *, rows, tt, yr, per):
    i = pl.program_id(0)
    n = pl.num_programs(0)
    w = buf.shape[2]

    def issue(step, slot):
        def body(r4, _):
            for j in range(4):
                r = r4 * 4 + j
                src = idx_ref[step, r]
                pltpu.make_async_copy(y_hbm.at[pl.ds(src * yr, yr)],
                                      buf.at[slot, pl.ds(r * yr, yr)],
                                      sem.at[slot]).start(priority=j % 2)
            return 0
        lax.fori_loop(0, rows // 4, body, 0)

    @pl.when(i == 0)
    def _():
        issue(0, 0)

    @pl.when(i + 1 < n)
    def _():
        issue(i + 1, (i + 1) % 2)

    slot = i % 2
    pltpu.make_async_copy(y_hbm.at[pl.ds(0, rows * yr)], buf.at[slot], sem.at[slot]).wait()
    cur = buf.at[slot]
    wts = wt_ref[...]
    los, his = [], []
    for s in range(yr):
        lo = hi = None
        for k in range(TOP_K):
            word = cur[pl.ds(k * tt * yr + s, tt, stride=yr), :]
            wk = wts[:, k:k + 1]
            tl = wk * pltpu.bitcast(word << 16, F32)
            th = wk * pltpu.bitcast(word & jnp.uint32(0xFFFF0000), F32)
            lo = tl if lo is None else lo + tl
            hi = th if hi is None else hi + th
        los.append(lo)
        his.append(hi)
    pieces = []
    for n0 in range(0, yr, per):
        pieces += los[n0:n0 + per] + his[n0:n0 + per]
    acc = h_ref[...] + jnp.concatenate(pieces, axis=1)
    ms = jnp.mean(acc * acc, axis=-1, keepdims=True)
    o_ref[...] = acc * lax.rsqrt(ms + EPS) * g_ref[...]


def _combine(ys, idx2d, h, top_w, gain, tt, tcol):
    nsteps, rows = idx2d.shape
    d = h.shape[1]
    w = ys.shape[1]
    yr = _y_slab_rows(d)
    return pl.pallas_call(
        functools.partial(_combine_kernel, rows=rows, tt=tt, yr=yr, per=tcol // (2 * LANES)),
        grid_spec=pltpu.PrefetchScalarGridSpec(
            num_scalar_prefetch=1,
            grid=(nsteps,),
            in_specs=[
                pl.BlockSpec(memory_space=pl.ANY),
                pl.BlockSpec((tt, d), lambda i, idx: (i, 0)),
                pl.BlockSpec((tt, TOP_K), lambda i, idx: (i, 0)),
                pl.BlockSpec((1, d), lambda i, idx: (0, 0)),
            ],
            out_specs=pl.BlockSpec((tt, d), lambda i, idx: (i, 0)),
            scratch_shapes=[pltpu.VMEM((2, rows * yr, w), jnp.uint32),
                            pltpu.SemaphoreType.DMA((2,))],
        ),
        out_shape=jax.ShapeDtypeStruct(h.shape, F32),
        compiler_params=_cparams(("arbitrary",)),
        name="combine",
    )(idx2d, ys, h, top_w, gain.reshape(1, d).astype(F32))


def _dispatch_plan(top_idx, rank, counts, bm, nblk):
    ne = counts.shape[0]
    padded = ((counts + bm - 1) // bm) * bm
    pad_end = jnp.cumsum(padded)
    pad_start = pad_end - padded
    onehot = top_idx[..., None] == jnp.arange(ne, dtype=jnp.int32)
    dest = rank + jnp.sum(jnp.where(onehot, pad_start, 0), axis=-1)
    blk_row0 = jnp.arange(nblk, dtype=jnp.int32) * bm
    blk_expert = jnp.minimum(
        jnp.sum((pad_end[None, :] <= blk_row0[:, None]).astype(jnp.int32), axis=1), ne - 1)
    total = pad_end[-1:].astype(jnp.int32)
    return (dest.astype(jnp.int32), blk_expert, total // bm,
            (pad_start + counts).astype(jnp.int32), pad_end.astype(jnp.int32), total)


def _rope_tables(n_pos):
    pos = jnp.arange(n_pos, dtype=F32)
    inv_freq = ROPE_THETA ** (-jnp.arange(0, QK_ROPE, 2, dtype=F32) / QK_ROPE)
    ang = pos[:, None] * inv_freq[None, :]
    cos, sin = jnp.cos(ang), jnp.sin(ang)
    zero = jnp.zeros((n_pos, LANES - QK_ROPE), F32)
    return jnp.concatenate([cos, cos, zero], axis=1), jnp.concatenate([-sin, sin, zero], axis=1)


def _with_swapped_rope(w_rope):
    half = QK_ROPE // 2
    return jnp.concatenate([w_rope, w_rope[..., half:], w_rope[..., :half]], axis=-1)


def kernel(x, meta_tokens, g_mix, w_in, b_gate, g_q_lat, g_kv_lat, w_uq, w_ukv, s5_a_re, s5_a_im, s5_log_dt, s5_b_re, s5_b_im, s5_c_re, s5_c_im, s5_d, w_glu_a, w_glu_b, w_out, g_ffn, w_router, b_router, w_gu, b_gu, w_down, b_down, g_final):
    assert w_in.shape[0] == 1, "single-layer block"
    nb, seq, d = x.shape
    t = nb * seq
    n_meta = meta_tokens.shape[0]
    ql, kvl = g_q_lat.shape[-1], g_kv_lat.shape[-1]
    sw = s5_d.shape[-1]
    nh = w_uq.shape[-1] // QK_DIM
    ne = w_router.shape[-1]
    assert ql % kvl == 0 and (ql + kvl) % LANES == 0 and sw % LANES == 0 and d % LANES == 0
    assert n_meta % (2 * SUBLANES) == 0 and n_meta <= LANES

    w0 = w_in[0]
    c_kv = ql + kvl
    c_kr = c_kv + QK_ROPE
    w_main = jnp.concatenate([w0[:, :c_kv], w0[:, c_kr:]], axis=1).astype(BF16)
    w_kr = _with_swapped_rope(w0[:, c_kv:c_kr]).astype(BF16)
    o_s5 = c_kv
    o_gs = o_s5 + sw
    o_gm = o_gs + d
    wq3 = w_uq[0].reshape(ql, nh, QK_DIM)
    w_q = jnp.concatenate([wq3[..., :QK_NOPE], _with_swapped_rope(wq3[..., QK_NOPE:])],
                          axis=-1).reshape(ql, nh * HEAD_PAD).astype(BF16)
    w_kv = w_ukv[0].astype(BF16)
    cos_t, sin_t = _rope_tables(n_meta + seq)
    rope_meta = (cos_t[:n_meta], sin_t[:n_meta])
    rope_real = (cos_t[n_meta:], sin_t[n_meta:])
    s5_tabs = _s5_tables(s5_a_re[0], s5_a_im[0], s5_log_dt[0], s5_b_re[0], s5_b_im[0],
                         s5_c_re[0], s5_c_im[0], s5_d[0])
    bw, cw, a_tab, g_tab_fn, d_skip = s5_tabs

    x2 = x.reshape(t, d)
    tm = 512

    proj_m = _norm_mm(meta_tokens, 0, g_mix[0], w_main, BF16, n_meta, 512, name="proj_meta")
    kr_m = _norm_mm(meta_tokens, 0, g_mix[0], w_kr, BF16, n_meta, LANES,
                    rope=(*rope_meta, LANES), name="krope_meta")
    kv_m = _norm_mm(proj_m, ql // kvl, g_kv_lat[0], w_kv, BF16, n_meta, 1024, name="kv_meta")
    kvm = jnp.pad(kv_m, ((0, LANES - n_meta), (0, 0)))
    krm = jnp.pad(kr_m, ((0, LANES - n_meta), (0, 0)))
    ntile = sw // LANES
    ns = bw.shape[2] // 2
    _, c_meta = _s5_scan(proj_m, o_s5 // LANES, 1, n_meta, n_meta,
                         (bw, cw, a_tab, g_tab_fn(n_meta // SUBLANES), d_skip),
                         jnp.zeros((ntile, 2, ns), F32), "s5_meta")

    proj = _norm_mm(x2, 0, g_mix[0], w_main, BF16, tm, 512, name="proj")
    kr = _norm_mm(x2, 0, g_mix[0], w_kr, BF16, tm, LANES, rope=(*rope_real, LANES), name="krope")
    q = _norm_mm(proj, 0, g_q_lat[0], w_q, BF16, tm, 1024, rope=(*rope_real, HEAD_PAD),
                 out_scale=math.log2(math.e) * QK_DIM ** -0.5, name="q")
    kv = _norm_mm(proj, ql // kvl, g_kv_lat[0], w_kv, BF16, tm, 1024, name="kv")

    tc = _pick(seq, 512, 2 * SUBLANES)
    z, _ = _s5_scan(proj, o_s5 // LANES, nb, seq, tc,
                    (bw, cw, a_tab, g_tab_fn(tc // SUBLANES), d_skip), c_meta[0], "s5")
    bias = b_gate[0].astype(F32).reshape(1, 2 * d)
    ys = _glu(z, w_glu_a[0].astype(BF16), w_glu_b[0].astype(BF16), proj, o_gs, bias[:, :d], 1024, 512)
    ym = _attention(q, kv, kr, kvm, krm, proj, o_gm, bias[:, d:], nb, seq, nh, n_meta, 1024, 512, 1)
    h = _outproj(ys, ym, w_out[0].astype(BF16), x2, tm, 512)

    assert _x_slab_rows(d) % SUBLANES == 0
    hp, top_idx, top_w, rank, counts = _router(h, g_ffn[0], w_router[0], b_router[0], 256)
    bm = 512 if t * TOP_K >= 512 * ne else 128
    nblk = (t * TOP_K + ne * (bm - 1) + bm - 1) // bm
    dest, blk_expert, n_active, zero_lo, zero_hi, total = _dispatch_plan(
        top_idx, rank, counts[0], bm, nblk)
    per_step = 512
    xs = _dispatch(hp, dest.reshape(t * TOP_K // per_step, per_step), zero_lo, zero_hi, total,
                   nblk * bm, _x_slab_rows(d))
    de = w_down.shape[2]
    tcol = _pick(d, 1024)
    ysort = _moe(xs, blk_expert, n_active, w_gu[0].astype(BF16), b_gu[0].reshape(ne, 1, 2 * de),
                 w_down[0].astype(BF16), b_down[0].reshape(ne, 1, d), bm, _pick(de, 512), tcol)
    tt = 64
    idx_c = dest.reshape(t // tt, tt, TOP_K).transpose(0, 2, 1).reshape(t // tt, TOP_K * tt)
    out = _combine(ysort, idx_c, h, top_w, g_final, tt, tcol)
    return out.reshape(nb, seq, d)
```

```python
import functools
import math

import jax
import jax.numpy as jnp
from jax import lax
from jax.experimental import pallas as pl
from jax.experimental.pallas import tpu as pltpu

F32 = jnp.float32
BF16 = jnp.bfloat16

EPS = 1e-6
QK_NOPE = 128
QK_ROPE = 64
QK_DIM = QK_NOPE + QK_ROPE
V_HEAD = 128
ROPE_THETA = 10000.0
NEG_INF = -1e30
S5_GROUP = 16
TOP_K = 4
SWIGLU_ALPHA = 1.702
SWIGLU_LIMIT = 7.0

LANES = 128
SUBLANES = 8
HEAD_PAD = 2 * LANES
S5_TILE_GROUPS = LANES // S5_GROUP
VMEM_LIMIT_V7X = 56 * 1024 * 1024


def _cparams(sem, vmem=VMEM_LIMIT_V7X):
    return pltpu.CompilerParams(dimension_semantics=sem, vmem_limit_bytes=vmem)


def _x_slab_rows(d):
    return d // (2 * LANES)


def _y_slab_rows(d):
    return d // (2 * LANES)


def _pick(n, pref, mult=LANES):
    if n <= pref:
        return n
    best = None
    for t in range(mult, pref + 1, mult):
        if n % t == 0:
            best = t
    assert best is not None, (n, pref, mult)
    return best


def _norm_mm_kernel(a_ref, g_ref, w_ref, *rest, rope_group, out_scale):
    if rope_group:
        cos_ref, sin_ref, o_ref, an_ref = rest
    else:
        o_ref, an_ref = rest

    @pl.when(pl.program_id(1) == 0)
    def _():
        xf = a_ref[...].astype(F32)
        ms = jnp.mean(xf * xf, axis=-1, keepdims=True)
        an_ref[...] = (xf * lax.rsqrt(ms + EPS) * g_ref[...]).astype(BF16)

    acc = jnp.dot(an_ref[...], w_ref[...], preferred_element_type=F32)
    if out_scale is not None:
        acc = acc * out_scale
    if not rope_group:
        o_ref[...] = acc.astype(o_ref.dtype)
        return
    c = cos_ref[...]
    s = sin_ref[...]
    for g0 in range(0, acc.shape[1], rope_group):
        lo = g0 + rope_group - LANES
        if lo > g0:
            o_ref[:, g0:lo] = acc[:, g0:lo].astype(o_ref.dtype)
        slab = acc[:, lo:lo + LANES]
        o_ref[:, lo:lo + LANES] = (slab * c + pltpu.roll(slab, QK_ROPE, 1) * s).astype(o_ref.dtype)


def _norm_mm(a, col_blk, gain, w, out_dtype, tm, tn, rope=None, out_scale=None, name=None):
    m = a.shape[0]
    k, n = w.shape
    tm = _pick(m, tm, SUBLANES)
    tn = _pick(n, tn)
    in_specs = [
        pl.BlockSpec((tm, k), lambda i, j: (i, col_blk)),
        pl.BlockSpec((1, k), lambda i, j: (0, 0)),
        pl.BlockSpec((k, tn), lambda i, j: (0, j)),
    ]
    args = [a, gain.reshape(1, k).astype(F32), w]
    group = 0
    if rope is not None:
        cos_t, sin_t, group = rope
        nrb = cos_t.shape[0] // tm
        in_specs += [pl.BlockSpec((tm, LANES), lambda i, j: (i % nrb, 0))] * 2
        args += [cos_t, sin_t]
    return pl.pallas_call(
        functools.partial(_norm_mm_kernel, rope_group=group, out_scale=out_scale),
        grid=(m // tm, n // tn),
        in_specs=in_specs,
        out_specs=pl.BlockSpec((tm, tn), lambda i, j: (i, j)),
        out_shape=jax.ShapeDtypeStruct((m, n), out_dtype),
        scratch_shapes=[pltpu.VMEM((tm, k), BF16)],
        compiler_params=_cparams(("parallel", "arbitrary")),
        name=name,
    )(*args)


def _s5_kernel(u_ref, perm_ref, permt_ref, bw_ref, cw_ref, a_ref, g_ref, d_ref, c0_ref,
               z_ref, cout_ref, bu_ref, xb_ref, carry_ref, *, tc, ns):
    t = pl.program_id(2)
    tseg = tc // SUBLANES

    @pl.when(t == 0)
    def _():
        carry_ref[...] = c0_ref[0]

    u_p = jnp.dot(perm_ref[...], u_ref[...], preferred_element_type=F32).astype(BF16)
    bu_ref[...] = jnp.dot(u_p, bw_ref[0], preferred_element_type=F32)

    ar = jnp.broadcast_to(a_ref[0, 0:1, :], (SUBLANES, ns))
    ai = jnp.broadcast_to(a_ref[0, 1:2, :], (SUBLANES, ns))

    def local_scan(i, x):
        xr, xi = x
        r0 = pl.multiple_of(i * SUBLANES, SUBLANES)
        nr = ar * xr - ai * xi + bu_ref[pl.ds(r0, SUBLANES), 0:ns]
        ni = ar * xi + ai * xr + bu_ref[pl.ds(r0, SUBLANES), ns:2 * ns]
        bu_ref[pl.ds(r0, SUBLANES), 0:ns] = nr
        bu_ref[pl.ds(r0, SUBLANES), ns:2 * ns] = ni
        return nr, ni

    zero = jnp.zeros((SUBLANES, ns), F32)
    er, ei = lax.fori_loop(0, tseg, local_scan, (zero, zero))

    c_r = carry_ref[0:1, :]
    c_i = carry_ref[1:2, :]
    sub = lax.broadcasted_iota(jnp.int32, (SUBLANES, ns), 0)
    g1r = g_ref[0, 0:1, :]
    g1i = g_ref[0, 1:2, :]
    fr = er + jnp.where(sub == 0, g1r * c_r - g1i * c_i, 0.0)
    fi = ei + jnp.where(sub == 0, g1r * c_i + g1i * c_r, 0.0)
    for k, d in enumerate((1, 2, 4)):
        gr = g_ref[0, 2 * k:2 * k + 1, :]
        gi = g_ref[0, 2 * k + 1:2 * k + 2, :]
        sr = jnp.where(sub >= d, pltpu.roll(fr, d, 0), 0.0)
        si = jnp.where(sub >= d, pltpu.roll(fi, d, 0), 0.0)
        fr, fi = fr + gr * sr - gi * si, fi + gr * si + gi * sr
    cin_r = jnp.where(sub == 0, c_r, pltpu.roll(fr, 1, 0))
    cin_i = jnp.where(sub == 0, c_i, pltpu.roll(fi, 1, 0))
    carry_ref[0:1, :] = fr[SUBLANES - 1:SUBLANES, :]
    carry_ref[1:2, :] = fi[SUBLANES - 1:SUBLANES, :]

    def add_carry(j, c):
        cr, ci = c
        c1r, c1i = ar * cr - ai * ci, ar * ci + ai * cr
        c2r, c2i = ar * c1r - ai * c1i, ar * c1i + ai * c1r
        r0 = pl.multiple_of(j * 2 * SUBLANES, 2 * SUBLANES)
        r1 = r0 + SUBLANES
        xr = jnp.concatenate([bu_ref[pl.ds(r0, SUBLANES), 0:ns] + c1r,
                              bu_ref[pl.ds(r1, SUBLANES), 0:ns] + c2r], axis=0)
        xi = jnp.concatenate([bu_ref[pl.ds(r0, SUBLANES), ns:2 * ns] + c1i,
                              bu_ref[pl.ds(r1, SUBLANES), ns:2 * ns] + c2i], axis=0)
        xb_ref[pl.ds(r0, 2 * SUBLANES), 0:ns] = xr.astype(BF16)
        xb_ref[pl.ds(r0, 2 * SUBLANES), ns:2 * ns] = xi.astype(BF16)
        return c2r, c2i

    lax.fori_loop(0, tseg // 2, add_carry, (cin_r, cin_i))

    y = jnp.dot(xb_ref[...], cw_ref[0], preferred_element_type=F32) + d_ref[...] * u_p.astype(F32)
    zp = jax.nn.gelu(y).astype(BF16)
    z_ref[...] = jnp.dot(permt_ref[...], zp, preferred_element_type=F32).astype(BF16)

    @pl.when(t == pl.num_programs(2) - 1)
    def _():
        cout_ref[0, 0] = carry_ref[...]


def _s5_scan(u_arr, u_col0, nb, seq, tc, tabs, c0, name):
    bw, cw, a_tab, g_tab, d_skip = tabs
    ntile, _, ns2 = bw.shape
    ns = ns2 // 2
    nt = seq // tc
    tseg = tc // SUBLANES
    r = jnp.arange(tc, dtype=jnp.int32)
    src = (r % SUBLANES) * tseg + r // SUBLANES
    perm = (src[:, None] == jnp.arange(tc, dtype=jnp.int32)[None, :]).astype(BF16)
    kern = functools.partial(_s5_kernel, tc=tc, ns=ns)
    return pl.pallas_call(
        kern,
        grid=(nb, ntile, nt),
        in_specs=[
            pl.BlockSpec((tc, LANES), lambda b, n, t: (b * nt + t, u_col0 + n)),
            pl.BlockSpec((tc, tc), lambda b, n, t: (0, 0)),
            pl.BlockSpec((tc, tc), lambda b, n, t: (0, 0)),
            pl.BlockSpec((1, LANES, 2 * ns), lambda b, n, t: (n, 0, 0)),
            pl.BlockSpec((1, 2 * ns, LANES), lambda b, n, t: (n, 0, 0)),
            pl.BlockSpec((1, 2, ns), lambda b, n, t: (n, 0, 0)),
            pl.BlockSpec((1, 6, ns), lambda b, n, t: (n, 0, 0)),
            pl.BlockSpec((1, LANES), lambda b, n, t: (0, n)),
            pl.BlockSpec((1, 2, ns), lambda b, n, t: (n, 0, 0)),
        ],
        out_specs=[
            pl.BlockSpec((tc, LANES), lambda b, n, t: (b * nt + t, n)),
            pl.BlockSpec((1, 1, 2, ns), lambda b, n, t: (b, n, 0, 0)),
        ],
        out_shape=[
            jax.ShapeDtypeStruct((nb * seq, ntile * LANES), BF16),
            jax.ShapeDtypeStruct((nb, ntile, 2, ns), F32),
        ],
        scratch_shapes=[
            pltpu.VMEM((tc, 2 * ns), F32),
            pltpu.VMEM((tc, 2 * ns), BF16),
            pltpu.VMEM((2, ns), F32),
        ],
        compiler_params=_cparams(("parallel", "parallel", "arbitrary")),
        name=name,
    )(u_arr, perm, perm.T, bw, cw, a_tab, g_tab, d_skip, c0)


def _s5_tables(a_re, a_im, log_dt, b_re, b_im, c_re, c_im, d_skip):
    ngrp, nst = a_re.shape
    ntile = ngrp // S5_TILE_GROUPS
    ns = S5_TILE_GROUPS * nst
    dt = jnp.exp(log_dt.astype(F32))[:, None]
    lr, li = a_re.astype(F32), a_im.astype(F32)

    def a_pow(k):
        mag = jnp.exp(lr * dt * k)
        return mag * jnp.cos(li * dt * k), mag * jnp.sin(li * dt * k)

    ab_re, ab_im = a_pow(1)
    den = lr * lr + li * li
    f_re = ((ab_re - 1.0) * lr + ab_im * li) / den
    f_im = (ab_im * lr - (ab_re - 1.0) * li) / den
    br, bi = b_re.astype(F32), b_im.astype(F32)
    bb_re = f_re[..., None] * br - f_im[..., None] * bi
    bb_im = f_re[..., None] * bi + f_im[..., None] * br
    eye = jnp.eye(S5_TILE_GROUPS, dtype=F32)

    def b_tile(bb):
        bb = bb.reshape(ntile, S5_TILE_GROUPS, nst, S5_GROUP)
        return jnp.einsum("ab,napc->nacbp", eye, bb).reshape(ntile, LANES, ns)

    def c_tile(cc):
        cc = cc.astype(F32).reshape(ntile, S5_TILE_GROUPS, S5_GROUP, nst)
        return jnp.einsum("ab,nacp->nbpac", eye, cc).reshape(ntile, ns, LANES)

    bw = jnp.concatenate([b_tile(bb_re), b_tile(bb_im)], axis=2).astype(BF16)
    cw = jnp.concatenate([c_tile(c_re), -c_tile(c_im)], axis=1).astype(BF16)
    a_tab = jnp.stack([ab_re.reshape(ntile, ns), ab_im.reshape(ntile, ns)], axis=1)

    def g_tab(tseg):
        rows = []
        for d in (1, 2, 4):
            pr, pi = a_pow(tseg * d)
            rows += [pr.reshape(ntile, ns), pi.reshape(ntile, ns)]
        return jnp.stack(rows, axis=1)

    return bw, cw, a_tab, g_tab, d_skip.astype(F32).reshape(1, -1)


def _glu_kernel(z_ref, wa_ref, wb_ref, gs_ref, bg_ref, o_ref):
    z = z_ref[...]
    a = jnp.dot(z, wa_ref[...], preferred_element_type=F32)
    b = jnp.dot(z, wb_ref[...], preferred_element_type=F32)
    gate = jax.nn.sigmoid(gs_ref[...].astype(F32) + bg_ref[...])
    o_ref[...] = (a * jax.nn.sigmoid(b) * gate).astype(o_ref.dtype)


def _glu(z, wa, wb, proj, gs_col0, bias, tm, tn):
    m, k = z.shape
    n = wa.shape[1]
    tm = _pick(m, tm, SUBLANES)
    tn = _pick(math.gcd(n, gs_col0) if gs_col0 else n, tn)
    goff = gs_col0 // tn
    return pl.pallas_call(
        _glu_kernel,
        grid=(m // tm, n // tn),
        in_specs=[
            pl.BlockSpec((tm, k), lambda i, j: (i, 0)),
            pl.BlockSpec((k, tn), lambda i, j: (0, j)),
            pl.BlockSpec((k, tn), lambda i, j: (0, j)),
            pl.BlockSpec((tm, tn), lambda i, j: (i, goff + j)),
            pl.BlockSpec((1, tn), lambda i, j: (0, j)),
        ],
        out_specs=pl.BlockSpec((tm, tn), lambda i, j: (i, j)),
        out_shape=jax.ShapeDtypeStruct((m, n), BF16),
        compiler_params=_cparams(("parallel", "arbitrary")),
        name="glu",
    )(z, wa, wb, proj, bias)


def _attn_kernel(q_ref, kn_ref, v_ref, kr_ref, km_ref, vm_ref, krm_ref, gm_ref, bg_ref, o_ref,
                 *, tq, tk, n_meta):
    qi = pl.program_id(2)

    def update(q, carry, j, mask_row0):
        m, l, acc = carry
        r0 = pl.multiple_of(j * tk, tk)
        k = jnp.concatenate([kn_ref[pl.ds(r0, tk), :], kr_ref[pl.ds(r0, tk), :]], axis=1)
        s = lax.dot_general(q, k, (((1,), (1,)), ((), ())), preferred_element_type=F32)
        if mask_row0 is not None:
            row = mask_row0 + lax.broadcasted_iota(jnp.int32, s.shape, 0)
            colp = j * tk + lax.broadcasted_iota(jnp.int32, s.shape, 1)
            s = jnp.where(colp <= row, s, NEG_INF)
        m_new = jnp.maximum(m, jnp.max(s, axis=1, keepdims=True))
        alpha = jnp.exp2(m - m_new)
        p = jnp.exp2(s - m_new)
        l = alpha * l + jnp.sum(p, axis=1, keepdims=True)
        acc = alpha * acc + jnp.dot(p.astype(BF16), v_ref[pl.ds(r0, tk), :],
                                    preferred_element_type=F32)
        return m_new, l, acc

    q = q_ref[...]
    km = jnp.concatenate([km_ref[...], krm_ref[...]], axis=1)
    s = lax.dot_general(q, km, (((1,), (1,)), ((), ())), preferred_element_type=F32)
    col = lax.broadcasted_iota(jnp.int32, s.shape, 1)
    s = jnp.where(col < n_meta, s, NEG_INF)
    m = jnp.max(s, axis=1, keepdims=True)
    p = jnp.exp2(s - m)
    l = jnp.sum(p, axis=1, keepdims=True)
    acc = jnp.dot(p.astype(BF16), vm_ref[...], preferred_element_type=F32)

    n_full = (qi * tq) // tk
    m, l, acc = lax.fori_loop(0, n_full, lambda j, c: update(q, c, j, None), (m, l, acc))

    for r in range(tq // tk):
        rows = slice(r * tk, (r + 1) * tk)
        qr = q_ref[rows, :]
        carry = (m[rows], l[rows], acc[rows])
        for dd in range(r):
            carry = update(qr, carry, n_full + dd, None)
        _, lr, ar = update(qr, carry, n_full + r, qi * tq + r * tk)
        gate = jax.nn.sigmoid(gm_ref[rows, :].astype(F32) + bg_ref[...])
        o_ref[rows, :] = (ar / lr * gate).astype(o_ref.dtype)


def _attention(q, kv, kr, kvm, krm, proj, gm_col0, bias, nb, seq, nh, n_meta, tq, tk):
    tq = _pick(seq, tq, SUBLANES)
    tk = _pick(tq, tk, SUBLANES)
    nq = seq // tq
    mrows = kvm.shape[0]
    goff = gm_col0 // LANES
    kern = functools.partial(_attn_kernel, tq=tq, tk=tk, n_meta=n_meta)
    return pl.pallas_call(
        kern,
        grid=(nb, nh, nq),
        in_specs=[
            pl.BlockSpec((tq, HEAD_PAD), lambda b, h, i: (b * nq + i, h)),
            pl.BlockSpec((seq, LANES), lambda b, h, i: (b, 2 * h)),
            pl.BlockSpec((seq, LANES), lambda b, h, i: (b, 2 * h + 1)),
            pl.BlockSpec((seq, LANES), lambda b, h, i: (b, 0)),
            pl.BlockSpec((mrows, LANES), lambda b, h, i: (0, 2 * h)),
            pl.BlockSpec((mrows, LANES), lambda b, h, i: (0, 2 * h + 1)),
            pl.BlockSpec((mrows, LANES), lambda b, h, i: (0, 0)),
            pl.BlockSpec((tq, LANES), lambda b, h, i: (b * nq + i, goff + h)),
            pl.BlockSpec((1, LANES), lambda b, h, i: (0, h)),
        ],
        out_specs=pl.BlockSpec((tq, LANES), lambda b, h, i: (b * nq + i, h)),
        out_shape=jax.ShapeDtypeStruct((nb * seq, nh * V_HEAD), BF16),
        compiler_params=_cparams(("parallel", "parallel", "arbitrary")),
        name="attention",
    )(q, kv, kv, kr, kvm, kvm, krm, proj, bias)


def _outproj_kernel(ys_ref, ym_ref, w_ref, res_ref, o_ref, a_ref):
    @pl.when(pl.program_id(1) == 0)
    def _():
        a_ref[...] = (ys_ref[...].astype(F32) + ym_ref[...].astype(F32)).astype(BF16)

    o_ref[...] = jnp.dot(a_ref[...], w_ref[...], preferred_element_type=F32) + res_ref[...]


def _outproj(ys, ym, w, res, tm, tn):
    m, k = ys.shape
    n = w.shape[1]
    tm = _pick(m, tm, SUBLANES)
    tn = _pick(n, tn)
    return pl.pallas_call(
        _outproj_kernel,
        grid=(m // tm, n // tn),
        in_specs=[
            pl.BlockSpec((tm, k), lambda i, j: (i, 0)),
            pl.BlockSpec((tm, k), lambda i, j: (i, 0)),
            pl.BlockSpec((k, tn), lambda i, j: (0, j)),
            pl.BlockSpec((tm, tn), lambda i, j: (i, j)),
        ],
        out_specs=pl.BlockSpec((tm, tn), lambda i, j: (i, j)),
        out_shape=jax.ShapeDtypeStruct((m, n), F32),
        scratch_shapes=[pltpu.VMEM((tm, k), BF16)],
        compiler_params=_cparams(("parallel", "arbitrary")),
        name="outproj",
    )(ys, ym, w, res)


def _router_kernel(h_ref, g_ref, wr_ref, wrl_ref, br_ref, hp_ref, idx_ref, wt_ref, rank_ref,
                   cnt_ref, run_ref, *, xr):
    @pl.when(pl.program_id(0) == 0)
    def _():
        run_ref[...] = jnp.zeros_like(run_ref)

    xf = h_ref[...]
    tm, d = xf.shape
    ms = jnp.mean(xf * xf, axis=-1, keepdims=True)
    hn = xf * lax.rsqrt(ms + EPS) * g_ref[...]

    half = d // 2
    bits = pltpu.bitcast(hn.astype(BF16).astype(F32), jnp.uint32)
    packed = (bits[:, :half] >> 16) | (bits[:, half:] & jnp.uint32(0xFFFF0000))
    for s in range(xr):
        hp_ref[pl.ds(s, tm, stride=xr), :] = packed[:, s * LANES:(s + 1) * LANES]

    hn_hi = hn.astype(BF16)
    hn_lo = (hn - hn_hi.astype(F32)).astype(BF16)
    logits = (jnp.dot(hn_hi, wr_ref[...], preferred_element_type=F32)
              + jnp.dot(hn_lo, wr_ref[...], preferred_element_type=F32)
              + jnp.dot(hn_hi, wrl_ref[...], preferred_element_type=F32)) + br_ref[...]
    ne = logits.shape[1]
    lane = lax.broadcasted_iota(jnp.int32, logits.shape, 1)
    vals, ids = [], []
    for _ in range(TOP_K):
        mx = jnp.max(logits, axis=1, keepdims=True)
        ix = jnp.min(jnp.where(logits == mx, lane, ne), axis=1, keepdims=True)
        vals.append(mx)
        ids.append(ix)
        logits = jnp.where(lane == ix, -jnp.inf, logits)
    v = jnp.concatenate(vals, axis=1)
    e = jnp.exp(v - v[:, 0:1])
    idx_ref[...] = jnp.concatenate(ids, axis=1)
    wt_ref[...] = e / jnp.sum(e, axis=1, keepdims=True)

    tri = (lax.broadcasted_iota(jnp.int32, (tm, tm), 0)
           > lax.broadcasted_iota(jnp.int32, (tm, tm), 1)).astype(BF16)
    base = run_ref[...]
    ranks = []
    for k in range(TOP_K):
        oh = (lane == ids[k]).astype(F32)
        prefix = jnp.dot(tri, oh.astype(BF16), preferred_element_type=F32)
        ranks.append(jnp.sum(oh * (base + prefix), axis=1, keepdims=True))
        base = base + jnp.sum(oh, axis=0, keepdims=True)
    run_ref[...] = base
    rank_ref[...] = jnp.concatenate(ranks, axis=1).astype(jnp.int32)
    cnt_ref[...] = base.astype(jnp.int32)


def _router(h, gain, w_router, b_router, tm):
    m, d = h.shape
    ne = w_router.shape[1]
    tm = _pick(m, tm, SUBLANES)
    xr = _x_slab_rows(d)
    wr_hi = w_router.astype(BF16)
    wr_lo = (w_router.astype(F32) - wr_hi.astype(F32)).astype(BF16)
    return pl.pallas_call(
        functools.partial(_router_kernel, xr=xr),
        grid=(m // tm,),
        in_specs=[
            pl.BlockSpec((tm, d), lambda i: (i, 0)),
            pl.BlockSpec((1, d), lambda i: (0, 0)),
            pl.BlockSpec((d, ne), lambda i: (0, 0)),
            pl.BlockSpec((d, ne), lambda i: (0, 0)),
            pl.BlockSpec((1, ne), lambda i: (0, 0)),
        ],
        out_specs=[
            pl.BlockSpec((tm * xr, LANES), lambda i: (i, 0)),
            pl.BlockSpec((tm, TOP_K), lambda i: (i, 0)),
            pl.BlockSpec((tm, TOP_K), lambda i: (i, 0)),
            pl.BlockSpec((tm, TOP_K), lambda i: (i, 0)),
            pl.BlockSpec((1, ne), lambda i: (0, 0)),
        ],
        out_shape=[
            jax.ShapeDtypeStruct((m * xr, LANES), jnp.uint32),
            jax.ShapeDtypeStruct((m, TOP_K), jnp.int32),
            jax.ShapeDtypeStruct((m, TOP_K), F32),
            jax.ShapeDtypeStruct((m, TOP_K), jnp.int32),
            jax.ShapeDtypeStruct((1, ne), jnp.int32),
        ],
        scratch_shapes=[pltpu.VMEM((1, ne), F32)],
        compiler_params=_cparams(("arbitrary",)),
        name="router",
    )(h, gain.reshape(1, d).astype(F32), wr_hi, wr_lo, b_router.reshape(1, ne).astype(F32))


def _dispatch_kernel(dest_ref, zlo_ref, zhi_ref, tot_ref, hp_ref, xs_hbm, stage, zbuf, sem, zsem,
                     *, per_step, npad, zrows, sr):
    i = pl.program_id(0)
    n = pl.num_programs(0)
    ne = zlo_ref.shape[0]
    tt = per_step // TOP_K

    def zero_row(r):
        return pltpu.make_async_copy(zbuf.at[pl.ds(0, sr)], xs_hbm.at[pl.ds(r * sr, sr)],
                                     zsem.at[0])

    def zero_chunk(c):
        return pltpu.make_async_copy(zbuf, xs_hbm.at[pl.ds(c * (zrows * sr), zrows * sr)],
                                     zsem.at[0])

    def for_zero_fills(fn_row, fn_chunk):
        def per_expert(e, _):
            lax.fori_loop(zlo_ref[e], zhi_ref[e], lambda r, _: (fn_row(r), 0)[1], 0)
            return 0
        lax.fori_loop(0, ne, per_expert, 0)
        lax.fori_loop(tot_ref[0] // zrows, npad // zrows, lambda c, _: (fn_chunk(c), 0)[1], 0)

    @pl.when(i == 0)
    def _():
        zbuf[...] = jnp.zeros_like(zbuf)
        for_zero_fills(lambda r: zero_row(r).start(), lambda c: zero_chunk(c).start())

    slot = i % 2
    stage[slot] = hp_ref[...]

    def body(lt, _):
        src = stage.at[slot, pl.ds(lt * sr, sr)]
        for k in range(TOP_K):
            pltpu.make_async_copy(src, xs_hbm.at[pl.ds(dest_ref[i, lt * TOP_K + k] * sr, sr)],
                                  sem.at[slot]).start(priority=k % 2)
        return 0
    lax.fori_loop(0, tt, body, 0)

    def wait_step(s):
        for _ in range(TOP_K):
            pltpu.make_async_copy(stage.at[s], xs_hbm.at[pl.ds(0, tt * sr)], sem.at[s]).wait()

    @pl.when(i > 0)
    def _():
        wait_step((i + 1) % 2)

    @pl.when(i == n - 1)
    def _():
        wait_step(i % 2)
        for_zero_fills(lambda r: zero_row(r).wait(), lambda c: zero_chunk(c).wait())


def _dispatch(hp, dest2d, zero_lo, zero_hi, total, npad, sr):
    nsteps, per_step = dest2d.shape
    w = hp.shape[1]
    zrows = 64
    assert npad % zrows == 0
    return pl.pallas_call(
        functools.partial(_dispatch_kernel, per_step=per_step, npad=npad, zrows=zrows, sr=sr),
        grid_spec=pltpu.PrefetchScalarGridSpec(
            num_scalar_prefetch=4,
            grid=(nsteps,),
            in_specs=[pl.BlockSpec((per_step // TOP_K * sr, w), lambda i, *_: (i, 0))],
            out_specs=pl.BlockSpec(memory_space=pl.ANY),
            scratch_shapes=[pltpu.VMEM((2, per_step // TOP_K * sr, w), hp.dtype),
                            pltpu.VMEM((zrows * sr, w), hp.dtype),
                            pltpu.SemaphoreType.DMA((2,)), pltpu.SemaphoreType.DMA((1,))],
        ),
        out_shape=jax.ShapeDtypeStruct((npad * sr, w), hp.dtype),
        compiler_params=_cparams(("arbitrary",)),
        name="dispatch",
    )(dest2d, zero_lo, zero_hi, total, hp)


def _moe_kernel(be_ref, na_ref, x_ref, wg_ref, wu_ref, bg_ref, bu_ref, wd_ref, bd_ref,
                o_ref, xb_ref, act_ref, *, bm, nc1, nc2, xr, yr):
    i = pl.program_id(0)
    c = pl.program_id(1)
    active = i < na_ref[0]
    tn = act_ref.shape[2]

    @pl.when(jnp.logical_and(active, c == 0))
    def _():
        w = LANES
        half = xr * w
        for s in range(xr):
            word = x_ref[pl.ds(s, bm, stride=xr), :]
            xb_ref[:, s * w:(s + 1) * w] = pltpu.bitcast(word << 16, F32).astype(BF16)
            xb_ref[:, half + s * w:half + (s + 1) * w] = pltpu.bitcast(
                word & jnp.uint32(0xFFFF0000), F32).astype(BF16)

    @pl.when(jnp.logical_and(active, c < nc1))
    def _():
        xb = xb_ref[...]
        g = jnp.dot(xb, wg_ref[0], preferred_element_type=F32) + bg_ref[0]
        u = jnp.dot(xb, wu_ref[0], preferred_element_type=F32) + bu_ref[0]
        gate = jnp.minimum(g, SWIGLU_LIMIT)
        up = jnp.clip(u, -SWIGLU_LIMIT, SWIGLU_LIMIT)
        act_ref[c] = (gate * jax.nn.sigmoid(SWIGLU_ALPHA * gate) * (up + 1.0)).astype(BF16)

    @pl.when(jnp.logical_and(active, c >= nc1))
    def _():
        y = bd_ref[0] + jnp.dot(act_ref[0], wd_ref[0, 0:tn, :], preferred_element_type=F32)
        for cc in range(1, nc1):
            y += jnp.dot(act_ref[cc], wd_ref[0, cc * tn:(cc + 1) * tn, :],
                         preferred_element_type=F32)
        half = y.shape[1] // 2
        lo = pltpu.bitcast(y[:, :half].astype(BF16).astype(F32), jnp.uint32) >> 16
        hi = pltpu.bitcast(y[:, half:].astype(BF16).astype(F32), jnp.uint32) & jnp.uint32(0xFFFF0000)
        word = lo | hi
        per = half // LANES
        for j in range(per):
            o_ref[pl.ds((c - nc1) * per + j, bm, stride=yr), :] = word[:, j * LANES:(j + 1) * LANES]

    @pl.when(jnp.logical_and(jnp.logical_not(active), c == 0))
    def _():
        o_ref[...] = jnp.zeros_like(o_ref)


def _moe(xs, blk_expert, n_active, w_gu, b_gu, w_down, b_down, bm, tn, tcol):
    w = xs.shape[1]
    ne, de, d = w_down.shape
    xr, yr = _x_slab_rows(d), _y_slab_rows(d)
    npad = xs.shape[0] // xr
    nc1 = de // tn
    nc2 = d // tcol
    nblk = npad // bm

    def blk(i, na):
        return jnp.minimum(i, na[0] - 1)

    def c1(c):
        return jnp.minimum(c, nc1 - 1)

    def c2(c):
        return jnp.maximum(c - nc1, 0)

    in_specs = [
        pl.BlockSpec((bm * xr, w), lambda i, c, be, na: (blk(i, na), 0)),
        pl.BlockSpec((1, d, tn), lambda i, c, be, na: (be[blk(i, na)], 0, c1(c))),
        pl.BlockSpec((1, d, tn), lambda i, c, be, na: (be[blk(i, na)], 0, nc1 + c1(c))),
        pl.BlockSpec((1, 1, tn), lambda i, c, be, na: (be[blk(i, na)], 0, c1(c))),
        pl.BlockSpec((1, 1, tn), lambda i, c, be, na: (be[blk(i, na)], 0, nc1 + c1(c))),
        pl.BlockSpec((1, de, tcol), lambda i, c, be, na: (be[blk(i, na)], 0, c2(c))),
        pl.BlockSpec((1, 1, tcol), lambda i, c, be, na: (be[blk(i, na)], 0, c2(c))),
    ]
    return pl.pallas_call(
        functools.partial(_moe_kernel, bm=bm, nc1=nc1, nc2=nc2, xr=xr, yr=yr),
        grid_spec=pltpu.PrefetchScalarGridSpec(
            num_scalar_prefetch=2,
            grid=(nblk, nc1 + nc2),
            in_specs=in_specs,
            out_specs=pl.BlockSpec((bm * yr, w), lambda i, c, be, na: (i, 0)),
            scratch_shapes=[pltpu.VMEM((bm, d), BF16), pltpu.VMEM((nc1, bm, tn), BF16)],
        ),
        out_shape=jax.ShapeDtypeStruct((npad * yr, w), jnp.uint32),
        compiler_params=_cparams(("arbitrary", "arbitrary")),
        name="moe",
    )(blk_expert, n_active, xs, w_gu, w_gu, b_gu, b_gu, w_down, b_down)


def _combine_kernel(idx_ref, y_hbm, h_ref, wt_ref, g_ref, o_ref, buf, sem, *, rows, tt, yr, per):
    i = pl.program_id(0)
    n = pl.num_programs(0)
    w = buf.shape[2]

    def issue(step, slot):
        def body(r4, _):
            for j in range(4):
                r = r4 * 4 + j
                src = idx_ref[step, r]
                pltpu.make_async_copy(y_hbm.at[pl.ds(src * yr, yr)],
                                      buf.at[slot, pl.ds(r * yr, yr)],
                                      sem.at[slot]).start(priority=j % 2)
            return 0
        lax.fori_loop(0, rows // 4, body, 0)

    @pl.when(i == 0)
    def _():
        issue(0, 0)

    @pl.when(i + 1 < n)
    def _():
        issue(i + 1, (i + 1) % 2)

    slot = i % 2
    pltpu.make_async_copy(y_hbm.at[pl.ds(0, rows * yr)], buf.at[slot], sem.at[slot]).wait()
    cur = buf.at[slot]
    wts = wt_ref[...]
    los, his = [], []
    for s in range(yr):
        lo = hi = None
        for k in range(TOP_K):
            word = cur[pl.ds(k * tt * yr + s, tt, stride=yr), :]
            wk = wts[:, k:k + 1]
            tl = wk * pltpu.bitcast(word << 16, F32)
            th = wk * pltpu.bitcast(word & jnp.uint32(0xFFFF0000), F32)
            lo = tl if lo is None else lo + tl
            hi = th if hi is None else hi + th
        los.append(lo)
        his.append(hi)
    pieces = []
    for n0 in range(0, yr, per):
        pieces += los[n0:n0 + per] + his[n0:n0 + per]
    acc = h_ref[...] + jnp.concatenate(pieces, axis=1)
    ms = jnp.mean(acc * acc, axis=-1, keepdims=True)
    o_ref[...] = acc * lax.rsqrt(ms + EPS) * g_ref[...]


def _combine(ys, idx2d, h, top_w, gain, tt, tcol):
    nsteps, rows = idx2d.shape
    d = h.shape[1]
    w = ys.shape[1]
    yr = _y_slab_rows(d)
    return pl.pallas_call(
        functools.partial(_combine_kernel, rows=rows, tt=tt, yr=yr, per=tcol // (2 * LANES)),
        grid_spec=pltpu.PrefetchScalarGridSpec(
            num_scalar_prefetch=1,
            grid=(nsteps,),
            in_specs=[
                pl.BlockSpec(memory_space=pl.ANY),
                pl.BlockSpec((tt, d), lambda i, idx: (i, 0)),
                pl.BlockSpec((tt, TOP_K), lambda i, idx: (i, 0)),
                pl.BlockSpec((1, d), lambda i, idx: (0, 0)),
            ],
            out_specs=pl.BlockSpec((tt, d), lambda i, idx: (i, 0)),
            scratch_shapes=[pltpu.VMEM((2, rows * yr, w), jnp.uint32),
                            pltpu.SemaphoreType.DMA((2,))],
        ),
        out_shape=jax.ShapeDtypeStruct(h.shape, F32),
        compiler_params=_cparams(("arbitrary",)),
        name="combine",
    )(idx2d, ys, h, top_w, gain.reshape(1, d).astype(F32))


def _dispatch_plan(top_idx, rank, counts, bm, nblk):
    ne = counts.shape[0]
    padded = ((counts + bm - 1) // bm) * bm
    pad_end = jnp.cumsum(padded)
    pad_start = pad_end - padded
    onehot = top_idx[..., None] == jnp.arange(ne, dtype=jnp.int32)
    dest = rank + jnp.sum(jnp.where(onehot, pad_start, 0), axis=-1)
    blk_row0 = jnp.arange(nblk, dtype=jnp.int32) * bm
    blk_expert = jnp.minimum(
        jnp.sum((pad_end[None, :] <= blk_row0[:, None]).astype(jnp.int32), axis=1), ne - 1)
    total = pad_end[-1:].astype(jnp.int32)
    return (dest.astype(jnp.int32), blk_expert, total // bm,
            (pad_start + counts).astype(jnp.int32), pad_end.astype(jnp.int32), total)


def _rope_tables(n_pos):
    pos = jnp.arange(n_pos, dtype=F32)
    inv_freq = ROPE_THETA ** (-jnp.arange(0, QK_ROPE, 2, dtype=F32) / QK_ROPE)
    ang = pos[:, None] * inv_freq[None, :]
    cos, sin = jnp.cos(ang), jnp.sin(ang)
    zero = jnp.zeros((n_pos, LANES - QK_ROPE), F32)
    return jnp.concatenate([cos, cos, zero], axis=1), jnp.concatenate([-sin, sin, zero], axis=1)


def _with_swapped_rope(w_rope):
    half = QK_ROPE // 2
    return jnp.concatenate([w_rope, w_rope[..., half:], w_rope[..., :half]], axis=-1)


def kernel(x, meta_tokens, g_mix, w_in, b_gate, g_q_lat, g_kv_lat, w_uq, w_ukv, s5_a_re, s5_a_im, s5_log_dt, s5_b_re, s5_b_im, s5_c_re, s5_c_im, s5_d, w_glu_a, w_glu_b, w_out, g_ffn, w_router, b_router, w_gu, b_gu, w_down, b_down, g_final):
    assert w_in.shape[0] == 1, "single-layer block"
    nb, seq, d = x.shape
    t = nb * seq
    n_meta = meta_tokens.shape[0]
    ql, kvl = g_q_lat.shape[-1], g_kv_lat.shape[-1]
    sw = s5_d.shape[-1]
    nh = w_uq.shape[-1] // QK_DIM
    ne = w_router.shape[-1]
    assert ql % kvl == 0 and (ql + kvl) % LANES == 0 and sw % LANES == 0 and d % LANES == 0
    assert n_meta % (2 * SUBLANES) == 0 and n_meta <= LANES

    w0 = w_in[0]
    c_kv = ql + kvl
    c_kr = c_kv + QK_ROPE
    w_main = jnp.concatenate([w0[:, :c_kv], w0[:, c_kr:]], axis=1).astype(BF16)
    w_kr = _with_swapped_rope(w0[:, c_kv:c_kr]).astype(BF16)
    o_s5 = c_kv
    o_gs = o_s5 + sw
    o_gm = o_gs + d
    wq3 = w_uq[0].reshape(ql, nh, QK_DIM)
    w_q = jnp.concatenate([wq3[..., :QK_NOPE], _with_swapped_rope(wq3[..., QK_NOPE:])],
                          axis=-1).reshape(ql, nh * HEAD_PAD).astype(BF16)
    w_kv = w_ukv[0].astype(BF16)
    cos_t, sin_t = _rope_tables(n_meta + seq)
    rope_meta = (cos_t[:n_meta], sin_t[:n_meta])
    rope_real = (cos_t[n_meta:], sin_t[n_meta:])
    s5_tabs = _s5_tables(s5_a_re[0], s5_a_im[0], s5_log_dt[0], s5_b_re[0], s5_b_im[0],
                         s5_c_re[0], s5_c_im[0], s5_d[0])
    bw, cw, a_tab, g_tab_fn, d_skip = s5_tabs

    x2 = x.reshape(t, d)
    tm = 512

    proj_m = _norm_mm(meta_tokens, 0, g_mix[0], w_main, BF16, n_meta, 512, name="proj_meta")
    kr_m = _norm_mm(meta_tokens, 0, g_mix[0], w_kr, BF16, n_meta, LANES,
                    rope=(*rope_meta, LANES), name="krope_meta")
    kv_m = _norm_mm(proj_m, ql // kvl, g_kv_lat[0], w_kv, BF16, n_meta, 1024, name="kv_meta")
    kvm = jnp.pad(kv_m, ((0, LANES - n_meta), (0, 0)))
    krm = jnp.pad(kr_m, ((0, LANES - n_meta), (0, 0)))
    ntile = sw // LANES
    ns = bw.shape[2] // 2
    _, c_meta = _s5_scan(proj_m, o_s5 // LANES, 1, n_meta, n_meta,
                         (bw, cw, a_tab, g_tab_fn(n_meta // SUBLANES), d_skip),
                         jnp.zeros((ntile, 2, ns), F32), "s5_meta")

    proj = _norm_mm(x2, 0, g_mix[0], w_main, BF16, tm, 1024, name="proj")
    kr = _norm_mm(x2, 0, g_mix[0], w_kr, BF16, tm, LANES, rope=(*rope_real, LANES), name="krope")
    q = _norm_mm(proj, 0, g_q_lat[0], w_q, BF16, tm, 1024, rope=(*rope_real, HEAD_PAD),
                 out_scale=math.log2(math.e) * QK_DIM ** -0.5, name="q")
    kv = _norm_mm(proj, ql // kvl, g_kv_lat[0], w_kv, BF16, tm, 1024, name="kv")

    tc = _pick(seq, 512, 2 * SUBLANES)
    z, _ = _s5_scan(proj, o_s5 // LANES, nb, seq, tc,
                    (bw, cw, a_tab, g_tab_fn(tc // SUBLANES), d_skip), c_meta[0], "s5")
    bias = b_gate[0].astype(F32).reshape(1, 2 * d)
    ys = _glu(z, w_glu_a[0].astype(BF16), w_glu_b[0].astype(BF16), proj, o_gs, bias[:, :d], 1024, 512)
    ym = _attention(q, kv, kr, kvm, krm, proj, o_gm, bias[:, d:], nb, seq, nh, n_meta, 2048, 512)
    h = _outproj(ys, ym, w_out[0].astype(BF16), x2, tm, 512)

    assert _x_slab_rows(d) % SUBLANES == 0
    hp, top_idx, top_w, rank, counts = _router(h, g_ffn[0], w_router[0], b_router[0], 256)
    bm = 512 if t * TOP_K >= 512 * ne else 128
    nblk = (t * TOP_K + ne * (bm - 1) + bm - 1) // bm
    dest, blk_expert, n_active, zero_lo, zero_hi, total = _dispatch_plan(
        top_idx, rank, counts[0], bm, nblk)
    per_step = 512
    xs = _dispatch(hp, dest.reshape(t * TOP_K // per_step, per_step), zero_lo, zero_hi, total,
                   nblk * bm, _x_slab_rows(d))
    de = w_down.shape[2]
    tcol = _pick(d, 1024)
    ysort = _moe(xs, blk_expert, n_active, w_gu[0].astype(BF16), b_gu[0].reshape(ne, 1, 2 * de),
                 w_down[0].astype(BF16), b_down[0].reshape(ne, 1, d), bm, _pick(de, 512), tcol)
    tt = 64
    idx_c = dest.reshape(t // tt, tt, TOP_K).transpose(0, 2, 1).reshape(t // tt, TOP_K * tt)
    out = _combine(ysort, idx_c, h, top_w, g_final, tt, tcol)
    return out.reshape(nb, seq, d)
```

```python
import functools
import math

import jax
import jax.numpy as jnp
from jax import lax
from jax.experimental import pallas as pl
from jax.experimental.pallas import tpu as pltpu

F32 = jnp.float32
BF16 = jnp.bfloat16

EPS = 1e-6
QK_NOPE = 128
QK_ROPE = 64
QK_DIM = QK_NOPE + QK_ROPE
V_HEAD = 128
ROPE_THETA = 10000.0
NEG_INF = -1e30
S5_GROUP = 16
TOP_K = 4
SWIGLU_ALPHA = 1.702
SWIGLU_LIMIT = 7.0

LANES = 128
SUBLANES = 8
HEAD_PAD = 2 * LANES
S5_TILE_GROUPS = LANES // S5_GROUP
VMEM_LIMIT_V7X = 56 * 1024 * 1024


def _cparams(sem, vmem=VMEM_LIMIT_V7X):
    return pltpu.CompilerParams(dimension_semantics=sem, vmem_limit_bytes=vmem)


def _x_slab_rows(d):
    return d // (2 * LANES)


def _y_slab_rows(d):
    return d // (2 * LANES)


def _pick(n, pref, mult=LANES):
    if n <= pref:
        return n
    best = None
    for t in range(mult, pref + 1, mult):
        if n % t == 0:
            best = t
    assert best is not None, (n, pref, mult)
    return best


def _norm_mm_kernel(a_ref, g_ref, w_ref, *rest, rope_group, out_scale):
    if rope_group:
        cos_ref, sin_ref, o_ref, an_ref = rest
    else:
        o_ref, an_ref = rest

    @pl.when(pl.program_id(1) == 0)
    def _():
        xf = a_ref[...].astype(F32)
        ms = jnp.mean(xf * xf, axis=-1, keepdims=True)
        an_ref[...] = (xf * lax.rsqrt(ms + EPS) * g_ref[...]).astype(BF16)

    acc = jnp.dot(an_ref[...], w_ref[...], preferred_element_type=F32)
    if out_scale is not None:
        acc = acc * out_scale
    if not rope_group:
        o_ref[...] = acc.astype(o_ref.dtype)
        return
    c = cos_ref[...]
    s = sin_ref[...]
    for g0 in range(0, acc.shape[1], rope_group):
        lo = g0 + rope_group - LANES
        if lo > g0:
            o_ref[:, g0:lo] = acc[:, g0:lo].astype(o_ref.dtype)
        slab = acc[:, lo:lo + LANES]
        o_ref[:, lo:lo + LANES] = (slab * c + pltpu.roll(slab, QK_ROPE, 1) * s).astype(o_ref.dtype)


def _norm_mm(a, col_blk, gain, w, out_dtype, tm, tn, rope=None, out_scale=None, name=None):
    m = a.shape[0]
    k, n = w.shape
    tm = _pick(m, tm, SUBLANES)
    tn = _pick(n, tn)
    in_specs = [
        pl.BlockSpec((tm, k), lambda i, j: (i, col_blk)),
        pl.BlockSpec((1, k), lambda i, j: (0, 0)),
        pl.BlockSpec((k, tn), lambda i, j: (0, j)),
    ]
    args = [a, gain.reshape(1, k).astype(F32), w]
    group = 0
    if rope is not None:
        cos_t, sin_t, group = rope
        nrb = cos_t.shape[0] // tm
        in_specs += [pl.BlockSpec((tm, LANES), lambda i, j: (i % nrb, 0))] * 2
        args += [cos_t, sin_t]
    return pl.pallas_call(
        functools.partial(_norm_mm_kernel, rope_group=group, out_scale=out_scale),
        grid=(m // tm, n // tn),
        in_specs=in_specs,
        out_specs=pl.BlockSpec((tm, tn), lambda i, j: (i, j)),
        out_shape=jax.ShapeDtypeStruct((m, n), out_dtype),
        scratch_shapes=[pltpu.VMEM((tm, k), BF16)],
        compiler_params=_cparams(("parallel", "arbitrary")),
        name=name,
    )(*args)


def _s5_kernel(u_ref, perm_ref, permt_ref, bw_ref, cw_ref, a_ref, g_ref, d_ref, c0_ref,
               z_ref, cout_ref, bu_ref, xb_ref, carry_ref, *, tc, ns):
    t = pl.program_id(2)
    tseg = tc // SUBLANES

    @pl.when(t == 0)
    def _():
        carry_ref[...] = c0_ref[0]

    u_p = jnp.dot(perm_ref[...], u_ref[...], preferred_element_type=F32).astype(BF16)
    bu_ref[...] = jnp.dot(u_p, bw_ref[0], preferred_element_type=F32)

    ar = jnp.broadcast_to(a_ref[0, 0:1, :], (SUBLANES, ns))
    ai = jnp.broadcast_to(a_ref[0, 1:2, :], (SUBLANES, ns))

    def local_scan(i, x):
        xr, xi = x
        r0 = pl.multiple_of(i * SUBLANES, SUBLANES)
        nr = ar * xr - ai * xi + bu_ref[pl.ds(r0, SUBLANES), 0:ns]
        ni = ar * xi + ai * xr + bu_ref[pl.ds(r0, SUBLANES), ns:2 * ns]
        bu_ref[pl.ds(r0, SUBLANES), 0:ns] = nr
        bu_ref[pl.ds(r0, SUBLANES), ns:2 * ns] = ni
        return nr, ni

    zero = jnp.zeros((SUBLANES, ns), F32)
    er, ei = lax.fori_loop(0, tseg, local_scan, (zero, zero))

    c_r = carry_ref[0:1, :]
    c_i = carry_ref[1:2, :]
    sub = lax.broadcasted_iota(jnp.int32, (SUBLANES, ns), 0)
    g1r = g_ref[0, 0:1, :]
    g1i = g_ref[0, 1:2, :]
    fr = er + jnp.where(sub == 0, g1r * c_r - g1i * c_i, 0.0)
    fi = ei + jnp.where(sub == 0, g1r * c_i + g1i * c_r, 0.0)
    for k, d in enumerate((1, 2, 4)):
        gr = g_ref[0, 2 * k:2 * k + 1, :]
        gi = g_ref[0, 2 * k + 1:2 * k + 2, :]
        sr = jnp.where(sub >= d, pltpu.roll(fr, d, 0), 0.0)
        si = jnp.where(sub >= d, pltpu.roll(fi, d, 0), 0.0)
        fr, fi = fr + gr * sr - gi * si, fi + gr * si + gi * sr
    cin_r = jnp.where(sub == 0, c_r, pltpu.roll(fr, 1, 0))
    cin_i = jnp.where(sub == 0, c_i, pltpu.roll(fi, 1, 0))
    carry_ref[0:1, :] = fr[SUBLANES - 1:SUBLANES, :]
    carry_ref[1:2, :] = fi[SUBLANES - 1:SUBLANES, :]

    def add_carry(j, c):
        cr, ci = c
        c1r, c1i = ar * cr - ai * ci, ar * ci + ai * cr
        c2r, c2i = ar * c1r - ai * c1i, ar * c1i + ai * c1r
        r0 = pl.multiple_of(j * 2 * SUBLANES, 2 * SUBLANES)
        r1 = r0 + SUBLANES
        xr = jnp.concatenate([bu_ref[pl.ds(r0, SUBLANES), 0:ns] + c1r,
                              bu_ref[pl.ds(r1, SUBLANES), 0:ns] + c2r], axis=0)
        xi = jnp.concatenate([bu_ref[pl.ds(r0, SUBLANES), ns:2 * ns] + c1i,
                              bu_ref[pl.ds(r1, SUBLANES), ns:2 * ns] + c2i], axis=0)
        xb_ref[pl.ds(r0, 2 * SUBLANES), 0:ns] = xr.astype(BF16)
        xb_ref[pl.ds(r0, 2 * SUBLANES), ns:2 * ns] = xi.astype(BF16)
        return c2r, c2i

    lax.fori_loop(0, tseg // 2, add_carry, (cin_r, cin_i))

    y = jnp.dot(xb_ref[...], cw_ref[0], preferred_element_type=F32) + d_ref[...] * u_p.astype(F32)
    zp = jax.nn.gelu(y).astype(BF16)
    z_ref[...] = jnp.dot(permt_ref[...], zp, preferred_element_type=F32).astype(BF16)

    @pl.when(t == pl.num_programs(2) - 1)
    def _():
        cout_ref[0, 0] = carry_ref[...]


def _s5_scan(u_arr, u_col0, nb, seq, tc, tabs, c0, name):
    bw, cw, a_tab, g_tab, d_skip = tabs
    ntile, _, ns2 = bw.shape
    ns = ns2 // 2
    nt = seq // tc
    tseg = tc // SUBLANES
    r = jnp.arange(tc, dtype=jnp.int32)
    src = (r % SUBLANES) * tseg + r // SUBLANES
    perm = (src[:, None] == jnp.arange(tc, dtype=jnp.int32)[None, :]).astype(BF16)
    kern = functools.partial(_s5_kernel, tc=tc, ns=ns)
    return pl.pallas_call(
        kern,
        grid=(nb, ntile, nt),
        in_specs=[
            pl.BlockSpec((tc, LANES), lambda b, n, t: (b * nt + t, u_col0 + n)),
            pl.BlockSpec((tc, tc), lambda b, n, t: (0, 0)),
            pl.BlockSpec((tc, tc), lambda b, n, t: (0, 0)),
            pl.BlockSpec((1, LANES, 2 * ns), lambda b, n, t: (n, 0, 0)),
            pl.BlockSpec((1, 2 * ns, LANES), lambda b, n, t: (n, 0, 0)),
            pl.BlockSpec((1, 2, ns), lambda b, n, t: (n, 0, 0)),
            pl.BlockSpec((1, 6, ns), lambda b, n, t: (n, 0, 0)),
            pl.BlockSpec((1, LANES), lambda b, n, t: (0, n)),
            pl.BlockSpec((1, 2, ns), lambda b, n, t: (n, 0, 0)),
        ],
        out_specs=[
            pl.BlockSpec((tc, LANES), lambda b, n, t: (b * nt + t, n)),
            pl.BlockSpec((1, 1, 2, ns), lambda b, n, t: (b, n, 0, 0)),
        ],
        out_shape=[
            jax.ShapeDtypeStruct((nb * seq, ntile * LANES), BF16),
            jax.ShapeDtypeStruct((nb, ntile, 2, ns), F32),
        ],
        scratch_shapes=[
            pltpu.VMEM((tc, 2 * ns), F32),
            pltpu.VMEM((tc, 2 * ns), BF16),
            pltpu.VMEM((2, ns), F32),
        ],
        compiler_params=_cparams(("parallel", "parallel", "arbitrary")),
        name=name,
    )(u_arr, perm, perm.T, bw, cw, a_tab, g_tab, d_skip, c0)


def _s5_tables(a_re, a_im, log_dt, b_re, b_im, c_re, c_im, d_skip):
    ngrp, nst = a_re.shape
    ntile = ngrp // S5_TILE_GROUPS
    ns = S5_TILE_GROUPS * nst
    dt = jnp.exp(log_dt.astype(F32))[:, None]
    lr, li = a_re.astype(F32), a_im.astype(F32)

    def a_pow(k):
        mag = jnp.exp(lr * dt * k)
        return mag * jnp.cos(li * dt * k), mag * jnp.sin(li * dt * k)

    ab_re, ab_im = a_pow(1)
    den = lr * lr + li * li
    f_re = ((ab_re - 1.0) * lr + ab_im * li) / den
    f_im = (ab_im * lr - (ab_re - 1.0) * li) / den
    br, bi = b_re.astype(F32), b_im.astype(F32)
    bb_re = f_re[..., None] * br - f_im[..., None] * bi
    bb_im = f_re[..., None] * bi + f_im[..., None] * br
    eye = jnp.eye(S5_TILE_GROUPS, dtype=F32)

    def b_tile(bb):
        bb = bb.reshape(ntile, S5_TILE_GROUPS, nst, S5_GROUP)
        return jnp.einsum("ab,napc->nacbp", eye, bb).reshape(ntile, LANES, ns)

    def c_tile(cc):
        cc = cc.astype(F32).reshape(ntile, S5_TILE_GROUPS, S5_GROUP, nst)
        return jnp.einsum("ab,nacp->nbpac", eye, cc).reshape(ntile, ns, LANES)

    bw = jnp.concatenate([b_tile(bb_re), b_tile(bb_im)], axis=2).astype(BF16)
    cw = jnp.concatenate([c_tile(c_re), -c_tile(c_im)], axis=1).astype(BF16)
    a_tab = jnp.stack([ab_re.reshape(ntile, ns), ab_im.reshape(ntile, ns)], axis=1)

    def g_tab(tseg):
        rows = []
        for d in (1, 2, 4):
            pr, pi = a_pow(tseg * d)
            rows += [pr.reshape(ntile, ns), pi.reshape(ntile, ns)]
        return jnp.stack(rows, axis=1)

    return bw, cw, a_tab, g_tab, d_skip.astype(F32).reshape(1, -1)


def _glu_kernel(z_ref, wa_ref, wb_ref, gs_ref, bg_ref, o_ref):
    z = z_ref[...]
    a = jnp.dot(z, wa_ref[...], preferred_element_type=F32)
    b = jnp.dot(z, wb_ref[...], preferred_element_type=F32)
    gate = jax.nn.sigmoid(gs_ref[...].astype(F32) + bg_ref[...])
    o_ref[...] = (a * jax.nn.sigmoid(b) * gate).astype(o_ref.dtype)


def _glu(z, wa, wb, proj, gs_col0, bias, tm, tn):
    m, k = z.shape
    n = wa.shape[1]
    tm = _pick(m, tm, SUBLANES)
    tn = _pick(math.gcd(n, gs_col0) if gs_col0 else n, tn)
    goff = gs_col0 // tn
    return pl.pallas_call(
        _glu_kernel,
        grid=(m // tm, n // tn),
        in_specs=[
            pl.BlockSpec((tm, k), lambda i, j: (i, 0)),
            pl.BlockSpec((k, tn), lambda i, j: (0, j)),
            pl.BlockSpec((k, tn), lambda i, j: (0, j)),
            pl.BlockSpec((tm, tn), lambda i, j: (i, goff + j)),
            pl.BlockSpec((1, tn), lambda i, j: (0, j)),
        ],
        out_specs=pl.BlockSpec((tm, tn), lambda i, j: (i, j)),
        out_shape=jax.ShapeDtypeStruct((m, n), BF16),
        compiler_params=_cparams(("parallel", "arbitrary")),
        name="glu",
    )(z, wa, wb, proj, bias)


def _attn_kernel(q_ref, kn_ref, v_ref, kr_ref, km_ref, vm_ref, krm_ref, gm_ref, bg_ref, o_ref,
                 *, tq, tk, n_meta):
    qi = pl.program_id(2)

    def update(q, carry, j, mask_row0):
        m, l, acc = carry
        r0 = pl.multiple_of(j * tk, tk)
        k = jnp.concatenate([kn_ref[pl.ds(r0, tk), :], kr_ref[pl.ds(r0, tk), :]], axis=1)
        s = lax.dot_general(q, k, (((1,), (1,)), ((), ())), preferred_element_type=F32)
        if mask_row0 is not None:
            row = mask_row0 + lax.broadcasted_iota(jnp.int32, s.shape, 0)
            colp = j * tk + lax.broadcasted_iota(jnp.int32, s.shape, 1)
            s = jnp.where(colp <= row, s, NEG_INF)
        m_new = jnp.maximum(m, jnp.max(s, axis=1, keepdims=True))
        alpha = jnp.exp2(m - m_new)
        p = jnp.exp2(s - m_new)
        l = alpha * l + jnp.sum(p, axis=1, keepdims=True)
        acc = alpha * acc + jnp.dot(p.astype(BF16), v_ref[pl.ds(r0, tk), :],
                                    preferred_element_type=F32)
        return m_new, l, acc

    q = q_ref[...]
    km = jnp.concatenate([km_ref[...], krm_ref[...]], axis=1)
    s = lax.dot_general(q, km, (((1,), (1,)), ((), ())), preferred_element_type=F32)
    col = lax.broadcasted_iota(jnp.int32, s.shape, 1)
    s = jnp.where(col < n_meta, s, NEG_INF)
    m = jnp.max(s, axis=1, keepdims=True)
    p = jnp.exp2(s - m)
    l = jnp.sum(p, axis=1, keepdims=True)
    acc = jnp.dot(p.astype(BF16), vm_ref[...], preferred_element_type=F32)

    n_full = (qi * tq) // tk
    m, l, acc = lax.fori_loop(0, n_full, lambda j, c: update(q, c, j, None), (m, l, acc))

    for r in range(tq // tk):
        rows = slice(r * tk, (r + 1) * tk)
        qr = q_ref[rows, :]
        carry = (m[rows], l[rows], acc[rows])
        for dd in range(r):
            carry = update(qr, carry, n_full + dd, None)
        _, lr, ar = update(qr, carry, n_full + r, qi * tq + r * tk)
        gate = jax.nn.sigmoid(gm_ref[rows, :].astype(F32) + bg_ref[...])
        o_ref[rows, :] = (ar / lr * gate).astype(o_ref.dtype)


def _attention(q, kv, kr, kvm, krm, proj, gm_col0, bias, nb, seq, nh, n_meta, tq, tk):
    tq = _pick(seq, tq, SUBLANES)
    tk = _pick(tq, tk, SUBLANES)
    nq = seq // tq
    mrows = kvm.shape[0]
    goff = gm_col0 // LANES
    kern = functools.partial(_attn_kernel, tq=tq, tk=tk, n_meta=n_meta)
    return pl.pallas_call(
        kern,
        grid=(nb, nh, nq),
        in_specs=[
            pl.BlockSpec((tq, HEAD_PAD), lambda b, h, i: (b * nq + i, h)),
            pl.BlockSpec((seq, LANES), lambda b, h, i: (b, 2 * h)),
            pl.BlockSpec((seq, LANES), lambda b, h, i: (b, 2 * h + 1)),
            pl.BlockSpec((seq, LANES), lambda b, h, i: (b, 0)),
            pl.BlockSpec((mrows, LANES), lambda b, h, i: (0, 2 * h)),
            pl.BlockSpec((mrows, LANES), lambda b, h, i: (0, 2 * h + 1)),
            pl.BlockSpec((mrows, LANES), lambda b, h, i: (0, 0)),
            pl.BlockSpec((tq, LANES), lambda b, h, i: (b * nq + i, goff + h)),
            pl.BlockSpec((1, LANES), lambda b, h, i: (0, h)),
        ],
        out_specs=pl.BlockSpec((tq, LANES), lambda b, h, i: (b * nq + i, h)),
        out_shape=jax.ShapeDtypeStruct((nb * seq, nh * V_HEAD), BF16),
        compiler_params=_cparams(("parallel", "parallel", "arbitrary")),
        name="attention",
    )(q, kv, kv, kr, kvm, kvm, krm, proj, bias)


def _outproj_kernel(ys_ref, ym_ref, w_ref, res_ref, o_ref, a_ref):
    @pl.when(pl.program_id(1) == 0)
    def _():
        a_ref[...] = (ys_ref[...].astype(F32) + ym_ref[...].astype(F32)).astype(BF16)

    o_ref[...] = jnp.dot(a_ref[...], w_ref[...], preferred_element_type=F32) + res_ref[...]


def _outproj(ys, ym, w, res, tm, tn):
    m, k = ys.shape
    n = w.shape[1]
    tm = _pick(m, tm, SUBLANES)
    tn = _pick(n, tn)
    return pl.pallas_call(
        _outproj_kernel,
        grid=(m // tm, n // tn),
        in_specs=[
            pl.BlockSpec((tm, k), lambda i, j: (i, 0)),
            pl.BlockSpec((tm, k), lambda i, j: (i, 0)),
            pl.BlockSpec((k, tn), lambda i, j: (0, j)),
            pl.BlockSpec((tm, tn), lambda i, j: (i, j)),
        ],
        out_specs=pl.BlockSpec((tm, tn), lambda i, j: (i, j)),
        out_shape=jax.ShapeDtypeStruct((m, n), F32),
        scratch_shapes=[pltpu.VMEM((tm, k), BF16)],
        compiler_params=_cparams(("parallel", "arbitrary")),
        name="outproj",
    )(ys, ym, w, res)


def _router_kernel(h_ref, g_ref, wr_ref, wrl_ref, br_ref, hp_ref, idx_ref, wt_ref, rank_ref,
                   cnt_ref, run_ref, *, xr):
    @pl.when(pl.program_id(0) == 0)
    def _():
        run_ref[...] = jnp.zeros_like(run_ref)

    xf = h_ref[...]
    tm, d = xf.shape
    ms = jnp.mean(xf * xf, axis=-1, keepdims=True)
    hn = xf * lax.rsqrt(ms + EPS) * g_ref[...]

    half = d // 2
    bits = pltpu.bitcast(hn.astype(BF16).astype(F32), jnp.uint32)
    packed = (bits[:, :half] >> 16) | (bits[:, half:] & jnp.uint32(0xFFFF0000))
    for s in range(xr):
        hp_ref[pl.ds(s, tm, stride=xr), :] = packed[:, s * LANES:(s + 1) * LANES]

    hn_hi = hn.astype(BF16)
    hn_lo = (hn - hn_hi.astype(F32)).astype(BF16)
    logits = (jnp.dot(hn_hi, wr_ref[...], preferred_element_type=F32)
              + jnp.dot(hn_lo, wr_ref[...], preferred_element_type=F32)
              + jnp.dot(hn_hi, wrl_ref[...], preferred_element_type=F32)) + br_ref[...]
    ne = logits.shape[1]
    lane = lax.broadcasted_iota(jnp.int32, logits.shape, 1)
    vals, ids = [], []
    for _ in range(TOP_K):
        mx = jnp.max(logits, axis=1, keepdims=True)
        ix = jnp.min(jnp.where(logits == mx, lane, ne), axis=1, keepdims=True)
        vals.append(mx)
        ids.append(ix)
        logits = jnp.where(lane == ix, -jnp.inf, logits)
    v = jnp.concatenate(vals, axis=1)
    e = jnp.exp(v - v[:, 0:1])
    idx_ref[...] = jnp.concatenate(ids, axis=1)
    wt_ref[...] = e / jnp.sum(e, axis=1, keepdims=True)

    tri = (lax.broadcasted_iota(jnp.int32, (tm, tm), 0)
           > lax.broadcasted_iota(jnp.int32, (tm, tm), 1)).astype(BF16)
    base = run_ref[...]
    ranks = []
    for k in range(TOP_K):
        oh = (lane == ids[k]).astype(F32)
        prefix = jnp.dot(tri, oh.astype(BF16), preferred_element_type=F32)
        ranks.append(jnp.sum(oh * (base + prefix), axis=1, keepdims=True))
        base = base + jnp.sum(oh, axis=0, keepdims=True)
    run_ref[...] = base
    rank_ref[...] = jnp.concatenate(ranks, axis=1).astype(jnp.int32)
    cnt_ref[...] = base.astype(jnp.int32)


def _router(h, gain, w_router, b_router, tm):
    m, d = h.shape
    ne = w_router.shape[1]
    tm = _pick(m, tm, SUBLANES)
    xr = _x_slab_rows(d)
    wr_hi = w_router.astype(BF16)
    wr_lo = (w_router.astype(F32) - wr_hi.astype(F32)).astype(BF16)
    return pl.pallas_call(
        functools.partial(_router_kernel, xr=xr),
        grid=(m // tm,),
        in_specs=[
            pl.BlockSpec((tm, d), lambda i: (i, 0)),
            pl.BlockSpec((1, d), lambda i: (0, 0)),
            pl.BlockSpec((d, ne), lambda i: (0, 0)),
            pl.BlockSpec((d, ne), lambda i: (0, 0)),
            pl.BlockSpec((1, ne), lambda i: (0, 0)),
        ],
        out_specs=[
            pl.BlockSpec((tm * xr, LANES), lambda i: (i, 0)),
            pl.BlockSpec((tm, TOP_K), lambda i: (i, 0)),
            pl.BlockSpec((tm, TOP_K), lambda i: (i, 0)),
            pl.BlockSpec((tm, TOP_K), lambda i: (i, 0)),
            pl.BlockSpec((1, ne), lambda i: (0, 0)),
        ],
        out_shape=[
            jax.ShapeDtypeStruct((m * xr, LANES), jnp.uint32),
            jax.ShapeDtypeStruct((m, TOP_K), jnp.int32),
            jax.ShapeDtypeStruct((m, TOP_K), F32),
            jax.ShapeDtypeStruct((m, TOP_K), jnp.int32),
            jax.ShapeDtypeStruct((1, ne), jnp.int32),
        ],
        scratch_shapes=[pltpu.VMEM((1, ne), F32)],
        compiler_params=_cparams(("arbitrary",)),
        name="router",
    )(h, gain.reshape(1, d).astype(F32), wr_hi, wr_lo, b_router.reshape(1, ne).astype(F32))


def _dispatch_kernel(dest_ref, zlo_ref, zhi_ref, tot_ref, hp_ref, xs_hbm, stage, zbuf, sem, zsem,
                     *, per_step, npad, zrows, sr):
    i = pl.program_id(0)
    n = pl.num_programs(0)
    ne = zlo_ref.shape[0]
    tt = per_step // TOP_K

    def zero_row(r):
        return pltpu.make_async_copy(zbuf.at[pl.ds(0, sr)], xs_hbm.at[pl.ds(r * sr, sr)],
                                     zsem.at[0])

    def zero_chunk(c):
        return pltpu.make_async_copy(zbuf, xs_hbm.at[pl.ds(c * (zrows * sr), zrows * sr)],
                                     zsem.at[0])

    def for_zero_fills(fn_row, fn_chunk):
        def per_expert(e, _):
            lax.fori_loop(zlo_ref[e], zhi_ref[e], lambda r, _: (fn_row(r), 0)[1], 0)
            return 0
        lax.fori_loop(0, ne, per_expert, 0)
        lax.fori_loop(tot_ref[0] // zrows, npad // zrows, lambda c, _: (fn_chunk(c), 0)[1], 0)

    @pl.when(i == 0)
    def _():
        zbuf[...] = jnp.zeros_like(zbuf)
        for_zero_fills(lambda r: zero_row(r).start(), lambda c: zero_chunk(c).start())

    slot = i % 2
    stage[slot] = hp_ref[...]

    def body(lt, _):
        src = stage.at[slot, pl.ds(lt * sr, sr)]
        for k in range(TOP_K):
            pltpu.make_async_copy(src, xs_hbm.at[pl.ds(dest_ref[i, lt * TOP_K + k] * sr, sr)],
                                  sem.at[slot]).start(priority=k % 2)
        return 0
    lax.fori_loop(0, tt, body, 0)

    def wait_step(s):
        for _ in range(TOP_K):
            pltpu.make_async_copy(stage.at[s], xs_hbm.at[pl.ds(0, tt * sr)], sem.at[s]).wait()

    @pl.when(i > 0)
    def _():
        wait_step((i + 1) % 2)

    @pl.when(i == n - 1)
    def _():
        wait_step(i % 2)
        for_zero_fills(lambda r: zero_row(r).wait(), lambda c: zero_chunk(c).wait())


def _dispatch(hp, dest2d, zero_lo, zero_hi, total, npad, sr):
    nsteps, per_step = dest2d.shape
    w = hp.shape[1]
    zrows = 64
    assert npad % zrows == 0
    return pl.pallas_call(
        functools.partial(_dispatch_kernel, per_step=per_step, npad=npad, zrows=zrows, sr=sr),
        grid_spec=pltpu.PrefetchScalarGridSpec(
            num_scalar_prefetch=4,
            grid=(nsteps,),
            in_specs=[pl.BlockSpec((per_step // TOP_K * sr, w), lambda i, *_: (i, 0))],
            out_specs=pl.BlockSpec(memory_space=pl.ANY),
            scratch_shapes=[pltpu.VMEM((2, per_step // TOP_K * sr, w), hp.dtype),
                            pltpu.VMEM((zrows * sr, w), hp.dtype),
                            pltpu.SemaphoreType.DMA((2,)), pltpu.SemaphoreType.DMA((1,))],
        ),
        out_shape=jax.ShapeDtypeStruct((npad * sr, w), hp.dtype),
        compiler_params=_cparams(("arbitrary",)),
        name="dispatch",
    )(dest2d, zero_lo, zero_hi, total, hp)


def _moe_kernel(be_ref, na_ref, x_ref, wg_ref, wu_ref, bg_ref, bu_ref, wd_ref, bd_ref,
                o_ref, xb_ref, act_ref, *, bm, nc1, nc2, xr, yr):
    i = pl.program_id(0)
    c = pl.program_id(1)
    active = i < na_ref[0]
    tn = act_ref.shape[2]

    @pl.when(jnp.logical_and(active, c == 0))
    def _():
        w = LANES
        half = xr * w
        for s in range(xr):
            word = x_ref[pl.ds(s, bm, stride=xr), :]
            xb_ref[:, s * w:(s + 1) * w] = pltpu.bitcast(word << 16, F32).astype(BF16)
            xb_ref[:, half + s * w:half + (s + 1) * w] = pltpu.bitcast(
                word & jnp.uint32(0xFFFF0000), F32).astype(BF16)

    @pl.when(jnp.logical_and(active, c < nc1))
    def _():
        xb = xb_ref[...]
        g = jnp.dot(xb, wg_ref[0], preferred_element_type=F32) + bg_ref[0]
        u = jnp.dot(xb, wu_ref[0], preferred_element_type=F32) + bu_ref[0]
        gate = jnp.minimum(g, SWIGLU_LIMIT)
        up = jnp.clip(u, -SWIGLU_LIMIT, SWIGLU_LIMIT)
        act_ref[c] = (gate * jax.nn.sigmoid(SWIGLU_ALPHA * gate) * (up + 1.0)).astype(BF16)

    @pl.when(jnp.logical_and(active, c >= nc1))
    def _():
        y = bd_ref[0] + jnp.dot(act_ref[0], wd_ref[0, 0:tn, :], preferred_element_type=F32)
        for cc in range(1, nc1):
            y += jnp.dot(act_ref[cc], wd_ref[0, cc * tn:(cc + 1) * tn, :],
                         preferred_element_type=F32)
        half = y.shape[1] // 2
        lo = pltpu.bitcast(y[:, :half].astype(BF16).astype(F32), jnp.uint32) >> 16
        hi = pltpu.bitcast(y[:, half:].astype(BF16).astype(F32), jnp.uint32) & jnp.uint32(0xFFFF0000)
        word = lo | hi
        per = half // LANES
        for j in range(per):
            o_ref[pl.ds((c - nc1) * per + j, bm, stride=yr), :] = word[:, j * LANES:(j + 1) * LANES]

    @pl.when(jnp.logical_and(jnp.logical_not(active), c == 0))
    def _():
        o_ref[...] = jnp.zeros_like(o_ref)


def _moe(xs, blk_expert, n_active, w_gu, b_gu, w_down, b_down, bm, tn, tcol):
    w = xs.shape[1]
    ne, de, d = w_down.shape
    xr, yr = _x_slab_rows(d), _y_slab_rows(d)
    npad = xs.shape[0] // xr
    nc1 = de // tn
    nc2 = d // tcol
    nblk = npad // bm

    def blk(i, na):
        return jnp.clip(i, 0, na[0] - 1)

    def x_map(i, c, be, na):
        return (jnp.where(c >= nc1, blk(i + 1, na), blk(i, na)), 0)

    def gu_map(first_next, col0):
        def index(i, c, be, na):
            nxt = c >= first_next
            e = be[jnp.where(nxt, blk(i + 1, na), blk(i, na))]
            return (e, 0, col0 + jnp.where(nxt, 0, jnp.minimum(c, nc1 - 1)))
        return index

    def down_map(i, c, be, na):
        cur = c >= nc1 - 1
        e = be[jnp.where(cur, blk(i, na), blk(i - 1, na))]
        return (e, 0, jnp.where(cur, jnp.maximum(c - nc1, 0), nc2 - 1))

    g_next = nc1 + min(1, nc2 - 1)
    u_next = nc1 + min(2, nc2 - 1)
    in_specs = [
        pl.BlockSpec((bm * xr, w), x_map),
        pl.BlockSpec((1, d, tn), gu_map(g_next, 0)),
        pl.BlockSpec((1, d, tn), gu_map(u_next, nc1)),
        pl.BlockSpec((1, 1, tn), gu_map(g_next, 0)),
        pl.BlockSpec((1, 1, tn), gu_map(u_next, nc1)),
        pl.BlockSpec((1, de, tcol), down_map),
        pl.BlockSpec((1, 1, tcol), down_map),
    ]
    return pl.pallas_call(
        functools.partial(_moe_kernel, bm=bm, nc1=nc1, nc2=nc2, xr=xr, yr=yr),
        grid_spec=pltpu.PrefetchScalarGridSpec(
            num_scalar_prefetch=2,
            grid=(nblk, nc1 + nc2),
            in_specs=in_specs,
            out_specs=pl.BlockSpec((bm * yr, w), lambda i, c, be, na: (i, 0)),
            scratch_shapes=[pltpu.VMEM((bm, d), BF16), pltpu.VMEM((nc1, bm, tn), BF16)],
        ),
        out_shape=jax.ShapeDtypeStruct((npad * yr, w), jnp.uint32),
        compiler_params=_cparams(("arbitrary", "arbitrary")),
        name="moe",
    )(blk_expert, n_active, xs, w_gu, w_gu, b_gu, b_gu, w_down, b_down)


def _combine_kernel(idx_ref, y_hbm, h_ref, wt_ref, g_ref, o_ref, buf, sem, *, rows, tt, yr, per):
    i = pl.program_id(0)
    n = pl.num_programs(0)
    w = buf.shape[2]

    def issue(step, slot):
        def body(r4, _):
            for j in range(4):
                r = r4 * 4 + j
                src = idx_ref[step, r]
                pltpu.make_async_copy(y_hbm.at[pl.ds(src * yr, yr)],
                                      buf.at[slot, pl.ds(r * yr, yr)],
                                      sem.at[slot]).start(priority=j % 2)
            return 0
        lax.fori_loop(0, rows // 4, body, 0)

    @pl.when(i == 0)
    def _():
        issue(0, 0)

    @pl.when(i + 1 < n)
    def _():
        issue(i + 1, (i + 1) % 2)

    slot = i % 2
    pltpu.make_async_copy(y_hbm.at[pl.ds(0, rows * yr)], buf.at[slot], sem.at[slot]).wait()
    cur = buf.at[slot]
    wts = wt_ref[...]
    los, his = [], []
    for s in range(yr):
        lo = hi = None
        for k in range(TOP_K):
            word = cur[pl.ds(k * tt * yr + s, tt, stride=yr), :]
            wk = wts[:, k:k + 1]
            tl = wk * pltpu.bitcast(word << 16, F32)
            th = wk * pltpu.bitcast(word & jnp.uint32(0xFFFF0000), F32)
            lo = tl if lo is None else lo + tl
            hi = th if hi is None else hi + th
        los.append(lo)
        his.append(hi)
    pieces = []
    for n0 in range(0, yr, per):
        pieces += los[n0:n0 + per] + his[n0:n0 + per]
    acc = h_ref[...] + jnp.concatenate(pieces, axis=1)
    ms = jnp.mean(acc * acc, axis=-1, keepdims=True)
    o_ref[...] = acc * lax.rsqrt(ms + EPS) * g_ref[...]


def _combine(ys, idx2d, h, top_w, gain, tt, tcol):
    nsteps, rows = idx2d.shape
    d = h.shape[1]
    w = ys.shape[1]
    yr = _y_slab_rows(d)
    return pl.pallas_call(
        functools.partial(_combine_kernel, rows=rows, tt=tt, yr=yr, per=tcol // (2 * LANES)),
        grid_spec=pltpu.PrefetchScalarGridSpec(
            num_scalar_prefetch=1,
            grid=(nsteps,),
            in_specs=[
                pl.BlockSpec(memory_space=pl.ANY),
                pl.BlockSpec((tt, d), lambda i, idx: (i, 0)),
                pl.BlockSpec((tt, TOP_K), lambda i, idx: (i, 0)),
                pl.BlockSpec((1, d), lambda i, idx: (0, 0)),
            ],
            out_specs=pl.BlockSpec((tt, d), lambda i, idx: (i, 0)),
            scratch_shapes=[pltpu.VMEM((2, rows * yr, w), jnp.uint32),
                            pltpu.SemaphoreType.DMA((2,))],
        ),
        out_shape=jax.ShapeDtypeStruct(h.shape, F32),
        compiler_params=_cparams(("arbitrary",)),
        name="combine",
    )(idx2d, ys, h, top_w, gain.reshape(1, d).astype(F32))


def _dispatch_plan(top_idx, rank, counts, bm, nblk):
    ne = counts.shape[0]
    padded = ((counts + bm - 1) // bm) * bm
    pad_end = jnp.cumsum(padded)
    pad_start = pad_end - padded
    onehot = top_idx[..., None] == jnp.arange(ne, dtype=jnp.int32)
    dest = rank + jnp.sum(jnp.where(onehot, pad_start, 0), axis=-1)
    blk_row0 = jnp.arange(nblk, dtype=jnp.int32) * bm
    blk_expert = jnp.minimum(
        jnp.sum((pad_end[None, :] <= blk_row0[:, None]).astype(jnp.int32), axis=1), ne - 1)
    total = pad_end[-1:].astype(jnp.int32)
    return (dest.astype(jnp.int32), blk_expert, total // bm,
            (pad_start + counts).astype(jnp.int32), pad_end.astype(jnp.int32), total)


def _rope_tables(n_pos):
    pos = jnp.arange(n_pos, dtype=F32)
    inv_freq = ROPE_THETA ** (-jnp.arange(0, QK_ROPE, 2, dtype=F32) / QK_ROPE)
    ang = pos[:, None] * inv_freq[None, :]
    cos, sin = jnp.cos(ang), jnp.sin(ang)
    zero = jnp.zeros((n_pos, LANES - QK_ROPE), F32)
    return jnp.concatenate([cos, cos, zero], axis=1), jnp.concatenate([-sin, sin, zero], axis=1)


def _with_swapped_rope(w_rope):
    half = QK_ROPE // 2
    return jnp.concatenate([w_rope, w_rope[..., half:], w_rope[..., :half]], axis=-1)


def kernel(x, meta_tokens, g_mix, w_in, b_gate, g_q_lat, g_kv_lat, w_uq, w_ukv, s5_a_re, s5_a_im, s5_log_dt, s5_b_re, s5_b_im, s5_c_re, s5_c_im, s5_d, w_glu_a, w_glu_b, w_out, g_ffn, w_router, b_router, w_gu, b_gu, w_down, b_down, g_final):
    assert w_in.shape[0] == 1, "single-layer block"
    nb, seq, d = x.shape
    t = nb * seq
    n_meta = meta_tokens.shape[0]
    ql, kvl = g_q_lat.shape[-1], g_kv_lat.shape[-1]
    sw = s5_d.shape[-1]
    nh = w_uq.shape[-1] // QK_DIM
    ne = w_router.shape[-1]
    assert ql % kvl == 0 and (ql + kvl) % LANES == 0 and sw % LANES == 0 and d % LANES == 0
    assert n_meta % (2 * SUBLANES) == 0 and n_meta <= LANES

    w0 = w_in[0]
    c_kv = ql + kvl
    c_kr = c_kv + QK_ROPE
    w_main = jnp.concatenate([w0[:, :c_kv], w0[:, c_kr:]], axis=1).astype(BF16)
    w_kr = _with_swapped_rope(w0[:, c_kv:c_kr]).astype(BF16)
    o_s5 = c_kv
    o_gs = o_s5 + sw
    o_gm = o_gs + d
    wq3 = w_uq[0].reshape(ql, nh, QK_DIM)
    w_q = jnp.concatenate([wq3[..., :QK_NOPE], _with_swapped_rope(wq3[..., QK_NOPE:])],
                          axis=-1).reshape(ql, nh * HEAD_PAD).astype(BF16)
    w_kv = w_ukv[0].astype(BF16)
    cos_t, sin_t = _rope_tables(n_meta + seq)
    rope_meta = (cos_t[:n_meta], sin_t[:n_meta])
    rope_real = (cos_t[n_meta:], sin_t[n_meta:])
    s5_tabs = _s5_tables(s5_a_re[0], s5_a_im[0], s5_log_dt[0], s5_b_re[0], s5_b_im[0],
                         s5_c_re[0], s5_c_im[0], s5_d[0])
    bw, cw, a_tab, g_tab_fn, d_skip = s5_tabs

    x2 = x.reshape(t, d)
    tm = 512

    proj_m = _norm_mm(meta_tokens, 0, g_mix[0], w_main, BF16, n_meta, 512, name="proj_meta")
    kr_m = _norm_mm(meta_tokens, 0, g_mix[0], w_kr, BF16, n_meta, LANES,
                    rope=(*rope_meta, LANES), name="krope_meta")
    kv_m = _norm_mm(proj_m, ql // kvl, g_kv_lat[0], w_kv, BF16, n_meta, 1024, name="kv_meta")
    kvm = jnp.pad(kv_m, ((0, LANES - n_meta), (0, 0)))
    krm = jnp.pad(kr_m, ((0, LANES - n_meta), (0, 0)))
    ntile = sw // LANES
    ns = bw.shape[2] // 2
    _, c_meta = _s5_scan(proj_m, o_s5 // LANES, 1, n_meta, n_meta,
                         (bw, cw, a_tab, g_tab_fn(n_meta // SUBLANES), d_skip),
                         jnp.zeros((ntile, 2, ns), F32), "s5_meta")

    proj = _norm_mm(x2, 0, g_mix[0], w_main, BF16, tm, 1024, name="proj")
    kr = _norm_mm(x2, 0, g_mix[0], w_kr, BF16, tm, LANES, rope=(*rope_real, LANES), name="krope")
    q = _norm_mm(proj, 0, g_q_lat[0], w_q, BF16, tm, 1024, rope=(*rope_real, HEAD_PAD),
                 out_scale=math.log2(math.e) * QK_DIM ** -0.5, name="q")
    kv = _norm_mm(proj, ql // kvl, g_kv_lat[0], w_kv, BF16, tm, 1024, name="kv")

    tc = _pick(seq, 512, 2 * SUBLANES)
    z, _ = _s5_scan(proj, o_s5 // LANES, nb, seq, tc,
                    (bw, cw, a_tab, g_tab_fn(tc // SUBLANES), d_skip), c_meta[0], "s5")
    bias = b_gate[0].astype(F32).reshape(1, 2 * d)
    ys = _glu(z, w_glu_a[0].astype(BF16), w_glu_b[0].astype(BF16), proj, o_gs, bias[:, :d], 1024, 512)
    ym = _attention(q, kv, kr, kvm, krm, proj, o_gm, bias[:, d:], nb, seq, nh, n_meta, 2048, 512)
    h = _outproj(ys, ym, w_out[0].astype(BF16), x2, tm, 512)

    assert _x_slab_rows(d) % SUBLANES == 0
    hp, top_idx, top_w, rank, counts = _router(h, g_ffn[0], w_router[0], b_router[0], 256)
    bm = 512 if t * TOP_K >= 512 * ne else 128
    nblk = (t * TOP_K + ne * (bm - 1) + bm - 1) // bm
    dest, blk_expert, n_active, zero_lo, zero_hi, total = _dispatch_plan(
        top_idx, rank, counts[0], bm, nblk)
    per_step = 512
    xs = _dispatch(hp, dest.reshape(t * TOP_K // per_step, per_step), zero_lo, zero_hi, total,
                   nblk * bm, _x_slab_rows(d))
    de = w_down.shape[2]
    tcol = _pick(d, 1024)
    ysort = _moe(xs, blk_expert, n_active, w_gu[0].astype(BF16), b_gu[0].reshape(ne, 1, 2 * de),
                 w_down[0].astype(BF16), b_down[0].reshape(ne, 1, d), bm, _pick(de, 512), tcol)
    tt = 64
    idx_c = dest.reshape(t // tt, tt, TOP_K).transpose(0, 2, 1).reshape(t // tt, TOP_K * tt)
    out = _combine(ysort, idx_c, h, top_w, g_final, tt, tcol)
    return out.reshape(nb, seq, d)
```

```python
import functools
import math

import jax
import jax.numpy as jnp
from jax import lax
from jax.experimental import pallas as pl
from jax.experimental.pallas import tpu as pltpu

F32 = jnp.float32
BF16 = jnp.bfloat16

EPS = 1e-6
QK_NOPE = 128
QK_ROPE = 64
QK_DIM = QK_NOPE + QK_ROPE
V_HEAD = 128
ROPE_THETA = 10000.0
NEG_INF = -1e30
S5_GROUP = 16
TOP_K = 4
SWIGLU_ALPHA = 1.702
SWIGLU_LIMIT = 7.0

LANES = 128
SUBLANES = 8
HEAD_PAD = 2 * LANES
S5_TILE_GROUPS = LANES // S5_GROUP
VMEM_LIMIT_V7X = 56 * 1024 * 1024


def _cparams(sem, vmem=VMEM_LIMIT_V7X):
    return pltpu.CompilerParams(dimension_semantics=sem, vmem_limit_bytes=vmem)


def _pack_bf16_pairs(x):
    half = x.shape[1] // 2
    bits = pltpu.bitcast(x.astype(BF16).astype(F32), jnp.uint32)
    return (bits[:, :half] >> 16) | (bits[:, half:] & jnp.uint32(0xFFFF0000))


def _unpack_bf16_pairs(word):
    return (pltpu.bitcast(word << 16, F32), pltpu.bitcast(word & jnp.uint32(0xFFFF0000), F32))


def _pick(n, pref, mult=LANES):
    if n <= pref:
        return n
    best = None
    for t in range(mult, pref + 1, mult):
        if n % t == 0:
            best = t
    assert best is not None, (n, pref, mult)
    return best


def _norm_mm_kernel(a_ref, g_ref, w_ref, *rest, rope_group, out_scale):
    if rope_group:
        cos_ref, sin_ref, o_ref, an_ref = rest
    else:
        o_ref, an_ref = rest

    @pl.when(pl.program_id(1) == 0)
    def _():
        xf = a_ref[...].astype(F32)
        ms = jnp.mean(xf * xf, axis=-1, keepdims=True)
        an_ref[...] = (xf * lax.rsqrt(ms + EPS) * g_ref[...]).astype(BF16)

    acc = jnp.dot(an_ref[...], w_ref[...], preferred_element_type=F32)
    if out_scale is not None:
        acc = acc * out_scale
    if not rope_group:
        o_ref[...] = acc.astype(o_ref.dtype)
        return
    c = cos_ref[...]
    s = sin_ref[...]
    for g0 in range(0, acc.shape[1], rope_group):
        lo = g0 + rope_group - LANES
        if lo > g0:
            o_ref[:, g0:lo] = acc[:, g0:lo].astype(o_ref.dtype)
        slab = acc[:, lo:lo + LANES]
        o_ref[:, lo:lo + LANES] = (slab * c + pltpu.roll(slab, QK_ROPE, 1) * s).astype(o_ref.dtype)


def _norm_mm(a, col_blk, gain, w, out_dtype, tm, tn, rope=None, out_scale=None, name=None):
    m = a.shape[0]
    k, n = w.shape
    tm = _pick(m if rope is None else math.gcd(m, rope[0].shape[0]), tm, SUBLANES)
    tn = _pick(n, tn)
    in_specs = [
        pl.BlockSpec((tm, k), lambda i, j: (i, col_blk)),
        pl.BlockSpec((1, k), lambda i, j: (0, 0)),
        pl.BlockSpec((k, tn), lambda i, j: (0, j)),
    ]
    args = [a, gain.reshape(1, k).astype(F32), w]
    group = 0
    if rope is not None:
        cos_t, sin_t, group = rope
        nrb = cos_t.shape[0] // tm
        in_specs += [pl.BlockSpec((tm, LANES), lambda i, j: (i % nrb, 0))] * 2
        args += [cos_t, sin_t]
    return pl.pallas_call(
        functools.partial(_norm_mm_kernel, rope_group=group, out_scale=out_scale),
        grid=(m // tm, n // tn),
        in_specs=in_specs,
        out_specs=pl.BlockSpec((tm, tn), lambda i, j: (i, j)),
        out_shape=jax.ShapeDtypeStruct((m, n), out_dtype),
        scratch_shapes=[pltpu.VMEM((tm, k), BF16)],
        compiler_params=_cparams(("parallel", "arbitrary")),
        name=name,
    )(*args)


def _s5_kernel(u_ref, perm_ref, permt_ref, bw_ref, cw_ref, a_ref, g_ref, d_ref, c0_ref,
               z_ref, cout_ref, bu_ref, xb_ref, carry_ref, *, tc, ns):
    t = pl.program_id(2)
    tseg = tc // SUBLANES

    @pl.when(t == 0)
    def _():
        carry_ref[...] = c0_ref[0]

    u_p = jnp.dot(perm_ref[...], u_ref[...], preferred_element_type=F32).astype(BF16)
    bu_ref[...] = jnp.dot(u_p, bw_ref[0], preferred_element_type=F32)

    ar = jnp.broadcast_to(a_ref[0, 0:1, :], (SUBLANES, ns))
    ai = jnp.broadcast_to(a_ref[0, 1:2, :], (SUBLANES, ns))

    def local_scan(i, x):
        xr, xi = x
        r0 = pl.multiple_of(i * SUBLANES, SUBLANES)
        nr = ar * xr - ai * xi + bu_ref[pl.ds(r0, SUBLANES), 0:ns]
        ni = ar * xi + ai * xr + bu_ref[pl.ds(r0, SUBLANES), ns:2 * ns]
        bu_ref[pl.ds(r0, SUBLANES), 0:ns] = nr
        bu_ref[pl.ds(r0, SUBLANES), ns:2 * ns] = ni
        return nr, ni

    zero = jnp.zeros((SUBLANES, ns), F32)
    er, ei = lax.fori_loop(0, tseg, local_scan, (zero, zero))

    c_r = carry_ref[0:1, :]
    c_i = carry_ref[1:2, :]
    sub = lax.broadcasted_iota(jnp.int32, (SUBLANES, ns), 0)
    g1r = g_ref[0, 0:1, :]
    g1i = g_ref[0, 1:2, :]
    fr = er + jnp.where(sub == 0, g1r * c_r - g1i * c_i, 0.0)
    fi = ei + jnp.where(sub == 0, g1r * c_i + g1i * c_r, 0.0)
    for k, d in enumerate((1, 2, 4)):
        gr = g_ref[0, 2 * k:2 * k + 1, :]
        gi = g_ref[0, 2 * k + 1:2 * k + 2, :]
        sr = jnp.where(sub >= d, pltpu.roll(fr, d, 0), 0.0)
        si = jnp.where(sub >= d, pltpu.roll(fi, d, 0), 0.0)
        fr, fi = fr + gr * sr - gi * si, fi + gr * si + gi * sr
    cin_r = jnp.where(sub == 0, c_r, pltpu.roll(fr, 1, 0))
    cin_i = jnp.where(sub == 0, c_i, pltpu.roll(fi, 1, 0))
    carry_ref[0:1, :] = fr[SUBLANES - 1:SUBLANES, :]
    carry_ref[1:2, :] = fi[SUBLANES - 1:SUBLANES, :]

    def add_carry(j, c):
        cr, ci = c
        c1r, c1i = ar * cr - ai * ci, ar * ci + ai * cr
        c2r, c2i = ar * c1r - ai * c1i, ar * c1i + ai * c1r
        r0 = pl.multiple_of(j * 2 * SUBLANES, 2 * SUBLANES)
        r1 = r0 + SUBLANES
        xr = jnp.concatenate([bu_ref[pl.ds(r0, SUBLANES), 0:ns] + c1r,
                              bu_ref[pl.ds(r1, SUBLANES), 0:ns] + c2r], axis=0)
        xi = jnp.concatenate([bu_ref[pl.ds(r0, SUBLANES), ns:2 * ns] + c1i,
                              bu_ref[pl.ds(r1, SUBLANES), ns:2 * ns] + c2i], axis=0)
        xb_ref[pl.ds(r0, 2 * SUBLANES), 0:ns] = xr.astype(BF16)
        xb_ref[pl.ds(r0, 2 * SUBLANES), ns:2 * ns] = xi.astype(BF16)
        return c2r, c2i

    lax.fori_loop(0, tseg // 2, add_carry, (cin_r, cin_i))

    y = jnp.dot(xb_ref[...], cw_ref[0], preferred_element_type=F32) + d_ref[...] * u_p.astype(F32)
    zp = jax.nn.gelu(y).astype(BF16)
    z_ref[...] = jnp.dot(permt_ref[...], zp, preferred_element_type=F32).astype(BF16)

    @pl.when(t == pl.num_programs(2) - 1)
    def _():
        cout_ref[0, 0] = carry_ref[...]


def _s5_scan(u_arr, u_col0, nb, seq, tc, tabs, c0, name):
    bw, cw, a_tab, g_tab, d_skip = tabs
    ntile, _, ns2 = bw.shape
    ns = ns2 // 2
    nt = seq // tc
    tseg = tc // SUBLANES
    r = jnp.arange(tc, dtype=jnp.int32)
    src = (r % SUBLANES) * tseg + r // SUBLANES
    perm = (src[:, None] == jnp.arange(tc, dtype=jnp.int32)[None, :]).astype(BF16)
    kern = functools.partial(_s5_kernel, tc=tc, ns=ns)
    return pl.pallas_call(
        kern,
        grid=(nb, ntile, nt),
        in_specs=[
            pl.BlockSpec((tc, LANES), lambda b, n, t: (b * nt + t, u_col0 + n)),
            pl.BlockSpec((tc, tc), lambda b, n, t: (0, 0)),
            pl.BlockSpec((tc, tc), lambda b, n, t: (0, 0)),
            pl.BlockSpec((1, LANES, 2 * ns), lambda b, n, t: (n, 0, 0)),
            pl.BlockSpec((1, 2 * ns, LANES), lambda b, n, t: (n, 0, 0)),
            pl.BlockSpec((1, 2, ns), lambda b, n, t: (n, 0, 0)),
            pl.BlockSpec((1, 6, ns), lambda b, n, t: (n, 0, 0)),
            pl.BlockSpec((1, LANES), lambda b, n, t: (0, n)),
            pl.BlockSpec((1, 2, ns), lambda b, n, t: (n, 0, 0)),
        ],
        out_specs=[
            pl.BlockSpec((tc, LANES), lambda b, n, t: (b * nt + t, n)),
            pl.BlockSpec((1, 1, 2, ns), lambda b, n, t: (b, n, 0, 0)),
        ],
        out_shape=[
            jax.ShapeDtypeStruct((nb * seq, ntile * LANES), BF16),
            jax.ShapeDtypeStruct((nb, ntile, 2, ns), F32),
        ],
        scratch_shapes=[
            pltpu.VMEM((tc, 2 * ns), F32),
            pltpu.VMEM((tc, 2 * ns), BF16),
            pltpu.VMEM((2, ns), F32),
        ],
        compiler_params=_cparams(("parallel", "parallel", "arbitrary")),
        name=name,
    )(u_arr, perm, perm.T, bw, cw, a_tab, g_tab, d_skip, c0)


def _s5_tables(a_re, a_im, log_dt, b_re, b_im, c_re, c_im, d_skip):
    ngrp, nst = a_re.shape
    ntile = ngrp // S5_TILE_GROUPS
    ns = S5_TILE_GROUPS * nst
    dt = jnp.exp(log_dt.astype(F32))[:, None]
    lr, li = a_re.astype(F32), a_im.astype(F32)

    def a_pow(k):
        mag = jnp.exp(lr * dt * k)
        return mag * jnp.cos(li * dt * k), mag * jnp.sin(li * dt * k)

    ab_re, ab_im = a_pow(1)
    den = lr * lr + li * li
    f_re = ((ab_re - 1.0) * lr + ab_im * li) / den
    f_im = (ab_im * lr - (ab_re - 1.0) * li) / den
    br, bi = b_re.astype(F32), b_im.astype(F32)
    bb_re = f_re[..., None] * br - f_im[..., None] * bi
    bb_im = f_re[..., None] * bi + f_im[..., None] * br
    eye = jnp.eye(S5_TILE_GROUPS, dtype=F32)

    def b_tile(bb):
        bb = bb.reshape(ntile, S5_TILE_GROUPS, nst, S5_GROUP)
        return jnp.einsum("ab,napc->nacbp", eye, bb).reshape(ntile, LANES, ns)

    def c_tile(cc):
        cc = cc.astype(F32).reshape(ntile, S5_TILE_GROUPS, S5_GROUP, nst)
        return jnp.einsum("ab,nacp->nbpac", eye, cc).reshape(ntile, ns, LANES)

    bw = jnp.concatenate([b_tile(bb_re), b_tile(bb_im)], axis=2).astype(BF16)
    cw = jnp.concatenate([c_tile(c_re), -c_tile(c_im)], axis=1).astype(BF16)
    a_tab = jnp.stack([ab_re.reshape(ntile, ns), ab_im.reshape(ntile, ns)], axis=1)

    def g_tab(tseg):
        rows = []
        for d in (1, 2, 4):
            pr, pi = a_pow(tseg * d)
            rows += [pr.reshape(ntile, ns), pi.reshape(ntile, ns)]
        return jnp.stack(rows, axis=1)

    return bw, cw, a_tab, g_tab, d_skip.astype(F32).reshape(1, -1)


def _glu_kernel(z_ref, wa_ref, wb_ref, gs_ref, bg_ref, o_ref):
    z = z_ref[...]
    a = jnp.dot(z, wa_ref[...], preferred_element_type=F32)
    b = jnp.dot(z, wb_ref[...], preferred_element_type=F32)
    gate = jax.nn.sigmoid(gs_ref[...].astype(F32) + bg_ref[...])
    o_ref[...] = (a * jax.nn.sigmoid(b) * gate).astype(o_ref.dtype)


def _glu(z, wa, wb, proj, gs_col0, bias, tm, tn):
    m, k = z.shape
    n = wa.shape[1]
    tm = _pick(m, tm, SUBLANES)
    tn = _pick(math.gcd(n, gs_col0) if gs_col0 else n, tn)
    goff = gs_col0 // tn
    return pl.pallas_call(
        _glu_kernel,
        grid=(m // tm, n // tn),
        in_specs=[
            pl.BlockSpec((tm, k), lambda i, j: (i, 0)),
            pl.BlockSpec((k, tn), lambda i, j: (0, j)),
            pl.BlockSpec((k, tn), lambda i, j: (0, j)),
            pl.BlockSpec((tm, tn), lambda i, j: (i, goff + j)),
            pl.BlockSpec((1, tn), lambda i, j: (0, j)),
        ],
        out_specs=pl.BlockSpec((tm, tn), lambda i, j: (i, j)),
        out_shape=jax.ShapeDtypeStruct((m, n), BF16),
        compiler_params=_cparams(("parallel", "arbitrary")),
        name="glu",
    )(z, wa, wb, proj, bias)


def _attn_kernel(q_ref, kn_ref, v_ref, kr_ref, km_ref, vm_ref, krm_ref, gm_ref, bg_ref, o_ref,
                 *, tq, tk, n_meta):
    qi = pl.program_id(2)

    def update(q, carry, j, mask_row0):
        m, l, acc = carry
        r0 = pl.multiple_of(j * tk, tk)
        k = jnp.concatenate([kn_ref[pl.ds(r0, tk), :], kr_ref[pl.ds(r0, tk), :]], axis=1)
        s = lax.dot_general(q, k, (((1,), (1,)), ((), ())), preferred_element_type=F32)
        if mask_row0 is not None:
            row = mask_row0 + lax.broadcasted_iota(jnp.int32, s.shape, 0)
            colp = j * tk + lax.broadcasted_iota(jnp.int32, s.shape, 1)
            s = jnp.where(colp <= row, s, NEG_INF)
        m_new = jnp.maximum(m, jnp.max(s, axis=1, keepdims=True))
        alpha = jnp.exp2(m - m_new)
        p = jnp.exp2(s - m_new)
        l = alpha * l + jnp.sum(p, axis=1, keepdims=True)
        acc = alpha * acc + jnp.dot(p.astype(BF16), v_ref[pl.ds(r0, tk), :],
                                    preferred_element_type=F32)
        return m_new, l, acc

    q = q_ref[...]
    km = jnp.concatenate([km_ref[...], krm_ref[...]], axis=1)
    s = lax.dot_general(q, km, (((1,), (1,)), ((), ())), preferred_element_type=F32)
    col = lax.broadcasted_iota(jnp.int32, s.shape, 1)
    s = jnp.where(col < n_meta, s, NEG_INF)
    m = jnp.max(s, axis=1, keepdims=True)
    p = jnp.exp2(s - m)
    l = jnp.sum(p, axis=1, keepdims=True)
    acc = jnp.dot(p.astype(BF16), vm_ref[...], preferred_element_type=F32)

    n_full = (qi * tq) // tk
    m, l, acc = lax.fori_loop(0, n_full, lambda j, c: update(q, c, j, None), (m, l, acc))

    for r in range(tq // tk):
        rows = slice(r * tk, (r + 1) * tk)
        qr = q_ref[rows, :]
        carry = (m[rows], l[rows], acc[rows])
        for dd in range(r):
            carry = update(qr, carry, n_full + dd, None)
        _, lr, ar = update(qr, carry, n_full + r, qi * tq + r * tk)
        gate = jax.nn.sigmoid(gm_ref[rows, :].astype(F32) + bg_ref[...])
        o_ref[rows, :] = (ar / lr * gate).astype(o_ref.dtype)


def _attention(q, kv, kr, kvm, krm, proj, gm_col0, bias, nb, seq, nh, n_meta, tq, tk):
    tq = _pick(seq, tq, SUBLANES)
    tk = _pick(tq, tk, SUBLANES)
    nq = seq // tq
    mrows = kvm.shape[0]
    goff = gm_col0 // LANES
    kern = functools.partial(_attn_kernel, tq=tq, tk=tk, n_meta=n_meta)
    return pl.pallas_call(
        kern,
        grid=(nb, nh, nq),
        in_specs=[
            pl.BlockSpec((tq, HEAD_PAD), lambda b, h, i: (b * nq + i, h)),
            pl.BlockSpec((seq, LANES), lambda b, h, i: (b, 2 * h)),
            pl.BlockSpec((seq, LANES), lambda b, h, i: (b, 2 * h + 1)),
            pl.BlockSpec((seq, LANES), lambda b, h, i: (b, 0)),
            pl.BlockSpec((mrows, LANES), lambda b, h, i: (0, 2 * h)),
            pl.BlockSpec((mrows, LANES), lambda b, h, i: (0, 2 * h + 1)),
            pl.BlockSpec((mrows, LANES), lambda b, h, i: (0, 0)),
            pl.BlockSpec((tq, LANES), lambda b, h, i: (b * nq + i, goff + h)),
            pl.BlockSpec((1, LANES), lambda b, h, i: (0, h)),
        ],
        out_specs=pl.BlockSpec((tq, LANES), lambda b, h, i: (b * nq + i, h)),
        out_shape=jax.ShapeDtypeStruct((nb * seq, nh * V_HEAD), BF16),
        compiler_params=_cparams(("parallel", "parallel", "arbitrary")),
        name="attention",
    )(q, kv, kv, kr, kvm, kvm, krm, proj, bias)


def _outproj_kernel(ys_ref, ym_ref, w_ref, res_ref, o_ref, a_ref):
    @pl.when(pl.program_id(1) == 0)
    def _():
        a_ref[...] = (ys_ref[...].astype(F32) + ym_ref[...].astype(F32)).astype(BF16)

    o_ref[...] = jnp.dot(a_ref[...], w_ref[...], preferred_element_type=F32) + res_ref[...]


def _outproj(ys, ym, w, res, tm, tn):
    m, k = ys.shape
    n = w.shape[1]
    tm = _pick(m, tm, SUBLANES)
    tn = _pick(n, tn)
    return pl.pallas_call(
        _outproj_kernel,
        grid=(m // tm, n // tn),
        in_specs=[
            pl.BlockSpec((tm, k), lambda i, j: (i, 0)),
            pl.BlockSpec((tm, k), lambda i, j: (i, 0)),
            pl.BlockSpec((k, tn), lambda i, j: (0, j)),
            pl.BlockSpec((tm, tn), lambda i, j: (i, j)),
        ],
        out_specs=pl.BlockSpec((tm, tn), lambda i, j: (i, j)),
        out_shape=jax.ShapeDtypeStruct((m, n), F32),
        scratch_shapes=[pltpu.VMEM((tm, k), BF16)],
        compiler_params=_cparams(("parallel", "arbitrary")),
        name="outproj",
    )(ys, ym, w, res)


def _router_kernel(h_ref, g_ref, wr_ref, wrl_ref, br_ref, hp_ref, idx_ref, wt_ref, rank_ref,
                   cnt_ref, run_ref):
    @pl.when(pl.program_id(0) == 0)
    def _():
        run_ref[...] = jnp.zeros_like(run_ref)

    xf = h_ref[...]
    tm, d = xf.shape
    ms = jnp.mean(xf * xf, axis=-1, keepdims=True)
    hn = xf * lax.rsqrt(ms + EPS) * g_ref[...]

    hp_ref[...] = _pack_bf16_pairs(hn)

    hn_hi = hn.astype(BF16)
    hn_lo = (hn - hn_hi.astype(F32)).astype(BF16)
    logits = (jnp.dot(hn_hi, wr_ref[...], preferred_element_type=F32)
              + jnp.dot(hn_lo, wr_ref[...], preferred_element_type=F32)
              + jnp.dot(hn_hi, wrl_ref[...], preferred_element_type=F32)) + br_ref[...]
    ne = logits.shape[1]
    lane = lax.broadcasted_iota(jnp.int32, logits.shape, 1)
    vals, ids = [], []
    for _ in range(TOP_K):
        mx = jnp.max(logits, axis=1, keepdims=True)
        ix = jnp.min(jnp.where(logits == mx, lane, ne), axis=1, keepdims=True)
        vals.append(mx)
        ids.append(ix)
        logits = jnp.where(lane == ix, -jnp.inf, logits)
    v = jnp.concatenate(vals, axis=1)
    e = jnp.exp(v - v[:, 0:1])
    idx_ref[...] = jnp.concatenate(ids, axis=1)
    wt_ref[...] = e / jnp.sum(e, axis=1, keepdims=True)

    tri = (lax.broadcasted_iota(jnp.int32, (tm, tm), 0)
           > lax.broadcasted_iota(jnp.int32, (tm, tm), 1)).astype(BF16)
    base = run_ref[...]
    ranks = []
    for k in range(TOP_K):
        oh = (lane == ids[k]).astype(F32)
        prefix = jnp.dot(tri, oh.astype(BF16), preferred_element_type=F32)
        ranks.append(jnp.sum(oh * (base + prefix), axis=1, keepdims=True))
        base = base + jnp.sum(oh, axis=0, keepdims=True)
    run_ref[...] = base
    rank_ref[...] = jnp.concatenate(ranks, axis=1).astype(jnp.int32)
    cnt_ref[...] = base.astype(jnp.int32)


def _router(h, gain, w_router, b_router, tm):
    m, d = h.shape
    ne = w_router.shape[1]
    tm = _pick(m, tm, SUBLANES)
    wr_hi = w_router.astype(BF16)
    wr_lo = (w_router.astype(F32) - wr_hi.astype(F32)).astype(BF16)
    return pl.pallas_call(
        _router_kernel,
        grid=(m // tm,),
        in_specs=[
            pl.BlockSpec((tm, d), lambda i: (i, 0)),
            pl.BlockSpec((1, d), lambda i: (0, 0)),
            pl.BlockSpec((d, ne), lambda i: (0, 0)),
            pl.BlockSpec((d, ne), lambda i: (0, 0)),
            pl.BlockSpec((1, ne), lambda i: (0, 0)),
        ],
        out_specs=[
            pl.BlockSpec((tm, d // 2), lambda i: (i, 0)),
            pl.BlockSpec((tm, TOP_K), lambda i: (i, 0)),
            pl.BlockSpec((tm, TOP_K), lambda i: (i, 0)),
            pl.BlockSpec((tm, TOP_K), lambda i: (i, 0)),
            pl.BlockSpec((1, ne), lambda i: (0, 0)),
        ],
        out_shape=[
            jax.ShapeDtypeStruct((m, d // 2), jnp.uint32),
            jax.ShapeDtypeStruct((m, TOP_K), jnp.int32),
            jax.ShapeDtypeStruct((m, TOP_K), F32),
            jax.ShapeDtypeStruct((m, TOP_K), jnp.int32),
            jax.ShapeDtypeStruct((1, ne), jnp.int32),
        ],
        scratch_shapes=[pltpu.VMEM((1, ne), F32)],
        compiler_params=_cparams(("arbitrary",)),
        name="router",
    )(h, gain.reshape(1, d).astype(F32), wr_hi, wr_lo, b_router.reshape(1, ne).astype(F32))


def _dispatch_kernel(dest_ref, zlo_ref, zhi_ref, tot_ref, hp_ref, xs_hbm, stage, zbuf, sem, zsem,
                     *, per_step, npad, zrows, sr):
    i = pl.program_id(0)
    n = pl.num_programs(0)
    ne = zlo_ref.shape[0]
    tt = per_step // TOP_K

    def zero_row(r):
        return pltpu.make_async_copy(zbuf.at[pl.ds(0, sr)], xs_hbm.at[pl.ds(r * sr, sr)],
                                     zsem.at[0])

    def zero_chunk(c):
        return pltpu.make_async_copy(zbuf, xs_hbm.at[pl.ds(c * (zrows * sr), zrows * sr)],
                                     zsem.at[0])

    def for_zero_fills(fn_row, fn_chunk):
        def per_expert(e, _):
            lax.fori_loop(zlo_ref[e], zhi_ref[e], lambda r, _: (fn_row(r), 0)[1], 0)
            return 0
        lax.fori_loop(0, ne, per_expert, 0)
        lax.fori_loop(tot_ref[0] // zrows, npad // zrows, lambda c, _: (fn_chunk(c), 0)[1], 0)

    @pl.when(i == 0)
    def _():
        zbuf[...] = jnp.zeros_like(zbuf)
        for_zero_fills(lambda r: zero_row(r).start(), lambda c: zero_chunk(c).start())

    slot = i % 2
    stage[slot] = hp_ref[...]

    def body(lt, _):
        src = stage.at[slot, pl.ds(lt * sr, sr)]
        for k in range(TOP_K):
            pltpu.make_async_copy(src, xs_hbm.at[pl.ds(dest_ref[i, lt * TOP_K + k] * sr, sr)],
                                  sem.at[slot]).start(priority=k % 2)
        return 0
    lax.fori_loop(0, tt, body, 0)

    def wait_step(s):
        for _ in range(TOP_K):
            pltpu.make_async_copy(stage.at[s], xs_hbm.at[pl.ds(0, tt * sr)], sem.at[s]).wait()

    @pl.when(i > 0)
    def _():
        wait_step((i + 1) % 2)

    @pl.when(i == n - 1)
    def _():
        wait_step(i % 2)
        for_zero_fills(lambda r: zero_row(r).wait(), lambda c: zero_chunk(c).wait())


def _dispatch(hp, dest2d, zero_lo, zero_hi, total, npad, sr):
    nsteps, per_step = dest2d.shape
    w = hp.shape[1]
    zrows = 64
    assert npad % zrows == 0
    return pl.pallas_call(
        functools.partial(_dispatch_kernel, per_step=per_step, npad=npad, zrows=zrows, sr=sr),
        grid_spec=pltpu.PrefetchScalarGridSpec(
            num_scalar_prefetch=4,
            grid=(nsteps,),
            in_specs=[pl.BlockSpec((per_step // TOP_K * sr, w), lambda i, *_: (i, 0))],
            out_specs=pl.BlockSpec(memory_space=pl.ANY),
            scratch_shapes=[pltpu.VMEM((2, per_step // TOP_K * sr, w), hp.dtype),
                            pltpu.VMEM((zrows * sr, w), hp.dtype),
                            pltpu.SemaphoreType.DMA((2,)), pltpu.SemaphoreType.DMA((1,))],
        ),
        out_shape=jax.ShapeDtypeStruct((npad * sr, w), hp.dtype),
        compiler_params=_cparams(("arbitrary",)),
        name="dispatch",
    )(dest2d, zero_lo, zero_hi, total, hp)


def _moe_kernel(be_ref, na_ref, x_ref, wg_ref, wu_ref, bg_ref, bu_ref, wd_ref, bd_ref,
                o_ref, xb_ref, act_ref, *, nc1):
    i = pl.program_id(0)
    c = pl.program_id(1)
    active = i < na_ref[0]
    tn = act_ref.shape[2]

    @pl.when(jnp.logical_and(active, c == 0))
    def _():
        half = x_ref.shape[1]
        lo, hi = _unpack_bf16_pairs(x_ref[...])
        xb_ref[:, :half] = lo.astype(BF16)
        xb_ref[:, half:] = hi.astype(BF16)

    @pl.when(jnp.logical_and(active, c < nc1))
    def _():
        xb = xb_ref[...]
        g = jnp.dot(xb, wg_ref[0], preferred_element_type=F32) + bg_ref[0]
        u = jnp.dot(xb, wu_ref[0], preferred_element_type=F32) + bu_ref[0]
        gate = jnp.minimum(g, SWIGLU_LIMIT)
        up = jnp.clip(u, -SWIGLU_LIMIT, SWIGLU_LIMIT)
        act_ref[c] = (gate * jax.nn.sigmoid(SWIGLU_ALPHA * gate) * (up + 1.0)).astype(BF16)

    @pl.when(jnp.logical_and(active, c >= nc1))
    def _():
        y = bd_ref[0] + jnp.dot(act_ref[0], wd_ref[0, 0:tn, :], preferred_element_type=F32)
        for cc in range(1, nc1):
            y += jnp.dot(act_ref[cc], wd_ref[0, cc * tn:(cc + 1) * tn, :],
                         preferred_element_type=F32)
        o_ref[...] = _pack_bf16_pairs(y)

    @pl.when(jnp.logical_and(jnp.logical_not(active), c >= nc1))
    def _():
        o_ref[...] = jnp.zeros_like(o_ref)


def _moe(xs, blk_expert, n_active, w_gu, b_gu, w_down, b_down, bm, tn, tcol):
    npad, w = xs.shape
    ne, de, d = w_down.shape
    nc1 = de // tn
    nc2 = d // tcol
    nblk = npad // bm

    def blk(i, na):
        return jnp.clip(i, 0, na[0] - 1)

    def x_map(i, c, be, na):
        return (jnp.where(c >= nc1, blk(i + 1, na), blk(i, na)), 0)

    def gu_map(first_next, col0):
        def index(i, c, be, na):
            nxt = c >= first_next
            e = be[jnp.where(nxt, blk(i + 1, na), blk(i, na))]
            return (e, 0, col0 + jnp.where(nxt, 0, jnp.minimum(c, nc1 - 1)))
        return index

    def down_map(i, c, be, na):
        cur = c >= nc1
        e = be[jnp.where(cur, blk(i, na), blk(i - 1, na))]
        return (e, 0, jnp.where(cur, c - nc1, nc2 - 1))

    g_next = nc1 + nc2 - 1
    u_next = nc1 + nc2
    in_specs = [
        pl.BlockSpec((bm, w), x_map),
        pl.BlockSpec((1, d, tn), gu_map(g_next, 0)),
        pl.BlockSpec((1, d, tn), gu_map(u_next, nc1)),
        pl.BlockSpec((1, 1, tn), gu_map(g_next, 0)),
        pl.BlockSpec((1, 1, tn), gu_map(u_next, nc1)),
        pl.BlockSpec((1, de, tcol), down_map),
        pl.BlockSpec((1, 1, tcol), down_map),
    ]
    return pl.pallas_call(
        functools.partial(_moe_kernel, nc1=nc1),
        grid_spec=pltpu.PrefetchScalarGridSpec(
            num_scalar_prefetch=2,
            grid=(nblk, nc1 + nc2),
            in_specs=in_specs,
            out_specs=pl.BlockSpec((bm, tcol // 2),
                                   lambda i, c, be, na: (i, jnp.maximum(c - nc1, 0))),
            scratch_shapes=[pltpu.VMEM((bm, d), BF16), pltpu.VMEM((nc1, bm, tn), BF16)],
        ),
        out_shape=jax.ShapeDtypeStruct((npad, d // 2), jnp.uint32),
        compiler_params=_cparams(("arbitrary", "arbitrary")),
        name="moe",
    )(blk_expert, n_active, xs, w_gu, w_gu, b_gu, b_gu, w_down, b_down)


def _combine_kernel(idx_ref, y_hbm, h_ref, wt_ref, g_ref, o_ref, buf, sem, *, rows, tt, hw):
    i = pl.program_id(0)
    n = pl.num_programs(0)

    def issue(step, slot):
        def body(r4, _):
            for j in range(4):
                r = r4 * 4 + j
                src = idx_ref[step, r]
                pltpu.make_async_copy(y_hbm.at[pl.ds(src, 1)], buf.at[slot, pl.ds(r, 1)],
                                      sem.at[slot]).start(priority=j % 2)
            return 0
        lax.fori_loop(0, rows // 4, body, 0)

    @pl.when(i == 0)
    def _():
        issue(0, 0)

    @pl.when(i + 1 < n)
    def _():
        issue(i + 1, (i + 1) % 2)

    slot = i % 2
    pltpu.make_async_copy(y_hbm.at[pl.ds(0, rows)], buf.at[slot], sem.at[slot]).wait()
    wts = wt_ref[...]
    lo = hi = None
    for k in range(TOP_K):
        tl, th = _unpack_bf16_pairs(buf[slot, k * tt:(k + 1) * tt, :])
        wk = wts[:, k:k + 1]
        lo = wk * tl if lo is None else lo + wk * tl
        hi = wk * th if hi is None else hi + wk * th
    pieces = []
    for w0 in range(0, lo.shape[1], hw):
        pieces += [lo[:, w0:w0 + hw], hi[:, w0:w0 + hw]]
    acc = h_ref[...] + jnp.concatenate(pieces, axis=1)
    ms = jnp.mean(acc * acc, axis=-1, keepdims=True)
    o_ref[...] = acc * lax.rsqrt(ms + EPS) * g_ref[...]


def _combine(ys, idx2d, h, top_w, gain, tt, tcol):
    nsteps, rows = idx2d.shape
    d = h.shape[1]
    w = ys.shape[1]
    return pl.pallas_call(
        functools.partial(_combine_kernel, rows=rows, tt=tt, hw=tcol // 2),
        grid_spec=pltpu.PrefetchScalarGridSpec(
            num_scalar_prefetch=1,
            grid=(nsteps,),
            in_specs=[
                pl.BlockSpec(memory_space=pl.ANY),
                pl.BlockSpec((tt, d), lambda i, idx: (i, 0)),
                pl.BlockSpec((tt, TOP_K), lambda i, idx: (i, 0)),
                pl.BlockSpec((1, d), lambda i, idx: (0, 0)),
            ],
            out_specs=pl.BlockSpec((tt, d), lambda i, idx: (i, 0)),
            scratch_shapes=[pltpu.VMEM((2, rows, w), jnp.uint32),
                            pltpu.SemaphoreType.DMA((2,))],
        ),
        out_shape=jax.ShapeDtypeStruct(h.shape, F32),
        compiler_params=_cparams(("arbitrary",)),
        name="combine",
    )(idx2d, ys, h, top_w, gain.reshape(1, d).astype(F32))


def _dispatch_plan(top_idx, rank, counts, bm, nblk):
    ne = counts.shape[0]
    padded = ((counts + bm - 1) // bm) * bm
    pad_end = jnp.cumsum(padded)
    pad_start = pad_end - padded
    onehot = top_idx[..., None] == jnp.arange(ne, dtype=jnp.int32)
    dest = rank + jnp.sum(jnp.where(onehot, pad_start, 0), axis=-1)
    blk_row0 = jnp.arange(nblk, dtype=jnp.int32) * bm
    blk_expert = jnp.minimum(
        jnp.sum((pad_end[None, :] <= blk_row0[:, None]).astype(jnp.int32), axis=1), ne - 1)
    total = pad_end[-1:].astype(jnp.int32)
    return (dest.astype(jnp.int32), blk_expert, total // bm,
            (pad_start + counts).astype(jnp.int32), pad_end.astype(jnp.int32), total)


def _rope_tables(n_pos):
    pos = jnp.arange(n_pos, dtype=F32)
    inv_freq = ROPE_THETA ** (-jnp.arange(0, QK_ROPE, 2, dtype=F32) / QK_ROPE)
    ang = pos[:, None] * inv_freq[None, :]
    cos, sin = jnp.cos(ang), jnp.sin(ang)
    zero = jnp.zeros((n_pos, LANES - QK_ROPE), F32)
    return jnp.concatenate([cos, cos, zero], axis=1), jnp.concatenate([-sin, sin, zero], axis=1)


def _with_swapped_rope(w_rope):
    half = QK_ROPE // 2
    return jnp.concatenate([w_rope, w_rope[..., half:], w_rope[..., :half]], axis=-1)


def kernel(x, meta_tokens, g_mix, w_in, b_gate, g_q_lat, g_kv_lat, w_uq, w_ukv, s5_a_re, s5_a_im, s5_log_dt, s5_b_re, s5_b_im, s5_c_re, s5_c_im, s5_d, w_glu_a, w_glu_b, w_out, g_ffn, w_router, b_router, w_gu, b_gu, w_down, b_down, g_final):
    assert w_in.shape[0] == 1, "single-layer block"
    nb, seq, d = x.shape
    t = nb * seq
    n_meta = meta_tokens.shape[0]
    ql, kvl = g_q_lat.shape[-1], g_kv_lat.shape[-1]
    sw = s5_d.shape[-1]
    nh = w_uq.shape[-1] // QK_DIM
    ne = w_router.shape[-1]
    assert ql % kvl == 0 and (ql + kvl) % LANES == 0 and sw % LANES == 0 and d % LANES == 0
    assert n_meta % (2 * SUBLANES) == 0 and n_meta <= LANES

    w0 = w_in[0]
    c_kv = ql + kvl
    c_kr = c_kv + QK_ROPE
    w_main = jnp.concatenate([w0[:, :c_kv], w0[:, c_kr:]], axis=1).astype(BF16)
    w_kr = _with_swapped_rope(w0[:, c_kv:c_kr]).astype(BF16)
    o_s5 = c_kv
    o_gs = o_s5 + sw
    o_gm = o_gs + d
    wq3 = w_uq[0].reshape(ql, nh, QK_DIM)
    w_q = jnp.concatenate([wq3[..., :QK_NOPE], _with_swapped_rope(wq3[..., QK_NOPE:])],
                          axis=-1).reshape(ql, nh * HEAD_PAD).astype(BF16)
    w_kv = w_ukv[0].astype(BF16)
    cos_t, sin_t = _rope_tables(n_meta + seq)
    rope_meta = (cos_t[:n_meta], sin_t[:n_meta])
    rope_real = (cos_t[n_meta:], sin_t[n_meta:])
    s5_tabs = _s5_tables(s5_a_re[0], s5_a_im[0], s5_log_dt[0], s5_b_re[0], s5_b_im[0],
                         s5_c_re[0], s5_c_im[0], s5_d[0])
    bw, cw, a_tab, g_tab_fn, d_skip = s5_tabs

    x2 = x.reshape(t, d)
    tm = 512

    proj_m = _norm_mm(meta_tokens, 0, g_mix[0], w_main, BF16, n_meta, 512, name="proj_meta")
    kr_m = _norm_mm(meta_tokens, 0, g_mix[0], w_kr, BF16, n_meta, LANES,
                    rope=(*rope_meta, LANES), name="krope_meta")
    kv_m = _norm_mm(proj_m, ql // kvl, g_kv_lat[0], w_kv, BF16, n_meta, 1024, name="kv_meta")
    kvm = jnp.pad(kv_m, ((0, LANES - n_meta), (0, 0)))
    krm = jnp.pad(kr_m, ((0, LANES - n_meta), (0, 0)))
    ntile = sw // LANES
    ns = bw.shape[2] // 2
    _, c_meta = _s5_scan(proj_m, o_s5 // LANES, 1, n_meta, n_meta,
                         (bw, cw, a_tab, g_tab_fn(n_meta // SUBLANES), d_skip),
                         jnp.zeros((ntile, 2, ns), F32), "s5_meta")

    proj = _norm_mm(x2, 0, g_mix[0], w_main, BF16, tm, 1024, name="proj")
    kr = _norm_mm(x2, 0, g_mix[0], w_kr, BF16, tm, LANES, rope=(*rope_real, LANES), name="krope")
    q = _norm_mm(proj, 0, g_q_lat[0], w_q, BF16, 1024, 2048, rope=(*rope_real, HEAD_PAD),
                 out_scale=math.log2(math.e) * QK_DIM ** -0.5, name="q")
    kv = _norm_mm(proj, ql // kvl, g_kv_lat[0], w_kv, BF16, 1024, 2048, name="kv")

    tc = _pick(seq, 512, 2 * SUBLANES)
    z, _ = _s5_scan(proj, o_s5 // LANES, nb, seq, tc,
                    (bw, cw, a_tab, g_tab_fn(tc // SUBLANES), d_skip), c_meta[0], "s5")
    bias = b_gate[0].astype(F32).reshape(1, 2 * d)
    ys = _glu(z, w_glu_a[0].astype(BF16), w_glu_b[0].astype(BF16), proj, o_gs, bias[:, :d], 1024, 1024)
    ym = _attention(q, kv, kr, kvm, krm, proj, o_gm, bias[:, d:], nb, seq, nh, n_meta, 2048, 512)
    h = _outproj(ys, ym, w_out[0].astype(BF16), x2, tm, 1024)

    hp, top_idx, top_w, rank, counts = _router(h, g_ffn[0], w_router[0], b_router[0], 256)
    bm = 512 if t * TOP_K >= 512 * ne else 128
    nblk = (t * TOP_K + ne * (bm - 1) + bm - 1) // bm
    dest, blk_expert, n_active, zero_lo, zero_hi, total = _dispatch_plan(
        top_idx, rank, counts[0], bm, nblk)
    per_step = 512
    xs = _dispatch(hp, dest.reshape(t * TOP_K // per_step, per_step), zero_lo, zero_hi, total,
                   nblk * bm, 1)
    de = w_down.shape[2]
    tcol = _pick(d, 1024)
    ysort = _moe(xs, blk_expert, n_active, w_gu[0].astype(BF16), b_gu[0].reshape(ne, 1, 2 * de),
                 w_down[0].astype(BF16), b_down[0].reshape(ne, 1, d), bm, _pick(de, 512), tcol)
    tt = 64
    idx_c = dest.reshape(t // tt, tt, TOP_K).transpose(0, 2, 1).reshape(t // tt, TOP_K * tt)
    out = _combine(ysort, idx_c, h, top_w, g_final, tt, tcol)
    return out.reshape(nb, seq, d)
```

```python
import functools
import math

import jax
import jax.numpy as jnp
from jax import lax
from jax.experimental import pallas as pl
from jax.experimental.pallas import tpu as pltpu

F32 = jnp.float32
BF16 = jnp.bfloat16

EPS = 1e-6
QK_NOPE = 128
QK_ROPE = 64
QK_DIM = QK_NOPE + QK_ROPE
V_HEAD = 128
ROPE_THETA = 10000.0
NEG_INF = -1e30
S5_GROUP = 16
TOP_K = 4
SWIGLU_ALPHA = 1.702
SWIGLU_LIMIT = 7.0

LANES = 128
SUBLANES = 8
HEAD_PAD = 2 * LANES
S5_TILE_GROUPS = LANES // S5_GROUP
VMEM_LIMIT_V7X = 56 * 1024 * 1024


def _cparams(sem, vmem=VMEM_LIMIT_V7X):
    return pltpu.CompilerParams(dimension_semantics=sem, vmem_limit_bytes=vmem)


def _pack_bf16_pairs(x):
    half = x.shape[1] // 2
    bits = pltpu.bitcast(x.astype(BF16).astype(F32), jnp.uint32)
    return (bits[:, :half] >> 16) | (bits[:, half:] & jnp.uint32(0xFFFF0000))


def _unpack_bf16_pairs(word):
    return (pltpu.bitcast(word << 16, F32), pltpu.bitcast(word & jnp.uint32(0xFFFF0000), F32))


def _pick(n, pref, mult=LANES):
    if n <= pref:
        return n
    best = None
    for t in range(mult, pref + 1, mult):
        if n % t == 0:
            best = t
    assert best is not None, (n, pref, mult)
    return best


def _norm_mm_kernel(a_ref, g_ref, w_ref, *rest, rope_group, out_scale):
    if rope_group:
        cos_ref, sin_ref, o_ref, an_ref = rest
    else:
        o_ref, an_ref = rest

    @pl.when(pl.program_id(1) == 0)
    def _():
        xf = a_ref[...].astype(F32)
        ms = jnp.mean(xf * xf, axis=-1, keepdims=True)
        an_ref[...] = (xf * lax.rsqrt(ms + EPS) * g_ref[...]).astype(BF16)

    acc = jnp.dot(an_ref[...], w_ref[...], preferred_element_type=F32)
    if out_scale is not None:
        acc = acc * out_scale
    if not rope_group:
        o_ref[...] = acc.astype(o_ref.dtype)
        return
    c = cos_ref[...]
    s = sin_ref[...]
    for g0 in range(0, acc.shape[1], rope_group):
        lo = g0 + rope_group - LANES
        if lo > g0:
            o_ref[:, g0:lo] = acc[:, g0:lo].astype(o_ref.dtype)
        slab = acc[:, lo:lo + LANES]
        o_ref[:, lo:lo + LANES] = (slab * c + pltpu.roll(slab, QK_ROPE, 1) * s).astype(o_ref.dtype)


def _norm_mm(a, col_blk, gain, w, out_dtype, tm, tn, rope=None, out_scale=None, name=None):
    m = a.shape[0]
    k, n = w.shape
    tm = _pick(m if rope is None else math.gcd(m, rope[0].shape[0]), tm, SUBLANES)
    tn = _pick(n, tn)
    in_specs = [
        pl.BlockSpec((tm, k), lambda i, j: (i, col_blk)),
        pl.BlockSpec((1, k), lambda i, j: (0, 0)),
        pl.BlockSpec((k, tn), lambda i, j: (0, j)),
    ]
    args = [a, gain.reshape(1, k).astype(F32), w]
    group = 0
    if rope is not None:
        cos_t, sin_t, group = rope
        nrb = cos_t.shape[0] // tm
        in_specs += [pl.BlockSpec((tm, LANES), lambda i, j: (i % nrb, 0))] * 2
        args += [cos_t, sin_t]
    return pl.pallas_call(
        functools.partial(_norm_mm_kernel, rope_group=group, out_scale=out_scale),
        grid=(m // tm, n // tn),
        in_specs=in_specs,
        out_specs=pl.BlockSpec((tm, tn), lambda i, j: (i, j)),
        out_shape=jax.ShapeDtypeStruct((m, n), out_dtype),
        scratch_shapes=[pltpu.VMEM((tm, k), BF16)],
        compiler_params=_cparams(("parallel", "arbitrary")),
        name=name,
    )(*args)


def _s5_kernel(u_ref, perm_ref, permt_ref, bw_ref, cw_ref, a_ref, g_ref, d_ref, c0_ref,
               z_ref, cout_ref, bu_ref, xb_ref, carry_ref, *, tc, ns):
    t = pl.program_id(2)
    tseg = tc // SUBLANES

    @pl.when(t == 0)
    def _():
        carry_ref[...] = c0_ref[0]

    u_p = jnp.dot(perm_ref[...], u_ref[...], preferred_element_type=F32).astype(BF16)
    bu_ref[...] = jnp.dot(u_p, bw_ref[0], preferred_element_type=F32)

    ar = jnp.broadcast_to(a_ref[0, 0:1, :], (SUBLANES, ns))
    ai = jnp.broadcast_to(a_ref[0, 1:2, :], (SUBLANES, ns))

    def local_scan(i, x):
        xr, xi = x
        r0 = pl.multiple_of(i * SUBLANES, SUBLANES)
        nr = ar * xr - ai * xi + bu_ref[pl.ds(r0, SUBLANES), 0:ns]
        ni = ar * xi + ai * xr + bu_ref[pl.ds(r0, SUBLANES), ns:2 * ns]
        bu_ref[pl.ds(r0, SUBLANES), 0:ns] = nr
        bu_ref[pl.ds(r0, SUBLANES), ns:2 * ns] = ni
        return nr, ni

    zero = jnp.zeros((SUBLANES, ns), F32)
    er, ei = lax.fori_loop(0, tseg, local_scan, (zero, zero), unroll=2 if tseg % 4 == 0 else 1)

    c_r = carry_ref[0:1, :]
    c_i = carry_ref[1:2, :]
    sub = lax.broadcasted_iota(jnp.int32, (SUBLANES, ns), 0)
    g1r = g_ref[0, 0:1, :]
    g1i = g_ref[0, 1:2, :]
    fr = er + jnp.where(sub == 0, g1r * c_r - g1i * c_i, 0.0)
    fi = ei + jnp.where(sub == 0, g1r * c_i + g1i * c_r, 0.0)
    for k, d in enumerate((1, 2, 4)):
        gr = g_ref[0, 2 * k:2 * k + 1, :]
        gi = g_ref[0, 2 * k + 1:2 * k + 2, :]
        sr = jnp.where(sub >= d, pltpu.roll(fr, d, 0), 0.0)
        si = jnp.where(sub >= d, pltpu.roll(fi, d, 0), 0.0)
        fr, fi = fr + gr * sr - gi * si, fi + gr * si + gi * sr
    cin_r = jnp.where(sub == 0, c_r, pltpu.roll(fr, 1, 0))
    cin_i = jnp.where(sub == 0, c_i, pltpu.roll(fi, 1, 0))
    carry_ref[0:1, :] = fr[SUBLANES - 1:SUBLANES, :]
    carry_ref[1:2, :] = fi[SUBLANES - 1:SUBLANES, :]

    def add_carry(j, c):
        cr, ci = c
        c1r, c1i = ar * cr - ai * ci, ar * ci + ai * cr
        c2r, c2i = ar * c1r - ai * c1i, ar * c1i + ai * c1r
        r0 = pl.multiple_of(j * 2 * SUBLANES, 2 * SUBLANES)
        r1 = r0 + SUBLANES
        xr = jnp.concatenate([bu_ref[pl.ds(r0, SUBLANES), 0:ns] + c1r,
                              bu_ref[pl.ds(r1, SUBLANES), 0:ns] + c2r], axis=0)
        xi = jnp.concatenate([bu_ref[pl.ds(r0, SUBLANES), ns:2 * ns] + c1i,
                              bu_ref[pl.ds(r1, SUBLANES), ns:2 * ns] + c2i], axis=0)
        xb_ref[pl.ds(r0, 2 * SUBLANES), 0:ns] = xr.astype(BF16)
        xb_ref[pl.ds(r0, 2 * SUBLANES), ns:2 * ns] = xi.astype(BF16)
        return c2r, c2i

    lax.fori_loop(0, tseg // 2, add_carry, (cin_r, cin_i), unroll=2 if tseg % 4 == 0 else 1)

    y = jnp.dot(xb_ref[...], cw_ref[0], preferred_element_type=F32) + d_ref[...] * u_p.astype(F32)
    zp = jax.nn.gelu(y).astype(BF16)
    z_ref[...] = jnp.dot(permt_ref[...], zp, preferred_element_type=F32).astype(BF16)

    @pl.when(t == pl.num_programs(2) - 1)
    def _():
        cout_ref[0, 0] = carry_ref[...]


def _s5_scan(u_arr, u_col0, nb, seq, tc, tabs, c0, name):
    bw, cw, a_tab, g_tab, d_skip = tabs
    ntile, _, ns2 = bw.shape
    ns = ns2 // 2
    nt = seq // tc
    tseg = tc // SUBLANES
    r = jnp.arange(tc, dtype=jnp.int32)
    src = (r % SUBLANES) * tseg + r // SUBLANES
    perm = (src[:, None] == jnp.arange(tc, dtype=jnp.int32)[None, :]).astype(BF16)
    kern = functools.partial(_s5_kernel, tc=tc, ns=ns)
    return pl.pallas_call(
        kern,
        grid=(nb, ntile, nt),
        in_specs=[
            pl.BlockSpec((tc, LANES), lambda b, n, t: (b * nt + t, u_col0 + n)),
            pl.BlockSpec((tc, tc), lambda b, n, t: (0, 0)),
            pl.BlockSpec((tc, tc), lambda b, n, t: (0, 0)),
            pl.BlockSpec((1, LANES, 2 * ns), lambda b, n, t: (n, 0, 0)),
            pl.BlockSpec((1, 2 * ns, LANES), lambda b, n, t: (n, 0, 0)),
            pl.BlockSpec((1, 2, ns), lambda b, n, t: (n, 0, 0)),
            pl.BlockSpec((1, 6, ns), lambda b, n, t: (n, 0, 0)),
            pl.BlockSpec((1, LANES), lambda b, n, t: (0, n)),
            pl.BlockSpec((1, 2, ns), lambda b, n, t: (n, 0, 0)),
        ],
        out_specs=[
            pl.BlockSpec((tc, LANES), lambda b, n, t: (b * nt + t, n)),
            pl.BlockSpec((1, 1, 2, ns), lambda b, n, t: (b, n, 0, 0)),
        ],
        out_shape=[
            jax.ShapeDtypeStruct((nb * seq, ntile * LANES), BF16),
            jax.ShapeDtypeStruct((nb, ntile, 2, ns), F32),
        ],
        scratch_shapes=[
            pltpu.VMEM((tc, 2 * ns), F32),
            pltpu.VMEM((tc, 2 * ns), BF16),
            pltpu.VMEM((2, ns), F32),
        ],
        compiler_params=_cparams(("parallel", "parallel", "arbitrary")),
        name=name,
    )(u_arr, perm, perm.T, bw, cw, a_tab, g_tab, d_skip, c0)


def _s5_tables(a_re, a_im, log_dt, b_re, b_im, c_re, c_im, d_skip):
    ngrp, nst = a_re.shape
    ntile = ngrp // S5_TILE_GROUPS
    ns = S5_TILE_GROUPS * nst
    dt = jnp.exp(log_dt.astype(F32))[:, None]
    lr, li = a_re.astype(F32), a_im.astype(F32)

    def a_pow(k):
        mag = jnp.exp(lr * dt * k)
        return mag * jnp.cos(li * dt * k), mag * jnp.sin(li * dt * k)

    ab_re, ab_im = a_pow(1)
    den = lr * lr + li * li
    f_re = ((ab_re - 1.0) * lr + ab_im * li) / den
    f_im = (ab_im * lr - (ab_re - 1.0) * li) / den
    br, bi = b_re.astype(F32), b_im.astype(F32)
    bb_re = f_re[..., None] * br - f_im[..., None] * bi
    bb_im = f_re[..., None] * bi + f_im[..., None] * br
    eye = jnp.eye(S5_TILE_GROUPS, dtype=F32)

    def b_tile(bb):
        bb = bb.reshape(ntile, S5_TILE_GROUPS, nst, S5_GROUP)
        return jnp.einsum("ab,napc->nacbp", eye, bb).reshape(ntile, LANES, ns)

    def c_tile(cc):
        cc = cc.astype(F32).reshape(ntile, S5_TILE_GROUPS, S5_GROUP, nst)
        return jnp.einsum("ab,nacp->nbpac", eye, cc).reshape(ntile, ns, LANES)

    bw = jnp.concatenate([b_tile(bb_re), b_tile(bb_im)], axis=2).astype(BF16)
    cw = jnp.concatenate([c_tile(c_re), -c_tile(c_im)], axis=1).astype(BF16)
    a_tab = jnp.stack([ab_re.reshape(ntile, ns), ab_im.reshape(ntile, ns)], axis=1)

    def g_tab(tseg):
        rows = []
        for d in (1, 2, 4):
            pr, pi = a_pow(tseg * d)
            rows += [pr.reshape(ntile, ns), pi.reshape(ntile, ns)]
        return jnp.stack(rows, axis=1)

    return bw, cw, a_tab, g_tab, d_skip.astype(F32).reshape(1, -1)


def _glu_kernel(z_ref, wa_ref, wb_ref, gs_ref, bg_ref, o_ref):
    z = z_ref[...]
    a = jnp.dot(z, wa_ref[...], preferred_element_type=F32)
    b = jnp.dot(z, wb_ref[...], preferred_element_type=F32)
    gate = jax.nn.sigmoid(gs_ref[...].astype(F32) + bg_ref[...])
    o_ref[...] = (a * jax.nn.sigmoid(b) * gate).astype(o_ref.dtype)


def _glu(z, wa, wb, proj, gs_col0, bias, tm, tn):
    m, k = z.shape
    n = wa.shape[1]
    tm = _pick(m, tm, SUBLANES)
    tn = _pick(math.gcd(n, gs_col0) if gs_col0 else n, tn)
    goff = gs_col0 // tn
    return pl.pallas_call(
        _glu_kernel,
        grid=(m // tm, n // tn),
        in_specs=[
            pl.BlockSpec((tm, k), lambda i, j: (i, 0)),
            pl.BlockSpec((k, tn), lambda i, j: (0, j)),
            pl.BlockSpec((k, tn), lambda i, j: (0, j)),
            pl.BlockSpec((tm, tn), lambda i, j: (i, goff + j)),
            pl.BlockSpec((1, tn), lambda i, j: (0, j)),
        ],
        out_specs=pl.BlockSpec((tm, tn), lambda i, j: (i, j)),
        out_shape=jax.ShapeDtypeStruct((m, n), BF16),
        compiler_params=_cparams(("parallel", "arbitrary")),
        name="glu",
    )(z, wa, wb, proj, bias)


def _attn_kernel(q_ref, kn_ref, v_ref, kr_ref, km_ref, vm_ref, krm_ref, gm_ref, bg_ref, o_ref,
                 *, tq, tk, n_meta):
    qi = pl.program_id(2)

    def update(q, carry, j, mask_row0):
        m, l, acc = carry
        r0 = pl.multiple_of(j * tk, tk)
        k = jnp.concatenate([kn_ref[pl.ds(r0, tk), :], kr_ref[pl.ds(r0, tk), :]], axis=1)
        s = lax.dot_general(q, k, (((1,), (1,)), ((), ())), preferred_element_type=F32)
        if mask_row0 is not None:
            row = mask_row0 + lax.broadcasted_iota(jnp.int32, s.shape, 0)
            colp = j * tk + lax.broadcasted_iota(jnp.int32, s.shape, 1)
            s = jnp.where(colp <= row, s, NEG_INF)
        m_new = jnp.maximum(m, jnp.max(s, axis=1, keepdims=True))
        alpha = jnp.exp2(m - m_new)
        p = jnp.exp2(s - m_new)
        l = alpha * l + jnp.sum(p, axis=1, keepdims=True)
        acc = alpha * acc + jnp.dot(p.astype(BF16), v_ref[pl.ds(r0, tk), :],
                                    preferred_element_type=F32)
        return m_new, l, acc

    q = q_ref[...]
    km = jnp.concatenate([km_ref[...], krm_ref[...]], axis=1)
    s = lax.dot_general(q, km, (((1,), (1,)), ((), ())), preferred_element_type=F32)
    col = lax.broadcasted_iota(jnp.int32, s.shape, 1)
    s = jnp.where(col < n_meta, s, NEG_INF)
    m = jnp.max(s, axis=1, keepdims=True)
    p = jnp.exp2(s - m)
    l = jnp.sum(p, axis=1, keepdims=True)
    acc = jnp.dot(p.astype(BF16), vm_ref[...], preferred_element_type=F32)

    n_full = (qi * tq) // tk
    m, l, acc = lax.fori_loop(0, n_full, lambda j, c: update(q, c, j, None), (m, l, acc))

    for r in range(tq // tk):
        rows = slice(r * tk, (r + 1) * tk)
        qr = q_ref[rows, :]
        carry = (m[rows], l[rows], acc[rows])
        for dd in range(r):
            carry = update(qr, carry, n_full + dd, None)
        _, lr, ar = update(qr, carry, n_full + r, qi * tq + r * tk)
        gate = jax.nn.sigmoid(gm_ref[rows, :].astype(F32) + bg_ref[...])
        o_ref[rows, :] = (ar / lr * gate).astype(o_ref.dtype)


def _attention(q, kv, kr, kvm, krm, proj, gm_col0, bias, nb, seq, nh, n_meta, tq, tk):
    tq = _pick(seq, tq, SUBLANES)
    tk = _pick(tq, tk, SUBLANES)
    nq = seq // tq
    mrows = kvm.shape[0]
    goff = gm_col0 // LANES
    kern = functools.partial(_attn_kernel, tq=tq, tk=tk, n_meta=n_meta)
    return pl.pallas_call(
        kern,
        grid=(nb, nh, nq),
        in_specs=[
            pl.BlockSpec((tq, HEAD_PAD), lambda b, h, i: (b * nq + i, h)),
            pl.BlockSpec((seq, LANES), lambda b, h, i: (b, 2 * h)),
            pl.BlockSpec((seq, LANES), lambda b, h, i: (b, 2 * h + 1)),
            pl.BlockSpec((seq, LANES), lambda b, h, i: (b, 0)),
            pl.BlockSpec((mrows, LANES), lambda b, h, i: (0, 2 * h)),
            pl.BlockSpec((mrows, LANES), lambda b, h, i: (0, 2 * h + 1)),
            pl.BlockSpec((mrows, LANES), lambda b, h, i: (0, 0)),
            pl.BlockSpec((tq, LANES), lambda b, h, i: (b * nq + i, goff + h)),
            pl.BlockSpec((1, LANES), lambda b, h, i: (0, h)),
        ],
        out_specs=pl.BlockSpec((tq, LANES), lambda b, h, i: (b * nq + i, h)),
        out_shape=jax.ShapeDtypeStruct((nb * seq, nh * V_HEAD), BF16),
        compiler_params=_cparams(("parallel", "parallel", "arbitrary")),
        name="attention",
    )(q, kv, kv, kr, kvm, kvm, krm, proj, bias)


def _outproj_kernel(ys_ref, ym_ref, w_ref, res_ref, o_ref, a_ref):
    @pl.when(pl.program_id(1) == 0)
    def _():
        a_ref[...] = (ys_ref[...].astype(F32) + ym_ref[...].astype(F32)).astype(BF16)

    o_ref[...] = jnp.dot(a_ref[...], w_ref[...], preferred_element_type=F32) + res_ref[...]


def _outproj(ys, ym, w, res, tm, tn):
    m, k = ys.shape
    n = w.shape[1]
    tm = _pick(m, tm, SUBLANES)
    tn = _pick(n, tn)
    return pl.pallas_call(
        _outproj_kernel,
        grid=(m // tm, n // tn),
        in_specs=[
            pl.BlockSpec((tm, k), lambda i, j: (i, 0)),
            pl.BlockSpec((tm, k), lambda i, j: (i, 0)),
            pl.BlockSpec((k, tn), lambda i, j: (0, j)),
            pl.BlockSpec((tm, tn), lambda i, j: (i, j)),
        ],
        out_specs=pl.BlockSpec((tm, tn), lambda i, j: (i, j)),
        out_shape=jax.ShapeDtypeStruct((m, n), F32),
        scratch_shapes=[pltpu.VMEM((tm, k), BF16)],
        compiler_params=_cparams(("parallel", "arbitrary")),
        name="outproj",
    )(ys, ym, w, res)


def _router_kernel(h_ref, g_ref, wr_ref, wrl_ref, br_ref, hp_ref, idx_ref, wt_ref, rank_ref,
                   cnt_ref, run_ref):
    @pl.when(pl.program_id(0) == 0)
    def _():
        run_ref[...] = jnp.zeros_like(run_ref)

    xf = h_ref[...]
    tm, d = xf.shape
    ms = jnp.mean(xf * xf, axis=-1, keepdims=True)
    hn = xf * lax.rsqrt(ms + EPS) * g_ref[...]

    hp_ref[...] = _pack_bf16_pairs(hn)

    hn_hi = hn.astype(BF16)
    hn_lo = (hn - hn_hi.astype(F32)).astype(BF16)
    logits = (jnp.dot(hn_hi, wr_ref[...], preferred_element_type=F32)
              + jnp.dot(hn_lo, wr_ref[...], preferred_element_type=F32)
              + jnp.dot(hn_hi, wrl_ref[...], preferred_element_type=F32)) + br_ref[...]
    ne = logits.shape[1]
    lane = lax.broadcasted_iota(jnp.int32, logits.shape, 1)
    vals, ids = [], []
    for _ in range(TOP_K):
        mx = jnp.max(logits, axis=1, keepdims=True)
        ix = jnp.min(jnp.where(logits == mx, lane, ne), axis=1, keepdims=True)
        vals.append(mx)
        ids.append(ix)
        logits = jnp.where(lane == ix, -jnp.inf, logits)
    v = jnp.concatenate(vals, axis=1)
    e = jnp.exp(v - v[:, 0:1])
    idx_ref[...] = jnp.concatenate(ids, axis=1)
    wt_ref[...] = e / jnp.sum(e, axis=1, keepdims=True)

    tri = (lax.broadcasted_iota(jnp.int32, (tm, tm), 0)
           > lax.broadcasted_iota(jnp.int32, (tm, tm), 1)).astype(BF16)
    base = run_ref[...]
    ranks = []
    for k in range(TOP_K):
        oh = (lane == ids[k]).astype(F32)
        prefix = jnp.dot(tri, oh.astype(BF16), preferred_element_type=F32)
        ranks.append(jnp.sum(oh * (base + prefix), axis=1, keepdims=True))
        base = base + jnp.sum(oh, axis=0, keepdims=True)
    run_ref[...] = base
    rank_ref[...] = jnp.concatenate(ranks, axis=1).astype(jnp.int32)
    cnt_ref[...] = base.astype(jnp.int32)


def _router(h, gain, w_router, b_router, tm):
    m, d = h.shape
    ne = w_router.shape[1]
    tm = _pick(m, tm, SUBLANES)
    wr_hi = w_router.astype(BF16)
    wr_lo = (w_router.astype(F32) - wr_hi.astype(F32)).astype(BF16)
    return pl.pallas_call(
        _router_kernel,
        grid=(m // tm,),
        in_specs=[
            pl.BlockSpec((tm, d), lambda i: (i, 0)),
            pl.BlockSpec((1, d), lambda i: (0, 0)),
            pl.BlockSpec((d, ne), lambda i: (0, 0)),
            pl.BlockSpec((d, ne), lambda i: (0, 0)),
            pl.BlockSpec((1, ne), lambda i: (0, 0)),
        ],
        out_specs=[
            pl.BlockSpec((tm, d // 2), lambda i: (i, 0)),
            pl.BlockSpec((tm, TOP_K), lambda i: (i, 0)),
            pl.BlockSpec((tm, TOP_K), lambda i: (i, 0)),
            pl.BlockSpec((tm, TOP_K), lambda i: (i, 0)),
            pl.BlockSpec((1, ne), lambda i: (0, 0)),
        ],
        out_shape=[
            jax.ShapeDtypeStruct((m, d // 2), jnp.uint32),
            jax.ShapeDtypeStruct((m, TOP_K), jnp.int32),
            jax.ShapeDtypeStruct((m, TOP_K), F32),
            jax.ShapeDtypeStruct((m, TOP_K), jnp.int32),
            jax.ShapeDtypeStruct((1, ne), jnp.int32),
        ],
        scratch_shapes=[pltpu.VMEM((1, ne), F32)],
        compiler_params=_cparams(("arbitrary",)),
        name="router",
    )(h, gain.reshape(1, d).astype(F32), wr_hi, wr_lo, b_router.reshape(1, ne).astype(F32))


def _dispatch_kernel(dest_ref, zlo_ref, zhi_ref, tot_ref, hp_ref, xs_hbm, stage, zbuf, sem, zsem,
                     *, per_step, npad, zrows, sr):
    i = pl.program_id(0)
    n = pl.num_programs(0)
    ne = zlo_ref.shape[0]
    tt = per_step // TOP_K

    def zero_row(r):
        return pltpu.make_async_copy(zbuf.at[pl.ds(0, sr)], xs_hbm.at[pl.ds(r * sr, sr)],
                                     zsem.at[0])

    def zero_chunk(c):
        return pltpu.make_async_copy(zbuf, xs_hbm.at[pl.ds(c * (zrows * sr), zrows * sr)],
                                     zsem.at[0])

    def for_zero_fills(fn_row, fn_chunk):
        def per_expert(e, _):
            lax.fori_loop(zlo_ref[e], zhi_ref[e], lambda r, _: (fn_row(r), 0)[1], 0)
            return 0
        lax.fori_loop(0, ne, per_expert, 0)
        lax.fori_loop(tot_ref[0] // zrows, npad // zrows, lambda c, _: (fn_chunk(c), 0)[1], 0)

    @pl.when(i == 0)
    def _():
        zbuf[...] = jnp.zeros_like(zbuf)
        for_zero_fills(lambda r: zero_row(r).start(), lambda c: zero_chunk(c).start())

    slot = i % 2
    stage[slot] = hp_ref[...]

    def body(lt, _):
        src = stage.at[slot, pl.ds(lt * sr, sr)]
        for k in range(TOP_K):
            pltpu.make_async_copy(src, xs_hbm.at[pl.ds(dest_ref[i, lt * TOP_K + k] * sr, sr)],
                                  sem.at[slot]).start(priority=k % 2)
        return 0
    lax.fori_loop(0, tt, body, 0)

    def wait_step(s):
        for _ in range(TOP_K):
            pltpu.make_async_copy(stage.at[s], xs_hbm.at[pl.ds(0, tt * sr)], sem.at[s]).wait()

    @pl.when(i > 0)
    def _():
        wait_step((i + 1) % 2)

    @pl.when(i == n - 1)
    def _():
        wait_step(i % 2)
        for_zero_fills(lambda r: zero_row(r).wait(), lambda c: zero_chunk(c).wait())


def _dispatch(hp, dest2d, zero_lo, zero_hi, total, npad, sr):
    nsteps, per_step = dest2d.shape
    w = hp.shape[1]
    zrows = 64
    assert npad % zrows == 0
    return pl.pallas_call(
        functools.partial(_dispatch_kernel, per_step=per_step, npad=npad, zrows=zrows, sr=sr),
        grid_spec=pltpu.PrefetchScalarGridSpec(
            num_scalar_prefetch=4,
            grid=(nsteps,),
            in_specs=[pl.BlockSpec((per_step // TOP_K * sr, w), lambda i, *_: (i, 0))],
            out_specs=pl.BlockSpec(memory_space=pl.ANY),
            scratch_shapes=[pltpu.VMEM((2, per_step // TOP_K * sr, w), hp.dtype),
                            pltpu.VMEM((zrows * sr, w), hp.dtype),
                            pltpu.SemaphoreType.DMA((2,)), pltpu.SemaphoreType.DMA((1,))],
        ),
        out_shape=jax.ShapeDtypeStruct((npad * sr, w), hp.dtype),
        compiler_params=_cparams(("arbitrary",)),
        name="dispatch",
    )(dest2d, zero_lo, zero_hi, total, hp)


def _moe_kernel(be_ref, na_ref, x_ref, wg_ref, wu_ref, bg_ref, bu_ref, wd_ref, bd_ref,
                o_ref, xb_ref, act_ref, *, nc1):
    i = pl.program_id(0)
    c = pl.program_id(1)
    active = i < na_ref[0]
    tn = act_ref.shape[2]

    @pl.when(jnp.logical_and(active, c == 0))
    def _():
        half = x_ref.shape[1]
        lo, hi = _unpack_bf16_pairs(x_ref[...])
        xb_ref[:, :half] = lo.astype(BF16)
        xb_ref[:, half:] = hi.astype(BF16)

    @pl.when(jnp.logical_and(active, c < nc1))
    def _():
        xb = xb_ref[...]
        g = jnp.dot(xb, wg_ref[0], preferred_element_type=F32) + bg_ref[0]
        u = jnp.dot(xb, wu_ref[0], preferred_element_type=F32) + bu_ref[0]
        gate = jnp.minimum(g, SWIGLU_LIMIT)
        up = jnp.clip(u, -SWIGLU_LIMIT, SWIGLU_LIMIT)
        act_ref[c] = (gate * jax.nn.sigmoid(SWIGLU_ALPHA * gate) * (up + 1.0)).astype(BF16)

    @pl.when(jnp.logical_and(active, c >= nc1))
    def _():
        y = bd_ref[0] + jnp.dot(act_ref[0], wd_ref[0, 0:tn, :], preferred_element_type=F32)
        for cc in range(1, nc1):
            y += jnp.dot(act_ref[cc], wd_ref[0, cc * tn:(cc + 1) * tn, :],
                         preferred_element_type=F32)
        o_ref[...] = _pack_bf16_pairs(y)

    @pl.when(jnp.logical_and(jnp.logical_not(active), c >= nc1))
    def _():
        o_ref[...] = jnp.zeros_like(o_ref)


def _moe(xs, blk_expert, n_active, w_gu, b_gu, w_down, b_down, bm, tn, tcol):
    npad, w = xs.shape
    ne, de, d = w_down.shape
    nc1 = de // tn
    nc2 = d // tcol
    nblk = npad // bm

    def blk(i, na):
        return jnp.clip(i, 0, na[0] - 1)

    def x_map(i, c, be, na):
        return (jnp.where(c >= nc1, blk(i + 1, na), blk(i, na)), 0)

    def gu_map(first_next, col0):
        def index(i, c, be, na):
            nxt = c >= first_next
            e = be[jnp.where(nxt, blk(i + 1, na), blk(i, na))]
            return (e, 0, col0 + jnp.where(nxt, 0, jnp.minimum(c, nc1 - 1)))
        return index

    def down_map(i, c, be, na):
        cur = c >= nc1
        e = be[jnp.where(cur, blk(i, na), blk(i - 1, na))]
        return (e, 0, jnp.where(cur, c - nc1, nc2 - 1))

    g_next = nc1 + nc2 - 1
    u_next = nc1 + nc2
    in_specs = [
        pl.BlockSpec((bm, w), x_map),
        pl.BlockSpec((1, d, tn), gu_map(g_next, 0)),
        pl.BlockSpec((1, d, tn), gu_map(u_next, nc1)),
        pl.BlockSpec((1, 1, tn), gu_map(g_next, 0)),
        pl.BlockSpec((1, 1, tn), gu_map(u_next, nc1)),
        pl.BlockSpec((1, de, tcol), down_map),
        pl.BlockSpec((1, 1, tcol), down_map),
    ]
    return pl.pallas_call(
        functools.partial(_moe_kernel, nc1=nc1),
        grid_spec=pltpu.PrefetchScalarGridSpec(
            num_scalar_prefetch=2,
            grid=(nblk, nc1 + nc2),
            in_specs=in_specs,
            out_specs=pl.BlockSpec((bm, tcol // 2),
                                   lambda i, c, be, na: (i, jnp.maximum(c - nc1, 0))),
            scratch_shapes=[pltpu.VMEM((bm, d), BF16), pltpu.VMEM((nc1, bm, tn), BF16)],
        ),
        out_shape=jax.ShapeDtypeStruct((npad, d // 2), jnp.uint32),
        compiler_params=_cparams(("arbitrary", "arbitrary")),
        name="moe",
    )(blk_expert, n_active, xs, w_gu, w_gu, b_gu, b_gu, w_down, b_down)


def _combine_kernel(idx_ref, y_hbm, h_ref, wt_ref, g_ref, o_ref, buf, sem, *, rows, tt, hw):
    i = pl.program_id(0)
    n = pl.num_programs(0)

    def issue(step, slot):
        def body(r4, _):
            for j in range(4):
                r = r4 * 4 + j
                src = idx_ref[step, r]
                pltpu.make_async_copy(y_hbm.at[pl.ds(src, 1)], buf.at[slot, pl.ds(r, 1)],
                                      sem.at[slot]).start(priority=j % 2)
            return 0
        lax.fori_loop(0, rows // 4, body, 0)

    @pl.when(i == 0)
    def _():
        issue(0, 0)

    @pl.when(i + 1 < n)
    def _():
        issue(i + 1, (i + 1) % 2)

    slot = i % 2
    pltpu.make_async_copy(y_hbm.at[pl.ds(0, rows)], buf.at[slot], sem.at[slot]).wait()
    wts = wt_ref[...]
    lo = hi = None
    for k in range(TOP_K):
        tl, th = _unpack_bf16_pairs(buf[slot, k * tt:(k + 1) * tt, :])
        wk = wts[:, k:k + 1]
        lo = wk * tl if lo is None else lo + wk * tl
        hi = wk * th if hi is None else hi + wk * th
    pieces = []
    for w0 in range(0, lo.shape[1], hw):
        pieces += [lo[:, w0:w0 + hw], hi[:, w0:w0 + hw]]
    acc = h_ref[...] + jnp.concatenate(pieces, axis=1)
    ms = jnp.mean(acc * acc, axis=-1, keepdims=True)
    o_ref[...] = acc * lax.rsqrt(ms + EPS) * g_ref[...]


def _combine(ys, idx2d, h, top_w, gain, tt, tcol):
    nsteps, rows = idx2d.shape
    d = h.shape[1]
    w = ys.shape[1]
    return pl.pallas_call(
        functools.partial(_combine_kernel, rows=rows, tt=tt, hw=tcol // 2),
        grid_spec=pltpu.PrefetchScalarGridSpec(
            num_scalar_prefetch=1,
            grid=(nsteps,),
            in_specs=[
                pl.BlockSpec(memory_space=pl.ANY),
                pl.BlockSpec((tt, d), lambda i, idx: (i, 0)),
                pl.BlockSpec((tt, TOP_K), lambda i, idx: (i, 0)),
                pl.BlockSpec((1, d), lambda i, idx: (0, 0)),
            ],
            out_specs=pl.BlockSpec((tt, d), lambda i, idx: (i, 0)),
            scratch_shapes=[pltpu.VMEM((2, rows, w), jnp.uint32),
                            pltpu.SemaphoreType.DMA((2,))],
        ),
        out_shape=jax.ShapeDtypeStruct(h.shape, F32),
        compiler_params=_cparams(("arbitrary",)),
        name="combine",
    )(idx2d, ys, h, top_w, gain.reshape(1, d).astype(F32))


def _dispatch_plan(top_idx, rank, counts, bm, nblk):
    ne = counts.shape[0]
    padded = ((counts + bm - 1) // bm) * bm
    pad_end = jnp.cumsum(padded)
    pad_start = pad_end - padded
    onehot = top_idx[..., None] == jnp.arange(ne, dtype=jnp.int32)
    dest = rank + jnp.sum(jnp.where(onehot, pad_start, 0), axis=-1)
    blk_row0 = jnp.arange(nblk, dtype=jnp.int32) * bm
    blk_expert = jnp.minimum(
        jnp.sum((pad_end[None, :] <= blk_row0[:, None]).astype(jnp.int32), axis=1), ne - 1)
    total = pad_end[-1:].astype(jnp.int32)
    return (dest.astype(jnp.int32), blk_expert, total // bm,
            (pad_start + counts).astype(jnp.int32), pad_end.astype(jnp.int32), total)


def _rope_tables(n_pos):
    pos = jnp.arange(n_pos, dtype=F32)
    inv_freq = ROPE_THETA ** (-jnp.arange(0, QK_ROPE, 2, dtype=F32) / QK_ROPE)
    ang = pos[:, None] * inv_freq[None, :]
    cos, sin = jnp.cos(ang), jnp.sin(ang)
    zero = jnp.zeros((n_pos, LANES - QK_ROPE), F32)
    return jnp.concatenate([cos, cos, zero], axis=1), jnp.concatenate([-sin, sin, zero], axis=1)


def _with_swapped_rope(w_rope):
    half = QK_ROPE // 2
    return jnp.concatenate([w_rope, w_rope[..., half:], w_rope[..., :half]], axis=-1)


def kernel(x, meta_tokens, g_mix, w_in, b_gate, g_q_lat, g_kv_lat, w_uq, w_ukv, s5_a_re, s5_a_im, s5_log_dt, s5_b_re, s5_b_im, s5_c_re, s5_c_im, s5_d, w_glu_a, w_glu_b, w_out, g_ffn, w_router, b_router, w_gu, b_gu, w_down, b_down, g_final):
    assert w_in.shape[0] == 1, "single-layer block"
    nb, seq, d = x.shape
    t = nb * seq
    n_meta = meta_tokens.shape[0]
    ql, kvl = g_q_lat.shape[-1], g_kv_lat.shape[-1]
    sw = s5_d.shape[-1]
    nh = w_uq.shape[-1] // QK_DIM
    ne = w_router.shape[-1]
    assert ql % kvl == 0 and (ql + kvl) % LANES == 0 and sw % LANES == 0 and d % LANES == 0
    assert n_meta % (2 * SUBLANES) == 0 and n_meta <= LANES

    w0 = w_in[0]
    c_kv = ql + kvl
    c_kr = c_kv + QK_ROPE
    w_main = jnp.concatenate([w0[:, :c_kv], w0[:, c_kr:]], axis=1).astype(BF16)
    w_kr = _with_swapped_rope(w0[:, c_kv:c_kr]).astype(BF16)
    o_s5 = c_kv
    o_gs = o_s5 + sw
    o_gm = o_gs + d
    wq3 = w_uq[0].reshape(ql, nh, QK_DIM)
    w_q = jnp.concatenate([wq3[..., :QK_NOPE], _with_swapped_rope(wq3[..., QK_NOPE:])],
                          axis=-1).reshape(ql, nh * HEAD_PAD).astype(BF16)
    w_kv = w_ukv[0].astype(BF16)
    cos_t, sin_t = _rope_tables(n_meta + seq)
    rope_meta = (cos_t[:n_meta], sin_t[:n_meta])
    rope_real = (cos_t[n_meta:], sin_t[n_meta:])
    s5_tabs = _s5_tables(s5_a_re[0], s5_a_im[0], s5_log_dt[0], s5_b_re[0], s5_b_im[0],
                         s5_c_re[0], s5_c_im[0], s5_d[0])
    bw, cw, a_tab, g_tab_fn, d_skip = s5_tabs

    x2 = x.reshape(t, d)
    tm = 512

    proj_m = _norm_mm(meta_tokens, 0, g_mix[0], w_main, BF16, n_meta, 512, name="proj_meta")
    kr_m = _norm_mm(meta_tokens, 0, g_mix[0], w_kr, BF16, n_meta, LANES,
                    rope=(*rope_meta, LANES), name="krope_meta")
    kv_m = _norm_mm(proj_m, ql // kvl, g_kv_lat[0], w_kv, BF16, n_meta, 1024, name="kv_meta")
    kvm = jnp.pad(kv_m, ((0, LANES - n_meta), (0, 0)))
    krm = jnp.pad(kr_m, ((0, LANES - n_meta), (0, 0)))
    ntile = sw // LANES
    ns = bw.shape[2] // 2
    _, c_meta = _s5_scan(proj_m, o_s5 // LANES, 1, n_meta, n_meta,
                         (bw, cw, a_tab, g_tab_fn(n_meta // SUBLANES), d_skip),
                         jnp.zeros((ntile, 2, ns), F32), "s5_meta")

    proj = _norm_mm(x2, 0, g_mix[0], w_main, BF16, tm, 1024, name="proj")
    kr = _norm_mm(x2, 0, g_mix[0], w_kr, BF16, tm, LANES, rope=(*rope_real, LANES), name="krope")
    q = _norm_mm(proj, 0, g_q_lat[0], w_q, BF16, 1024, 2048, rope=(*rope_real, HEAD_PAD),
                 out_scale=math.log2(math.e) * QK_DIM ** -0.5, name="q")
    kv = _norm_mm(proj, ql // kvl, g_kv_lat[0], w_kv, BF16, 1024, 2048, name="kv")

    tc = _pick(seq, 512, 2 * SUBLANES)
    z, _ = _s5_scan(proj, o_s5 // LANES, nb, seq, tc,
                    (bw, cw, a_tab, g_tab_fn(tc // SUBLANES), d_skip), c_meta[0], "s5")
    bias = b_gate[0].astype(F32).reshape(1, 2 * d)
    ys = _glu(z, w_glu_a[0].astype(BF16), w_glu_b[0].astype(BF16), proj, o_gs, bias[:, :d], 1024, 1024)
    ym = _attention(q, kv, kr, kvm, krm, proj, o_gm, bias[:, d:], nb, seq, nh, n_meta, 2048, 512)
    h = _outproj(ys, ym, w_out[0].astype(BF16), x2, tm, 1024)

    hp, top_idx, top_w, rank, counts = _router(h, g_ffn[0], w_router[0], b_router[0], 256)
    bm = 512 if t * TOP_K >= 512 * ne else 128
    nblk = (t * TOP_K + ne * (bm - 1) + bm - 1) // bm
    dest, blk_expert, n_active, zero_lo, zero_hi, total = _dispatch_plan(
        top_idx, rank, counts[0], bm, nblk)
    per_step = 1024 if t * TOP_K % 1024 == 0 else 512
    xs = _dispatch(hp, dest.reshape(t * TOP_K // per_step, per_step), zero_lo, zero_hi, total,
                   nblk * bm, 1)
    de = w_down.shape[2]
    tcol = _pick(d, 1024)
    ysort = _moe(xs, blk_expert, n_active, w_gu[0].astype(BF16), b_gu[0].reshape(ne, 1, 2 * de),
                 w_down[0].astype(BF16), b_down[0].reshape(ne, 1, d), bm, _pick(de, 512), tcol)
    tt = 128
    idx_c = dest.reshape(t // tt, tt, TOP_K).transpose(0, 2, 1).reshape(t // tt, TOP_K * tt)
    out = _combine(ysort, idx_c, h, top_w, g_final, tt, tcol)
    return out.reshape(nb, seq, d)
```

```python
import functools
import math

import jax
import jax.numpy as jnp
from jax import lax
from jax.experimental import pallas as pl
from jax.experimental.pallas import tpu as pltpu

F32 = jnp.float32
BF16 = jnp.bfloat16

EPS = 1e-6
QK_NOPE = 128
QK_ROPE = 64
QK_DIM = QK_NOPE + QK_ROPE
V_HEAD = 128
ROPE_THETA = 10000.0
NEG_INF = -1e30
S5_GROUP = 16
TOP_K = 4
SWIGLU_ALPHA = 1.702
SWIGLU_LIMIT = 7.0

LANES = 128
SUBLANES = 8
HEAD_PAD = 2 * LANES
S5_TILE_GROUPS = LANES // S5_GROUP
VMEM_LIMIT_V7X = 56 * 1024 * 1024


def _cparams(sem, vmem=VMEM_LIMIT_V7X):
    return pltpu.CompilerParams(dimension_semantics=sem, vmem_limit_bytes=vmem)


def _pack_bf16_pairs(x):
    half = x.shape[1] // 2
    bits = pltpu.bitcast(x.astype(BF16).astype(F32), jnp.uint32)
    return (bits[:, :half] >> 16) | (bits[:, half:] & jnp.uint32(0xFFFF0000))


def _unpack_bf16_pairs(word):
    return (pltpu.bitcast(word << 16, F32), pltpu.bitcast(word & jnp.uint32(0xFFFF0000), F32))


def _pick(n, pref, mult=LANES):
    if n <= pref:
        return n
    best = None
    for t in range(mult, pref + 1, mult):
        if n % t == 0:
            best = t
    assert best is not None, (n, pref, mult)
    return best


def _rope_slab(slab, c, s):
    return slab * c + pltpu.roll(slab, QK_ROPE, 1) * s


def _norm_mm_kernel(a_ref, g_ref, w_ref, *rest, rope_group, out_scale, side):
    if side:
        cos_ref, sin_ref, ws_ref, o_ref, os_ref, an_ref = rest
    elif rope_group:
        cos_ref, sin_ref, o_ref, an_ref = rest
    else:
        o_ref, an_ref = rest

    @pl.when(pl.program_id(1) == 0)
    def _():
        xf = a_ref[...].astype(F32)
        ms = jnp.mean(xf * xf, axis=-1, keepdims=True)
        an_ref[...] = (xf * lax.rsqrt(ms + EPS) * g_ref[...]).astype(BF16)
        if side:
            slab = jnp.dot(an_ref[...], ws_ref[...], preferred_element_type=F32)
            os_ref[...] = _rope_slab(slab, cos_ref[...], sin_ref[...]).astype(os_ref.dtype)

    acc = jnp.dot(an_ref[...], w_ref[...], preferred_element_type=F32)
    if out_scale is not None:
        acc = acc * out_scale
    if not rope_group:
        o_ref[...] = acc.astype(o_ref.dtype)
        return
    c = cos_ref[...]
    s = sin_ref[...]
    for g0 in range(0, acc.shape[1], rope_group):
        lo = g0 + rope_group - LANES
        if lo > g0:
            o_ref[:, g0:lo] = acc[:, g0:lo].astype(o_ref.dtype)
        o_ref[:, lo:lo + LANES] = _rope_slab(acc[:, lo:lo + LANES], c, s).astype(o_ref.dtype)


def _norm_mm(a, col_blk, gain, w, out_dtype, tm, tn, rope=None, side=None, out_scale=None,
             name=None):
    assert rope is None or side is None
    m = a.shape[0]
    k, n = w.shape
    tables = rope[:2] if rope is not None else side[1:] if side is not None else None
    tm = _pick(m if tables is None else math.gcd(m, tables[0].shape[0]), tm, SUBLANES)
    tn = _pick(n, tn)
    in_specs = [
        pl.BlockSpec((tm, k), lambda i, j: (i, col_blk)),
        pl.BlockSpec((1, k), lambda i, j: (0, 0)),
        pl.BlockSpec((k, tn), lambda i, j: (0, j)),
    ]
    args = [a, gain.reshape(1, k).astype(F32), w]
    out_specs = [pl.BlockSpec((tm, tn), lambda i, j: (i, j))]
    out_shape = [jax.ShapeDtypeStruct((m, n), out_dtype)]
    if tables is not None:
        nrb = tables[0].shape[0] // tm
        in_specs += [pl.BlockSpec((tm, LANES), lambda i, j: (i % nrb, 0))] * 2
        args += list(tables)
    if side is not None:
        in_specs.append(pl.BlockSpec((k, LANES), lambda i, j: (0, 0)))
        args.append(side[0])
        out_specs.append(pl.BlockSpec((tm, LANES), lambda i, j: (i, 0)))
        out_shape.append(jax.ShapeDtypeStruct((m, LANES), out_dtype))
    outs = pl.pallas_call(
        functools.partial(_norm_mm_kernel, rope_group=rope[2] if rope is not None else 0,
                          out_scale=out_scale, side=side is not None),
        grid=(m // tm, n // tn),
        in_specs=in_specs,
        out_specs=out_specs,
        out_shape=out_shape,
        scratch_shapes=[pltpu.VMEM((tm, k), BF16)],
        compiler_params=_cparams(("parallel", "arbitrary")),
        name=name,
    )(*args)
    return outs if side is not None else outs[0]


def _s5_kernel(u_ref, perm_ref, permt_ref, bw_ref, cw_ref, a_ref, g_ref, d_ref, c0_ref,
               z_ref, cout_ref, bu_ref, xb_ref, carry_ref, *, tc, ns):
    t = pl.program_id(2)
    tseg = tc // SUBLANES

    @pl.when(t == 0)
    def _():
        carry_ref[...] = c0_ref[0]

    u_p = jnp.dot(perm_ref[...], u_ref[...], preferred_element_type=F32).astype(BF16)
    bu_ref[...] = jnp.dot(u_p, bw_ref[0], preferred_element_type=F32)

    ar = jnp.broadcast_to(a_ref[0, 0:1, :], (SUBLANES, ns))
    ai = jnp.broadcast_to(a_ref[0, 1:2, :], (SUBLANES, ns))

    def local_scan(i, x):
        xr, xi = x
        r0 = pl.multiple_of(i * SUBLANES, SUBLANES)
        nr = ar * xr - ai * xi + bu_ref[pl.ds(r0, SUBLANES), 0:ns]
        ni = ar * xi + ai * xr + bu_ref[pl.ds(r0, SUBLANES), ns:2 * ns]
        bu_ref[pl.ds(r0, SUBLANES), 0:ns] = nr
        bu_ref[pl.ds(r0, SUBLANES), ns:2 * ns] = ni
        return nr, ni

    zero = jnp.zeros((SUBLANES, ns), F32)
    er, ei = lax.fori_loop(0, tseg, local_scan, (zero, zero), unroll=2 if tseg % 4 == 0 else 1)

    c_r = carry_ref[0:1, :]
    c_i = carry_ref[1:2, :]
    sub = lax.broadcasted_iota(jnp.int32, (SUBLANES, ns), 0)
    g1r = g_ref[0, 0:1, :]
    g1i = g_ref[0, 1:2, :]
    fr = er + jnp.where(sub == 0, g1r * c_r - g1i * c_i, 0.0)
    fi = ei + jnp.where(sub == 0, g1r * c_i + g1i * c_r, 0.0)
    for k, d in enumerate((1, 2, 4)):
        gr = g_ref[0, 2 * k:2 * k + 1, :]
        gi = g_ref[0, 2 * k + 1:2 * k + 2, :]
        sr = jnp.where(sub >= d, pltpu.roll(fr, d, 0), 0.0)
        si = jnp.where(sub >= d, pltpu.roll(fi, d, 0), 0.0)
        fr, fi = fr + gr * sr - gi * si, fi + gr * si + gi * sr
    cin_r = jnp.where(sub == 0, c_r, pltpu.roll(fr, 1, 0))
    cin_i = jnp.where(sub == 0, c_i, pltpu.roll(fi, 1, 0))
    carry_ref[0:1, :] = fr[SUBLANES - 1:SUBLANES, :]
    carry_ref[1:2, :] = fi[SUBLANES - 1:SUBLANES, :]

    def add_carry(j, c):
        cr, ci = c
        c1r, c1i = ar * cr - ai * ci, ar * ci + ai * cr
        c2r, c2i = ar * c1r - ai * c1i, ar * c1i + ai * c1r
        r0 = pl.multiple_of(j * 2 * SUBLANES, 2 * SUBLANES)
        r1 = r0 + SUBLANES
        xr = jnp.concatenate([bu_ref[pl.ds(r0, SUBLANES), 0:ns] + c1r,
                              bu_ref[pl.ds(r1, SUBLANES), 0:ns] + c2r], axis=0)
        xi = jnp.concatenate([bu_ref[pl.ds(r0, SUBLANES), ns:2 * ns] + c1i,
                              bu_ref[pl.ds(r1, SUBLANES), ns:2 * ns] + c2i], axis=0)
        xb_ref[pl.ds(r0, 2 * SUBLANES), 0:ns] = xr.astype(BF16)
        xb_ref[pl.ds(r0, 2 * SUBLANES), ns:2 * ns] = xi.astype(BF16)
        return c2r, c2i

    lax.fori_loop(0, tseg // 2, add_carry, (cin_r, cin_i), unroll=2 if tseg % 4 == 0 else 1)

    y = jnp.dot(xb_ref[...], cw_ref[0], preferred_element_type=F32) + d_ref[...] * u_p.astype(F32)
    zp = jax.nn.gelu(y).astype(BF16)
    z_ref[...] = jnp.dot(permt_ref[...], zp, preferred_element_type=F32).astype(BF16)

    @pl.when(t == pl.num_programs(2) - 1)
    def _():
        cout_ref[0, 0] = carry_ref[...]


def _s5_scan(u_arr, u_col0, nb, seq, tc, tabs, c0, name):
    bw, cw, a_tab, g_tab, d_skip = tabs
    ntile, _, ns2 = bw.shape
    ns = ns2 // 2
    nt = seq // tc
    tseg = tc // SUBLANES
    r = jnp.arange(tc, dtype=jnp.int32)
    src = (r % SUBLANES) * tseg + r // SUBLANES
    perm = (src[:, None] == jnp.arange(tc, dtype=jnp.int32)[None, :]).astype(BF16)
    kern = functools.partial(_s5_kernel, tc=tc, ns=ns)
    return pl.pallas_call(
        kern,
        grid=(nb, ntile, nt),
        in_specs=[
            pl.BlockSpec((tc, LANES), lambda b, n, t: (b * nt + t, u_col0 + n)),
            pl.BlockSpec((tc, tc), lambda b, n, t: (0, 0)),
            pl.BlockSpec((tc, tc), lambda b, n, t: (0, 0)),
            pl.BlockSpec((1, LANES, 2 * ns), lambda b, n, t: (n, 0, 0)),
            pl.BlockSpec((1, 2 * ns, LANES), lambda b, n, t: (n, 0, 0)),
            pl.BlockSpec((1, 2, ns), lambda b, n, t: (n, 0, 0)),
            pl.BlockSpec((1, 6, ns), lambda b, n, t: (n, 0, 0)),
            pl.BlockSpec((1, LANES), lambda b, n, t: (0, n)),
            pl.BlockSpec((1, 2, ns), lambda b, n, t: (n, 0, 0)),
        ],
        out_specs=[
            pl.BlockSpec((tc, LANES), lambda b, n, t: (b * nt + t, n)),
            pl.BlockSpec((1, 1, 2, ns), lambda b, n, t: (b, n, 0, 0)),
        ],
        out_shape=[
            jax.ShapeDtypeStruct((nb * seq, ntile * LANES), BF16),
            jax.ShapeDtypeStruct((nb, ntile, 2, ns), F32),
        ],
        scratch_shapes=[
            pltpu.VMEM((tc, 2 * ns), F32),
            pltpu.VMEM((tc, 2 * ns), BF16),
            pltpu.VMEM((2, ns), F32),
        ],
        compiler_params=_cparams(("parallel", "parallel", "arbitrary")),
        name=name,
    )(u_arr, perm, perm.T, bw, cw, a_tab, g_tab, d_skip, c0)


def _s5_tables(a_re, a_im, log_dt, b_re, b_im, c_re, c_im, d_skip):
    ngrp, nst = a_re.shape
    ntile = ngrp // S5_TILE_GROUPS
    ns = S5_TILE_GROUPS * nst
    dt = jnp.exp(log_dt.astype(F32))[:, None]
    lr, li = a_re.astype(F32), a_im.astype(F32)

    def a_pow(k):
        mag = jnp.exp(lr * dt * k)
        return mag * jnp.cos(li * dt * k), mag * jnp.sin(li * dt * k)

    ab_re, ab_im = a_pow(1)
    den = lr * lr + li * li
    f_re = ((ab_re - 1.0) * lr + ab_im * li) / den
    f_im = (ab_im * lr - (ab_re - 1.0) * li) / den
    br, bi = b_re.astype(F32), b_im.astype(F32)
    bb_re = f_re[..., None] * br - f_im[..., None] * bi
    bb_im = f_re[..., None] * bi + f_im[..., None] * br
    eye = jnp.eye(S5_TILE_GROUPS, dtype=F32)

    def b_tile(bb):
        bb = bb.reshape(ntile, S5_TILE_GROUPS, nst, S5_GROUP)
        return jnp.einsum("ab,napc->nacbp", eye, bb).reshape(ntile, LANES, ns)

    def c_tile(cc):
        cc = cc.astype(F32).reshape(ntile, S5_TILE_GROUPS, S5_GROUP, nst)
        return jnp.einsum("ab,nacp->nbpac", eye, cc).reshape(ntile, ns, LANES)

    bw = jnp.concatenate([b_tile(bb_re), b_tile(bb_im)], axis=2).astype(BF16)
    cw = jnp.concatenate([c_tile(c_re), -c_tile(c_im)], axis=1).astype(BF16)
    a_tab = jnp.stack([ab_re.reshape(ntile, ns), ab_im.reshape(ntile, ns)], axis=1)

    def g_tab(tseg):
        rows = []
        for d in (1, 2, 4):
            pr, pi = a_pow(tseg * d)
            rows += [pr.reshape(ntile, ns), pi.reshape(ntile, ns)]
        return jnp.stack(rows, axis=1)

    return bw, cw, a_tab, g_tab, d_skip.astype(F32).reshape(1, -1)


def _glu_kernel(z_ref, wa_ref, wb_ref, gs_ref, bg_ref, o_ref):
    z = z_ref[...]
    a = jnp.dot(z, wa_ref[...], preferred_element_type=F32)
    b = jnp.dot(z, wb_ref[...], preferred_element_type=F32)
    gate = jax.nn.sigmoid(gs_ref[...].astype(F32) + bg_ref[...])
    o_ref[...] = (a * jax.nn.sigmoid(b) * gate).astype(o_ref.dtype)


def _glu(z, wa, wb, proj, gs_col0, bias, tm, tn):
    m, k = z.shape
    n = wa.shape[1]
    tm = _pick(m, tm, SUBLANES)
    tn = _pick(math.gcd(n, gs_col0) if gs_col0 else n, tn)
    goff = gs_col0 // tn
    return pl.pallas_call(
        _glu_kernel,
        grid=(m // tm, n // tn),
        in_specs=[
            pl.BlockSpec((tm, k), lambda i, j: (i, 0)),
            pl.BlockSpec((k, tn), lambda i, j: (0, j)),
            pl.BlockSpec((k, tn), lambda i, j: (0, j)),
            pl.BlockSpec((tm, tn), lambda i, j: (i, goff + j)),
            pl.BlockSpec((1, tn), lambda i, j: (0, j)),
        ],
        out_specs=pl.BlockSpec((tm, tn), lambda i, j: (i, j)),
        out_shape=jax.ShapeDtypeStruct((m, n), BF16),
        compiler_params=_cparams(("parallel", "arbitrary")),
        name="glu",
    )(z, wa, wb, proj, bias)


def _attn_kernel(q_ref, kn_ref, v_ref, kr_ref, km_ref, vm_ref, krm_ref, gm_ref, bg_ref, o_ref,
                 *, tq, tk, n_meta):
    qi = pl.program_id(2)

    def update(q, carry, j, mask_row0):
        m, l, acc = carry
        r0 = pl.multiple_of(j * tk, tk)
        k = jnp.concatenate([kn_ref[pl.ds(r0, tk), :], kr_ref[pl.ds(r0, tk), :]], axis=1)
        s = lax.dot_general(q, k, (((1,), (1,)), ((), ())), preferred_element_type=F32)
        if mask_row0 is not None:
            row = mask_row0 + lax.broadcasted_iota(jnp.int32, s.shape, 0)
            colp = j * tk + lax.broadcasted_iota(jnp.int32, s.shape, 1)
            s = jnp.where(colp <= row, s, NEG_INF)
        m_new = jnp.maximum(m, jnp.max(s, axis=1, keepdims=True))
        alpha = jnp.exp2(m - m_new)
        p = jnp.exp2(s - m_new)
        l = alpha * l + jnp.sum(p, axis=1, keepdims=True)
        acc = alpha * acc + jnp.dot(p.astype(BF16), v_ref[pl.ds(r0, tk), :],
                                    preferred_element_type=F32)
        return m_new, l, acc

    q = q_ref[...]
    km = jnp.concatenate([km_ref[...], krm_ref[...]], axis=1)
    s = lax.dot_general(q, km, (((1,), (1,)), ((), ())), preferred_element_type=F32)
    col = lax.broadcasted_iota(jnp.int32, s.shape, 1)
    s = jnp.where(col < n_meta, s, NEG_INF)
    m = jnp.max(s, axis=1, keepdims=True)
    p = jnp.exp2(s - m)
    l = jnp.sum(p, axis=1, keepdims=True)
    acc = jnp.dot(p.astype(BF16), vm_ref[...], preferred_element_type=F32)

    n_full = (qi * tq) // tk
    m, l, acc = lax.fori_loop(0, n_full, lambda j, c: update(q, c, j, None), (m, l, acc))

    for r in range(tq // tk):
        rows = slice(r * tk, (r + 1) * tk)
        qr = q_ref[rows, :]
        carry = (m[rows], l[rows], acc[rows])
        for dd in range(r):
            carry = update(qr, carry, n_full + dd, None)
        _, lr, ar = update(qr, carry, n_full + r, qi * tq + r * tk)
        gate = jax.nn.sigmoid(gm_ref[rows, :].astype(F32) + bg_ref[...])
        o_ref[rows, :] = (ar / lr * gate).astype(o_ref.dtype)


def _attention(q, kv, kr, kvm, krm, proj, gm_col0, bias, nb, seq, nh, n_meta, tq, tk):
    tq = _pick(seq, tq, SUBLANES)
    tk = _pick(tq, tk, SUBLANES)
    nq = seq // tq
    mrows = kvm.shape[0]
    goff = gm_col0 // LANES
    kern = functools.partial(_attn_kernel, tq=tq, tk=tk, n_meta=n_meta)
    return pl.pallas_call(
        kern,
        grid=(nb, nh, nq),
        in_specs=[
            pl.BlockSpec((tq, HEAD_PAD), lambda b, h, i: (b * nq + i, h)),
            pl.BlockSpec((seq, LANES), lambda b, h, i: (b, 2 * h)),
            pl.BlockSpec((seq, LANES), lambda b, h, i: (b, 2 * h + 1)),
            pl.BlockSpec((seq, LANES), lambda b, h, i: (b, 0)),
            pl.BlockSpec((mrows, LANES), lambda b, h, i: (0, 2 * h)),
            pl.BlockSpec((mrows, LANES), lambda b, h, i: (0, 2 * h + 1)),
            pl.BlockSpec((mrows, LANES), lambda b, h, i: (0, 0)),
            pl.BlockSpec((tq, LANES), lambda b, h, i: (b * nq + i, goff + h)),
            pl.BlockSpec((1, LANES), lambda b, h, i: (0, h)),
        ],
        out_specs=pl.BlockSpec((tq, LANES), lambda b, h, i: (b * nq + i, h)),
        out_shape=jax.ShapeDtypeStruct((nb * seq, nh * V_HEAD), BF16),
        compiler_params=_cparams(("parallel", "parallel", "arbitrary")),
        name="attention",
    )(q, kv, kv, kr, kvm, kvm, krm, proj, bias)


def _outproj_kernel(ys_ref, ym_ref, w_ref, res_ref, o_ref, a_ref):
    @pl.when(pl.program_id(1) == 0)
    def _():
        a_ref[...] = (ys_ref[...].astype(F32) + ym_ref[...].astype(F32)).astype(BF16)

    o_ref[...] = jnp.dot(a_ref[...], w_ref[...], preferred_element_type=F32) + res_ref[...]


def _outproj(ys, ym, w, res, tm, tn):
    m, k = ys.shape
    n = w.shape[1]
    tm = _pick(m, tm, SUBLANES)
    tn = _pick(n, tn)
    return pl.pallas_call(
        _outproj_kernel,
        grid=(m // tm, n // tn),
        in_specs=[
            pl.BlockSpec((tm, k), lambda i, j: (i, 0)),
            pl.BlockSpec((tm, k), lambda i, j: (i, 0)),
            pl.BlockSpec((k, tn), lambda i, j: (0, j)),
            pl.BlockSpec((tm, tn), lambda i, j: (i, j)),
        ],
        out_specs=pl.BlockSpec((tm, tn), lambda i, j: (i, j)),
        out_shape=jax.ShapeDtypeStruct((m, n), F32),
        scratch_shapes=[pltpu.VMEM((tm, k), BF16)],
        compiler_params=_cparams(("parallel", "arbitrary")),
        name="outproj",
    )(ys, ym, w, res)


def _router_kernel(h_ref, g_ref, wr_ref, wrl_ref, br_ref, hp_ref, idx_ref, wt_ref, rank_ref,
                   cnt_ref, run_ref):
    @pl.when(pl.program_id(0) == 0)
    def _():
        run_ref[...] = jnp.zeros_like(run_ref)

    xf = h_ref[...]
    tm, d = xf.shape
    ms = jnp.mean(xf * xf, axis=-1, keepdims=True)
    hn = xf * lax.rsqrt(ms + EPS) * g_ref[...]

    hp_ref[...] = _pack_bf16_pairs(hn)

    hn_hi = hn.astype(BF16)
    hn_lo = (hn - hn_hi.astype(F32)).astype(BF16)
    logits = (jnp.dot(hn_hi, wr_ref[...], preferred_element_type=F32)
              + jnp.dot(hn_lo, wr_ref[...], preferred_element_type=F32)
              + jnp.dot(hn_hi, wrl_ref[...], preferred_element_type=F32)) + br_ref[...]
    ne = logits.shape[1]
    lane = lax.broadcasted_iota(jnp.int32, logits.shape, 1)
    vals, ids = [], []
    for _ in range(TOP_K):
        mx = jnp.max(logits, axis=1, keepdims=True)
        ix = jnp.min(jnp.where(logits == mx, lane, ne), axis=1, keepdims=True)
        vals.append(mx)
        ids.append(ix)
        logits = jnp.where(lane == ix, -jnp.inf, logits)
    v = jnp.concatenate(vals, axis=1)
    e = jnp.exp(v - v[:, 0:1])
    idx_ref[...] = jnp.concatenate(ids, axis=1)
    wt_ref[...] = e / jnp.sum(e, axis=1, keepdims=True)

    tri = (lax.broadcasted_iota(jnp.int32, (tm, tm), 0)
           > lax.broadcasted_iota(jnp.int32, (tm, tm), 1)).astype(BF16)
    base = run_ref[...]
    ranks = []
    for k in range(TOP_K):
        oh = (lane == ids[k]).astype(F32)
        prefix = jnp.dot(tri, oh.astype(BF16), preferred_element_type=F32)
        ranks.append(jnp.sum(oh * (base + prefix), axis=1, keepdims=True))
        base = base + jnp.sum(oh, axis=0, keepdims=True)
    run_ref[...] = base
    rank_ref[...] = jnp.concatenate(ranks, axis=1).astype(jnp.int32)
    cnt_ref[...] = base.astype(jnp.int32)


def _router(h, gain, w_router, b_router, tm):
    m, d = h.shape
    ne = w_router.shape[1]
    tm = _pick(m, tm, SUBLANES)
    wr_hi = w_router.astype(BF16)
    wr_lo = (w_router.astype(F32) - wr_hi.astype(F32)).astype(BF16)
    return pl.pallas_call(
        _router_kernel,
        grid=(m // tm,),
        in_specs=[
            pl.BlockSpec((tm, d), lambda i: (i, 0)),
            pl.BlockSpec((1, d), lambda i: (0, 0)),
            pl.BlockSpec((d, ne), lambda i: (0, 0)),
            pl.BlockSpec((d, ne), lambda i: (0, 0)),
            pl.BlockSpec((1, ne), lambda i: (0, 0)),
        ],
        out_specs=[
            pl.BlockSpec((tm, d // 2), lambda i: (i, 0)),
            pl.BlockSpec((tm, TOP_K), lambda i: (i, 0)),
            pl.BlockSpec((tm, TOP_K), lambda i: (i, 0)),
            pl.BlockSpec((tm, TOP_K), lambda i: (i, 0)),
            pl.BlockSpec((1, ne), lambda i: (0, 0)),
        ],
        out_shape=[
            jax.ShapeDtypeStruct((m, d // 2), jnp.uint32),
            jax.ShapeDtypeStruct((m, TOP_K), jnp.int32),
            jax.ShapeDtypeStruct((m, TOP_K), F32),
            jax.ShapeDtypeStruct((m, TOP_K), jnp.int32),
            jax.ShapeDtypeStruct((1, ne), jnp.int32),
        ],
        scratch_shapes=[pltpu.VMEM((1, ne), F32)],
        compiler_params=_cparams(("arbitrary",)),
        name="router",
    )(h, gain.reshape(1, d).astype(F32), wr_hi, wr_lo, b_router.reshape(1, ne).astype(F32))


def _dispatch_kernel(dest_ref, zlo_ref, zhi_ref, tot_ref, hp_ref, xs_hbm, stage, zbuf, sem, zsem,
                     *, per_step, npad, zrows, sr):
    i = pl.program_id(0)
    n = pl.num_programs(0)
    ne = zlo_ref.shape[0]
    tt = per_step // TOP_K

    def zero_row(r):
        return pltpu.make_async_copy(zbuf.at[pl.ds(0, sr)], xs_hbm.at[pl.ds(r * sr, sr)],
                                     zsem.at[0])

    def zero_chunk(c):
        return pltpu.make_async_copy(zbuf, xs_hbm.at[pl.ds(c * (zrows * sr), zrows * sr)],
                                     zsem.at[0])

    def for_zero_fills(fn_row, fn_chunk):
        def per_expert(e, _):
            lax.fori_loop(zlo_ref[e], zhi_ref[e], lambda r, _: (fn_row(r), 0)[1], 0)
            return 0
        lax.fori_loop(0, ne, per_expert, 0)
        lax.fori_loop(tot_ref[0] // zrows, npad // zrows, lambda c, _: (fn_chunk(c), 0)[1], 0)

    @pl.when(i == 0)
    def _():
        zbuf[...] = jnp.zeros_like(zbuf)
        for_zero_fills(lambda r: zero_row(r).start(), lambda c: zero_chunk(c).start())

    slot = i % 2
    stage[slot] = hp_ref[...]

    def body(lt, _):
        src = stage.at[slot, pl.ds(lt * sr, sr)]
        for k in range(TOP_K):
            pltpu.make_async_copy(src, xs_hbm.at[pl.ds(dest_ref[i, lt * TOP_K + k] * sr, sr)],
                                  sem.at[slot]).start(priority=k % 2)
        return 0
    lax.fori_loop(0, tt, body, 0)

    def wait_step(s):
        for _ in range(TOP_K):
            pltpu.make_async_copy(stage.at[s], xs_hbm.at[pl.ds(0, tt * sr)], sem.at[s]).wait()

    @pl.when(i > 0)
    def _():
        wait_step((i + 1) % 2)

    @pl.when(i == n - 1)
    def _():
        wait_step(i % 2)
        for_zero_fills(lambda r: zero_row(r).wait(), lambda c: zero_chunk(c).wait())


def _dispatch(hp, dest2d, zero_lo, zero_hi, total, npad, sr):
    nsteps, per_step = dest2d.shape
    w = hp.shape[1]
    zrows = 64
    assert npad % zrows == 0
    return pl.pallas_call(
        functools.partial(_dispatch_kernel, per_step=per_step, npad=npad, zrows=zrows, sr=sr),
        grid_spec=pltpu.PrefetchScalarGridSpec(
            num_scalar_prefetch=4,
            grid=(nsteps,),
            in_specs=[pl.BlockSpec((per_step // TOP_K * sr, w), lambda i, *_: (i, 0))],
            out_specs=pl.BlockSpec(memory_space=pl.ANY),
            scratch_shapes=[pltpu.VMEM((2, per_step // TOP_K * sr, w), hp.dtype),
                            pltpu.VMEM((zrows * sr, w), hp.dtype),
                            pltpu.SemaphoreType.DMA((2,)), pltpu.SemaphoreType.DMA((1,))],
        ),
        out_shape=jax.ShapeDtypeStruct((npad * sr, w), hp.dtype),
        compiler_params=_cparams(("arbitrary",)),
        name="dispatch",
    )(dest2d, zero_lo, zero_hi, total, hp)


def _moe_kernel(be_ref, na_ref, x_ref, wg_ref, wu_ref, bg_ref, bu_ref, wd_ref, bd_ref,
                o_ref, xb_ref, act_ref, *, nc1):
    i = pl.program_id(0)
    c = pl.program_id(1)
    active = i < na_ref[0]
    tn = act_ref.shape[2]

    @pl.when(jnp.logical_and(active, c == 0))
    def _():
        half = x_ref.shape[1]
        lo, hi = _unpack_bf16_pairs(x_ref[...])
        xb_ref[:, :half] = lo.astype(BF16)
        xb_ref[:, half:] = hi.astype(BF16)

    @pl.when(jnp.logical_and(active, c < nc1))
    def _():
        xb = xb_ref[...]
        g = jnp.dot(xb, wg_ref[0], preferred_element_type=F32) + bg_ref[0]
        u = jnp.dot(xb, wu_ref[0], preferred_element_type=F32) + bu_ref[0]
        gate = jnp.minimum(g, SWIGLU_LIMIT)
        up = jnp.clip(u, -SWIGLU_LIMIT, SWIGLU_LIMIT)
        act_ref[c] = (gate * jax.nn.sigmoid(SWIGLU_ALPHA * gate) * (up + 1.0)).astype(BF16)

    @pl.when(jnp.logical_and(active, c >= nc1))
    def _():
        y = bd_ref[0] + jnp.dot(act_ref[0], wd_ref[0, 0:tn, :], preferred_element_type=F32)
        for cc in range(1, nc1):
            y += jnp.dot(act_ref[cc], wd_ref[0, cc * tn:(cc + 1) * tn, :],
                         preferred_element_type=F32)
        o_ref[...] = _pack_bf16_pairs(y)

    @pl.when(jnp.logical_and(jnp.logical_not(active), c >= nc1))
    def _():
        o_ref[...] = jnp.zeros_like(o_ref)


def _moe(xs, blk_expert, n_active, w_gu, b_gu, w_down, b_down, bm, tn, tcol):
    npad, w = xs.shape
    ne, de, d = w_down.shape
    nc1 = de // tn
    nc2 = d // tcol
    nblk = npad // bm

    def blk(i, na):
        return jnp.clip(i, 0, na[0] - 1)

    def x_map(i, c, be, na):
        return (jnp.where(c >= nc1, blk(i + 1, na), blk(i, na)), 0)

    def gu_map(first_next, col0):
        def index(i, c, be, na):
            nxt = c >= first_next
            e = be[jnp.where(nxt, blk(i + 1, na), blk(i, na))]
            return (e, 0, col0 + jnp.where(nxt, 0, jnp.minimum(c, nc1 - 1)))
        return index

    def down_map(i, c, be, na):
        cur = c >= nc1
        e = be[jnp.where(cur, blk(i, na), blk(i - 1, na))]
        return (e, 0, jnp.where(cur, c - nc1, nc2 - 1))

    g_next = nc1 + nc2 - 1
    u_next = nc1 + nc2
    in_specs = [
        pl.BlockSpec((bm, w), x_map),
        pl.BlockSpec((1, d, tn), gu_map(g_next, 0)),
        pl.BlockSpec((1, d, tn), gu_map(u_next, nc1)),
        pl.BlockSpec((1, 1, tn), gu_map(g_next, 0)),
        pl.BlockSpec((1, 1, tn), gu_map(u_next, nc1)),
        pl.BlockSpec((1, de, tcol), down_map),
        pl.BlockSpec((1, 1, tcol), down_map),
    ]
    return pl.pallas_call(
        functools.partial(_moe_kernel, nc1=nc1),
        grid_spec=pltpu.PrefetchScalarGridSpec(
            num_scalar_prefetch=2,
            grid=(nblk, nc1 + nc2),
            in_specs=in_specs,
            out_specs=pl.BlockSpec((bm, tcol // 2),
                                   lambda i, c, be, na: (i, jnp.maximum(c - nc1, 0))),
            scratch_shapes=[pltpu.VMEM((bm, d), BF16), pltpu.VMEM((nc1, bm, tn), BF16)],
        ),
        out_shape=jax.ShapeDtypeStruct((npad, d // 2), jnp.uint32),
        compiler_params=_cparams(("arbitrary", "arbitrary")),
        name="moe",
    )(blk_expert, n_active, xs, w_gu, w_gu, b_gu, b_gu, w_down, b_down)


def _combine_kernel(idx_ref, y_hbm, h_ref, wt_ref, g_ref, o_ref, buf, sem, *, rows, tt, hw):
    i = pl.program_id(0)
    n = pl.num_programs(0)

    def issue(step, slot):
        def body(r4, _):
            for j in range(4):
                r = r4 * 4 + j
                src = idx_ref[step, r]
                pltpu.make_async_copy(y_hbm.at[pl.ds(src, 1)], buf.at[slot, pl.ds(r, 1)],
                                      sem.at[slot]).start(priority=j % 2)
            return 0
        lax.fori_loop(0, rows // 4, body, 0)

    @pl.when(i == 0)
    def _():
        issue(0, 0)

    @pl.when(i + 1 < n)
    def _():
        issue(i + 1, (i + 1) % 2)

    slot = i % 2
    pltpu.make_async_copy(y_hbm.at[pl.ds(0, rows)], buf.at[slot], sem.at[slot]).wait()
    wts = wt_ref[...]
    lo = hi = None
    for k in range(TOP_K):
        tl, th = _unpack_bf16_pairs(buf[slot, k * tt:(k + 1) * tt, :])
        wk = wts[:, k:k + 1]
        lo = wk * tl if lo is None else lo + wk * tl
        hi = wk * th if hi is None else hi + wk * th
    pieces = []
    for w0 in range(0, lo.shape[1], hw):
        pieces += [lo[:, w0:w0 + hw], hi[:, w0:w0 + hw]]
    acc = h_ref[...] + jnp.concatenate(pieces, axis=1)
    ms = jnp.mean(acc * acc, axis=-1, keepdims=True)
    o_ref[...] = acc * lax.rsqrt(ms + EPS) * g_ref[...]


def _combine(ys, idx2d, h, top_w, gain, tt, tcol):
    nsteps, rows = idx2d.shape
    d = h.shape[1]
    w = ys.shape[1]
    return pl.pallas_call(
        functools.partial(_combine_kernel, rows=rows, tt=tt, hw=tcol // 2),
        grid_spec=pltpu.PrefetchScalarGridSpec(
            num_scalar_prefetch=1,
            grid=(nsteps,),
            in_specs=[
                pl.BlockSpec(memory_space=pl.ANY),
                pl.BlockSpec((tt, d), lambda i, idx: (i, 0)),
                pl.BlockSpec((tt, TOP_K), lambda i, idx: (i, 0)),
                pl.BlockSpec((1, d), lambda i, idx: (0, 0)),
            ],
            out_specs=pl.BlockSpec((tt, d), lambda i, idx: (i, 0)),
            scratch_shapes=[pltpu.VMEM((2, rows, w), jnp.uint32),
                            pltpu.SemaphoreType.DMA((2,))],
        ),
        out_shape=jax.ShapeDtypeStruct(h.shape, F32),
        compiler_params=_cparams(("arbitrary",)),
        name="combine",
    )(idx2d, ys, h, top_w, gain.reshape(1, d).astype(F32))


def _dispatch_plan(top_idx, rank, counts, bm, nblk):
    ne = counts.shape[0]
    padded = ((counts + bm - 1) // bm) * bm
    pad_end = jnp.cumsum(padded)
    pad_start = pad_end - padded
    onehot = top_idx[..., None] == jnp.arange(ne, dtype=jnp.int32)
    dest = rank + jnp.sum(jnp.where(onehot, pad_start, 0), axis=-1)
    blk_row0 = jnp.arange(nblk, dtype=jnp.int32) * bm
    blk_expert = jnp.minimum(
        jnp.sum((pad_end[None, :] <= blk_row0[:, None]).astype(jnp.int32), axis=1), ne - 1)
    total = pad_end[-1:].astype(jnp.int32)
    return (dest.astype(jnp.int32), blk_expert, total // bm,
            (pad_start + counts).astype(jnp.int32), pad_end.astype(jnp.int32), total)


def _rope_tables(n_pos):
    pos = jnp.arange(n_pos, dtype=F32)
    inv_freq = ROPE_THETA ** (-jnp.arange(0, QK_ROPE, 2, dtype=F32) / QK_ROPE)
    ang = pos[:, None] * inv_freq[None, :]
    cos, sin = jnp.cos(ang), jnp.sin(ang)
    zero = jnp.zeros((n_pos, LANES - QK_ROPE), F32)
    return jnp.concatenate([cos, cos, zero], axis=1), jnp.concatenate([-sin, sin, zero], axis=1)


def _with_swapped_rope(w_rope):
    half = QK_ROPE // 2
    return jnp.concatenate([w_rope, w_rope[..., half:], w_rope[..., :half]], axis=-1)


def kernel(x, meta_tokens, g_mix, w_in, b_gate, g_q_lat, g_kv_lat, w_uq, w_ukv, s5_a_re, s5_a_im, s5_log_dt, s5_b_re, s5_b_im, s5_c_re, s5_c_im, s5_d, w_glu_a, w_glu_b, w_out, g_ffn, w_router, b_router, w_gu, b_gu, w_down, b_down, g_final):
    assert w_in.shape[0] == 1, "single-layer block"
    nb, seq, d = x.shape
    t = nb * seq
    n_meta = meta_tokens.shape[0]
    ql, kvl = g_q_lat.shape[-1], g_kv_lat.shape[-1]
    sw = s5_d.shape[-1]
    nh = w_uq.shape[-1] // QK_DIM
    ne = w_router.shape[-1]
    assert ql % kvl == 0 and (ql + kvl) % LANES == 0 and sw % LANES == 0 and d % LANES == 0
    assert n_meta % (2 * SUBLANES) == 0 and n_meta <= LANES

    w0 = w_in[0]
    c_kv = ql + kvl
    c_kr = c_kv + QK_ROPE
    w_main = jnp.concatenate([w0[:, :c_kv], w0[:, c_kr:]], axis=1).astype(BF16)
    w_kr = _with_swapped_rope(w0[:, c_kv:c_kr]).astype(BF16)
    o_s5 = c_kv
    o_gs = o_s5 + sw
    o_gm = o_gs + d
    wq3 = w_uq[0].reshape(ql, nh, QK_DIM)
    w_q = jnp.concatenate([wq3[..., :QK_NOPE], _with_swapped_rope(wq3[..., QK_NOPE:])],
                          axis=-1).reshape(ql, nh * HEAD_PAD).astype(BF16)
    w_kv = w_ukv[0].astype(BF16)
    cos_t, sin_t = _rope_tables(n_meta + seq)
    rope_meta = (cos_t[:n_meta], sin_t[:n_meta])
    rope_real = (cos_t[n_meta:], sin_t[n_meta:])
    s5_tabs = _s5_tables(s5_a_re[0], s5_a_im[0], s5_log_dt[0], s5_b_re[0], s5_b_im[0],
                         s5_c_re[0], s5_c_im[0], s5_d[0])
    bw, cw, a_tab, g_tab_fn, d_skip = s5_tabs

    x2 = x.reshape(t, d)
    tm = 512

    proj_m, kr_m = _norm_mm(meta_tokens, 0, g_mix[0], w_main, BF16, n_meta, 512,
                            side=(w_kr, *rope_meta), name="proj_meta")
    kv_m = _norm_mm(proj_m, ql // kvl, g_kv_lat[0], w_kv, BF16, n_meta, 1024, name="kv_meta")
    kvm = jnp.pad(kv_m, ((0, LANES - n_meta), (0, 0)))
    krm = jnp.pad(kr_m, ((0, LANES - n_meta), (0, 0)))
    ntile = sw // LANES
    ns = bw.shape[2] // 2
    _, c_meta = _s5_scan(proj_m, o_s5 // LANES, 1, n_meta, n_meta,
                         (bw, cw, a_tab, g_tab_fn(n_meta // SUBLANES), d_skip),
                         jnp.zeros((ntile, 2, ns), F32), "s5_meta")

    proj, kr = _norm_mm(x2, 0, g_mix[0], w_main, BF16, tm, 1024, side=(w_kr, *rope_real),
                        name="proj")
    q = _norm_mm(proj, 0, g_q_lat[0], w_q, BF16, 1024, 2048, rope=(*rope_real, HEAD_PAD),
                 out_scale=math.log2(math.e) * QK_DIM ** -0.5, name="q")
    kv = _norm_mm(proj, ql // kvl, g_kv_lat[0], w_kv, BF16, 1024, 2048, name="kv")

    tc = _pick(seq, 512, 2 * SUBLANES)
    z, _ = _s5_scan(proj, o_s5 // LANES, nb, seq, tc,
                    (bw, cw, a_tab, g_tab_fn(tc // SUBLANES), d_skip), c_meta[0], "s5")
    bias = b_gate[0].astype(F32).reshape(1, 2 * d)
    ys = _glu(z, w_glu_a[0].astype(BF16), w_glu_b[0].astype(BF16), proj, o_gs, bias[:, :d], 1024, 1024)
    ym = _attention(q, kv, kr, kvm, krm, proj, o_gm, bias[:, d:], nb, seq, nh, n_meta, 2048, 512)
    h = _outproj(ys, ym, w_out[0].astype(BF16), x2, tm, 1024)

    hp, top_idx, top_w, rank, counts = _router(h, g_ffn[0], w_router[0], b_router[0], 256)
    bm = 512 if t * TOP_K >= 512 * ne else 128
    nblk = (t * TOP_K + ne * (bm - 1) + bm - 1) // bm
    dest, blk_expert, n_active, zero_lo, zero_hi, total = _dispatch_plan(
        top_idx, rank, counts[0], bm, nblk)
    per_step = 1024 if t * TOP_K % 1024 == 0 else 512
    xs = _dispatch(hp, dest.reshape(t * TOP_K // per_step, per_step), zero_lo, zero_hi, total,
                   nblk * bm, 1)
    de = w_down.shape[2]
    tcol = _pick(d, 1024)
    ysort = _moe(xs, blk_expert, n_active, w_gu[0].astype(BF16), b_gu[0].reshape(ne, 1, 2 * de),
                 w_down[0].astype(BF16), b_down[0].reshape(ne, 1, d), bm, _pick(de, 512), tcol)
    tt = 128
    idx_c = dest.reshape(t // tt, tt, TOP_K).transpose(0, 2, 1).reshape(t // tt, TOP_K * tt)
    out = _combine(ysort, idx_c, h, top_w, g_final, tt, tcol)
    return out.reshape(nb, seq, d)
```

```python
import functools
import math

import jax
import jax.numpy as jnp
from jax import lax
from jax.experimental import pallas as pl
from jax.experimental.pallas import tpu as pltpu

F32 = jnp.float32
BF16 = jnp.bfloat16

EPS = 1e-6
QK_NOPE = 128
QK_ROPE = 64
QK_DIM = QK_NOPE + QK_ROPE
V_HEAD = 128
ROPE_THETA = 10000.0
NEG_INF = -1e30
S5_GROUP = 16
TOP_K = 4
SWIGLU_ALPHA = 1.702
SWIGLU_LIMIT = 7.0

LANES = 128
SUBLANES = 8
HEAD_PAD = 2 * LANES
S5_TILE_GROUPS = LANES // S5_GROUP
VMEM_LIMIT_V7X = 56 * 1024 * 1024


def _cparams(sem, vmem=VMEM_LIMIT_V7X):
    return pltpu.CompilerParams(dimension_semantics=sem, vmem_limit_bytes=vmem)


def _pack_bf16_pairs(x):
    half = x.shape[1] // 2
    bits = pltpu.bitcast(x.astype(BF16).astype(F32), jnp.uint32)
    return (bits[:, :half] >> 16) | (bits[:, half:] & jnp.uint32(0xFFFF0000))


def _unpack_bf16_pairs(word):
    return (pltpu.bitcast(word << 16, F32), pltpu.bitcast(word & jnp.uint32(0xFFFF0000), F32))


def _pick(n, pref, mult=LANES):
    if n <= pref:
        return n
    best = None
    for t in range(mult, pref + 1, mult):
        if n % t == 0:
            best = t
    assert best is not None, (n, pref, mult)
    return best


def _rope_slab(slab, c, s):
    return slab * c + pltpu.roll(slab, QK_ROPE, 1) * s


def _norm_mm_kernel(a_ref, g_ref, w_ref, *rest, rope_group, out_scale, side):
    if side:
        cos_ref, sin_ref, ws_ref, o_ref, os_ref, an_ref = rest
    elif rope_group:
        cos_ref, sin_ref, o_ref, an_ref = rest
    else:
        o_ref, an_ref = rest

    @pl.when(pl.program_id(1) == 0)
    def _():
        xf = a_ref[...].astype(F32)
        ms = jnp.mean(xf * xf, axis=-1, keepdims=True)
        an_ref[...] = (xf * lax.rsqrt(ms + EPS) * g_ref[...]).astype(BF16)
        if side:
            slab = jnp.dot(an_ref[...], ws_ref[...], preferred_element_type=F32)
            os_ref[...] = _rope_slab(slab, cos_ref[...], sin_ref[...]).astype(os_ref.dtype)

    acc = jnp.dot(an_ref[...], w_ref[...], preferred_element_type=F32)
    if out_scale is not None:
        acc = acc * out_scale
    if not rope_group:
        o_ref[...] = acc.astype(o_ref.dtype)
        return
    c = cos_ref[...]
    s = sin_ref[...]
    for g0 in range(0, acc.shape[1], rope_group):
        lo = g0 + rope_group - LANES
        if lo > g0:
            o_ref[:, g0:lo] = acc[:, g0:lo].astype(o_ref.dtype)
        o_ref[:, lo:lo + LANES] = _rope_slab(acc[:, lo:lo + LANES], c, s).astype(o_ref.dtype)


def _norm_mm(a, col_blk, gain, w, out_dtype, tm, tn, rope=None, side=None, out_scale=None,
             name=None):
    assert rope is None or side is None
    m = a.shape[0]
    k, n = w.shape
    tables = rope[:2] if rope is not None else side[1:] if side is not None else None
    tm = _pick(m if tables is None else math.gcd(m, tables[0].shape[0]), tm, SUBLANES)
    tn = _pick(n, tn)
    in_specs = [
        pl.BlockSpec((tm, k), lambda i, j: (i, col_blk)),
        pl.BlockSpec((1, k), lambda i, j: (0, 0)),
        pl.BlockSpec((k, tn), lambda i, j: (0, j)),
    ]
    args = [a, gain.reshape(1, k).astype(F32), w]
    out_specs = [pl.BlockSpec((tm, tn), lambda i, j: (i, j))]
    out_shape = [jax.ShapeDtypeStruct((m, n), out_dtype)]
    if tables is not None:
        nrb = tables[0].shape[0] // tm
        in_specs += [pl.BlockSpec((tm, LANES), lambda i, j: (i % nrb, 0))] * 2
        args += list(tables)
    if side is not None:
        in_specs.append(pl.BlockSpec((k, LANES), lambda i, j: (0, 0)))
        args.append(side[0])
        out_specs.append(pl.BlockSpec((tm, LANES), lambda i, j: (i, 0)))
        out_shape.append(jax.ShapeDtypeStruct((m, LANES), out_dtype))
    outs = pl.pallas_call(
        functools.partial(_norm_mm_kernel, rope_group=rope[2] if rope is not None else 0,
                          out_scale=out_scale, side=side is not None),
        grid=(m // tm, n // tn),
        in_specs=in_specs,
        out_specs=out_specs,
        out_shape=out_shape,
        scratch_shapes=[pltpu.VMEM((tm, k), BF16)],
        compiler_params=_cparams(("parallel", "arbitrary")),
        name=name,
    )(*args)
    return outs if side is not None else outs[0]


def _s5_kernel(u_ref, perm_ref, permt_ref, bw_ref, cw_ref, a_ref, g_ref, d_ref, c0_ref,
               z_ref, cout_ref, bu_ref, xb_ref, carry_ref, *, tc, ns):
    t = pl.program_id(2)
    tseg = tc // SUBLANES

    @pl.when(t == 0)
    def _():
        carry_ref[...] = c0_ref[0]

    u_p = jnp.dot(perm_ref[...], u_ref[...], preferred_element_type=F32).astype(BF16)
    bu_ref[...] = jnp.dot(u_p, bw_ref[0], preferred_element_type=F32)

    ar = jnp.broadcast_to(a_ref[0, 0:1, :], (SUBLANES, ns))
    ai = jnp.broadcast_to(a_ref[0, 1:2, :], (SUBLANES, ns))

    def local_scan(i, x):
        xr, xi = x
        r0 = pl.multiple_of(i * SUBLANES, SUBLANES)
        nr = ar * xr - ai * xi + bu_ref[pl.ds(r0, SUBLANES), 0:ns]
        ni = ar * xi + ai * xr + bu_ref[pl.ds(r0, SUBLANES), ns:2 * ns]
        bu_ref[pl.ds(r0, SUBLANES), 0:ns] = nr
        bu_ref[pl.ds(r0, SUBLANES), ns:2 * ns] = ni
        return nr, ni

    zero = jnp.zeros((SUBLANES, ns), F32)
    er, ei = lax.fori_loop(0, tseg, local_scan, (zero, zero), unroll=2 if tseg % 4 == 0 else 1)

    c_r = carry_ref[0:1, :]
    c_i = carry_ref[1:2, :]
    sub = lax.broadcasted_iota(jnp.int32, (SUBLANES, ns), 0)
    g1r = g_ref[0, 0:1, :]
    g1i = g_ref[0, 1:2, :]
    fr = er + jnp.where(sub == 0, g1r * c_r - g1i * c_i, 0.0)
    fi = ei + jnp.where(sub == 0, g1r * c_i + g1i * c_r, 0.0)
    for k, d in enumerate((1, 2, 4)):
        gr = g_ref[0, 2 * k:2 * k + 1, :]
        gi = g_ref[0, 2 * k + 1:2 * k + 2, :]
        sr = jnp.where(sub >= d, pltpu.roll(fr, d, 0), 0.0)
        si = jnp.where(sub >= d, pltpu.roll(fi, d, 0), 0.0)
        fr, fi = fr + gr * sr - gi * si, fi + gr * si + gi * sr
    cin_r = jnp.where(sub == 0, c_r, pltpu.roll(fr, 1, 0))
    cin_i = jnp.where(sub == 0, c_i, pltpu.roll(fi, 1, 0))
    carry_ref[0:1, :] = fr[SUBLANES - 1:SUBLANES, :]
    carry_ref[1:2, :] = fi[SUBLANES - 1:SUBLANES, :]

    def add_carry(j, c):
        cr, ci = c
        c1r, c1i = ar * cr - ai * ci, ar * ci + ai * cr
        c2r, c2i = ar * c1r - ai * c1i, ar * c1i + ai * c1r
        r0 = pl.multiple_of(j * 2 * SUBLANES, 2 * SUBLANES)
        r1 = r0 + SUBLANES
        xr = jnp.concatenate([bu_ref[pl.ds(r0, SUBLANES), 0:ns] + c1r,
                              bu_ref[pl.ds(r1, SUBLANES), 0:ns] + c2r], axis=0)
        xi = jnp.concatenate([bu_ref[pl.ds(r0, SUBLANES), ns:2 * ns] + c1i,
                              bu_ref[pl.ds(r1, SUBLANES), ns:2 * ns] + c2i], axis=0)
        xb_ref[pl.ds(r0, 2 * SUBLANES), 0:ns] = xr.astype(BF16)
        xb_ref[pl.ds(r0, 2 * SUBLANES), ns:2 * ns] = xi.astype(BF16)
        return c2r, c2i

    lax.fori_loop(0, tseg // 2, add_carry, (cin_r, cin_i), unroll=2 if tseg % 4 == 0 else 1)

    y = jnp.dot(xb_ref[...], cw_ref[0], preferred_element_type=F32) + d_ref[...] * u_p.astype(F32)
    zp = jax.nn.gelu(y).astype(BF16)
    z_ref[...] = jnp.dot(permt_ref[...], zp, preferred_element_type=F32).astype(BF16)

    @pl.when(t == pl.num_programs(2) - 1)
    def _():
        cout_ref[0, 0] = carry_ref[...]


def _s5_scan(u_arr, u_col0, nb, seq, tc, tabs, c0, name):
    bw, cw, a_tab, g_tab, d_skip = tabs
    ntile, _, ns2 = bw.shape
    ns = ns2 // 2
    nt = seq // tc
    tseg = tc // SUBLANES
    r = jnp.arange(tc, dtype=jnp.int32)
    src = (r % SUBLANES) * tseg + r // SUBLANES
    perm = (src[:, None] == jnp.arange(tc, dtype=jnp.int32)[None, :]).astype(BF16)
    kern = functools.partial(_s5_kernel, tc=tc, ns=ns)
    return pl.pallas_call(
        kern,
        grid=(nb, ntile, nt),
        in_specs=[
            pl.BlockSpec((tc, LANES), lambda b, n, t: (b * nt + t, u_col0 + n)),
            pl.BlockSpec((tc, tc), lambda b, n, t: (0, 0)),
            pl.BlockSpec((tc, tc), lambda b, n, t: (0, 0)),
            pl.BlockSpec((1, LANES, 2 * ns), lambda b, n, t: (n, 0, 0)),
            pl.BlockSpec((1, 2 * ns, LANES), lambda b, n, t: (n, 0, 0)),
            pl.BlockSpec((1, 2, ns), lambda b, n, t: (n, 0, 0)),
            pl.BlockSpec((1, 6, ns), lambda b, n, t: (n, 0, 0)),
            pl.BlockSpec((1, LANES), lambda b, n, t: (0, n)),
            pl.BlockSpec((1, 2, ns), lambda b, n, t: (n, 0, 0)),
        ],
        out_specs=[
            pl.BlockSpec((tc, LANES), lambda b, n, t: (b * nt + t, n)),
            pl.BlockSpec((1, 1, 2, ns), lambda b, n, t: (b, n, 0, 0)),
        ],
        out_shape=[
            jax.ShapeDtypeStruct((nb * seq, ntile * LANES), BF16),
            jax.ShapeDtypeStruct((nb, ntile, 2, ns), F32),
        ],
        scratch_shapes=[
            pltpu.VMEM((tc, 2 * ns), F32),
            pltpu.VMEM((tc, 2 * ns), BF16),
            pltpu.VMEM((2, ns), F32),
        ],
        compiler_params=_cparams(("parallel", "parallel", "arbitrary")),
        name=name,
    )(u_arr, perm, perm.T, bw, cw, a_tab, g_tab, d_skip, c0)


def _s5_tables(a_re, a_im, log_dt, b_re, b_im, c_re, c_im, d_skip):
    ngrp, nst = a_re.shape
    ntile = ngrp // S5_TILE_GROUPS
    ns = S5_TILE_GROUPS * nst
    dt = jnp.exp(log_dt.astype(F32))[:, None]
    lr, li = a_re.astype(F32), a_im.astype(F32)

    def a_pow(k):
        mag = jnp.exp(lr * dt * k)
        return mag * jnp.cos(li * dt * k), mag * jnp.sin(li * dt * k)

    ab_re, ab_im = a_pow(1)
    den = lr * lr + li * li
    f_re = ((ab_re - 1.0) * lr + ab_im * li) / den
    f_im = (ab_im * lr - (ab_re - 1.0) * li) / den
    br, bi = b_re.astype(F32), b_im.astype(F32)
    bb_re = f_re[..., None] * br - f_im[..., None] * bi
    bb_im = f_re[..., None] * bi + f_im[..., None] * br
    eye = jnp.eye(S5_TILE_GROUPS, dtype=F32)

    def b_tile(bb):
        bb = bb.reshape(ntile, S5_TILE_GROUPS, nst, S5_GROUP)
        return jnp.einsum("ab,napc->nacbp", eye, bb).reshape(ntile, LANES, ns)

    def c_tile(cc):
        cc = cc.astype(F32).reshape(ntile, S5_TILE_GROUPS, S5_GROUP, nst)
        return jnp.einsum("ab,nacp->nbpac", eye, cc).reshape(ntile, ns, LANES)

    bw = jnp.concatenate([b_tile(bb_re), b_tile(bb_im)], axis=2).astype(BF16)
    cw = jnp.concatenate([c_tile(c_re), -c_tile(c_im)], axis=1).astype(BF16)
    a_tab = jnp.stack([ab_re.reshape(ntile, ns), ab_im.reshape(ntile, ns)], axis=1)

    def g_tab(tseg):
        rows = []
        for d in (1, 2, 4):
            pr, pi = a_pow(tseg * d)
            rows += [pr.reshape(ntile, ns), pi.reshape(ntile, ns)]
        return jnp.stack(rows, axis=1)

    return bw, cw, a_tab, g_tab, d_skip.astype(F32).reshape(1, -1)


def _glu_kernel(z_ref, wa_ref, wb_ref, gs_ref, bg_ref, o_ref):
    z = z_ref[...]
    a = jnp.dot(z, wa_ref[...], preferred_element_type=F32)
    b = jnp.dot(z, wb_ref[...], preferred_element_type=F32)
    gate = jax.nn.sigmoid(gs_ref[...].astype(F32) + bg_ref[...])
    o_ref[...] = (a * jax.nn.sigmoid(b) * gate).astype(o_ref.dtype)


def _glu(z, wa, wb, proj, gs_col0, bias, tm, tn):
    m, k = z.shape
    n = wa.shape[1]
    tm = _pick(m, tm, SUBLANES)
    tn = _pick(math.gcd(n, gs_col0) if gs_col0 else n, tn)
    goff = gs_col0 // tn
    return pl.pallas_call(
        _glu_kernel,
        grid=(m // tm, n // tn),
        in_specs=[
            pl.BlockSpec((tm, k), lambda i, j: (i, 0)),
            pl.BlockSpec((k, tn), lambda i, j: (0, j)),
            pl.BlockSpec((k, tn), lambda i, j: (0, j)),
            pl.BlockSpec((tm, tn), lambda i, j: (i, goff + j)),
            pl.BlockSpec((1, tn), lambda i, j: (0, j)),
        ],
        out_specs=pl.BlockSpec((tm, tn), lambda i, j: (i, j)),
        out_shape=jax.ShapeDtypeStruct((m, n), BF16),
        compiler_params=_cparams(("parallel", "arbitrary")),
        name="glu",
    )(z, wa, wb, proj, bias)


def _attn_kernel(q_ref, kn_ref, v_ref, kr_ref, km_ref, vm_ref, krm_ref, gm_ref, bg_ref, o_ref,
                 *, tq, tk, n_meta):
    qi = pl.program_id(2)

    def update(q, carry, j, mask_row0):
        m, l, acc = carry
        r0 = pl.multiple_of(j * tk, tk)
        k = jnp.concatenate([kn_ref[pl.ds(r0, tk), :], kr_ref[pl.ds(r0, tk), :]], axis=1)
        s = lax.dot_general(q, k, (((1,), (1,)), ((), ())), preferred_element_type=F32)
        if mask_row0 is not None:
            row = mask_row0 + lax.broadcasted_iota(jnp.int32, s.shape, 0)
            colp = j * tk + lax.broadcasted_iota(jnp.int32, s.shape, 1)
            s = jnp.where(colp <= row, s, NEG_INF)
        m_new = jnp.maximum(m, jnp.max(s, axis=1, keepdims=True))
        alpha = jnp.exp2(m - m_new)
        p = jnp.exp2(s - m_new)
        l = alpha * l + jnp.sum(p, axis=1, keepdims=True)
        acc = alpha * acc + jnp.dot(p.astype(BF16), v_ref[pl.ds(r0, tk), :],
                                    preferred_element_type=F32)
        return m_new, l, acc

    q = q_ref[...]
    km = jnp.concatenate([km_ref[...], krm_ref[...]], axis=1)
    s = lax.dot_general(q, km, (((1,), (1,)), ((), ())), preferred_element_type=F32)
    col = lax.broadcasted_iota(jnp.int32, s.shape, 1)
    s = jnp.where(col < n_meta, s, NEG_INF)
    m = jnp.max(s, axis=1, keepdims=True)
    p = jnp.exp2(s - m)
    l = jnp.sum(p, axis=1, keepdims=True)
    acc = jnp.dot(p.astype(BF16), vm_ref[...], preferred_element_type=F32)

    n_full = (qi * tq) // tk
    m, l, acc = lax.fori_loop(0, n_full, lambda j, c: update(q, c, j, None), (m, l, acc))

    for r in range(tq // tk):
        rows = slice(r * tk, (r + 1) * tk)
        qr = q_ref[rows, :]
        carry = (m[rows], l[rows], acc[rows])
        for dd in range(r):
            carry = update(qr, carry, n_full + dd, None)
        _, lr, ar = update(qr, carry, n_full + r, qi * tq + r * tk)
        gate = jax.nn.sigmoid(gm_ref[rows, :].astype(F32) + bg_ref[...])
        o_ref[rows, :] = (ar / lr * gate).astype(o_ref.dtype)


def _attention(q, kv, kr, kvm, krm, proj, gm_col0, bias, nb, seq, nh, n_meta, tq, tk):
    tq = _pick(seq, tq, SUBLANES)
    tk = _pick(tq, tk, SUBLANES)
    nq = seq // tq
    mrows = kvm.shape[0]
    goff = gm_col0 // LANES
    kern = functools.partial(_attn_kernel, tq=tq, tk=tk, n_meta=n_meta)
    return pl.pallas_call(
        kern,
        grid=(nb, nh, nq),
        in_specs=[
            pl.BlockSpec((tq, HEAD_PAD), lambda b, h, i: (b * nq + i, h)),
            pl.BlockSpec((seq, LANES), lambda b, h, i: (b, 2 * h)),
            pl.BlockSpec((seq, LANES), lambda b, h, i: (b, 2 * h + 1)),
            pl.BlockSpec((seq, LANES), lambda b, h, i: (b, 0)),
            pl.BlockSpec((mrows, LANES), lambda b, h, i: (0, 2 * h)),
            pl.BlockSpec((mrows, LANES), lambda b, h, i: (0, 2 * h + 1)),
            pl.BlockSpec((mrows, LANES), lambda b, h, i: (0, 0)),
            pl.BlockSpec((tq, LANES), lambda b, h, i: (b * nq + i, goff + h)),
            pl.BlockSpec((1, LANES), lambda b, h, i: (0, h)),
        ],
        out_specs=pl.BlockSpec((tq, LANES), lambda b, h, i: (b * nq + i, h)),
        out_shape=jax.ShapeDtypeStruct((nb * seq, nh * V_HEAD), BF16),
        compiler_params=_cparams(("parallel", "parallel", "arbitrary")),
        name="attention",
    )(q, kv, kv, kr, kvm, kvm, krm, proj, bias)


def _outproj_kernel(ys_ref, ym_ref, w_ref, res_ref, o_ref, a_ref):
    @pl.when(pl.program_id(1) == 0)
    def _():
        a_ref[...] = (ys_ref[...].astype(F32) + ym_ref[...].astype(F32)).astype(BF16)

    o_ref[...] = jnp.dot(a_ref[...], w_ref[...], preferred_element_type=F32) + res_ref[...]


def _outproj(ys, ym, w, res, tm, tn):
    m, k = ys.shape
    n = w.shape[1]
    tm = _pick(m, tm, SUBLANES)
    tn = _pick(n, tn)
    return pl.pallas_call(
        _outproj_kernel,
        grid=(m // tm, n // tn),
        in_specs=[
            pl.BlockSpec((tm, k), lambda i, j: (i, 0)),
            pl.BlockSpec((tm, k), lambda i, j: (i, 0)),
            pl.BlockSpec((k, tn), lambda i, j: (0, j)),
            pl.BlockSpec((tm, tn), lambda i, j: (i, j)),
        ],
        out_specs=pl.BlockSpec((tm, tn), lambda i, j: (i, j)),
        out_shape=jax.ShapeDtypeStruct((m, n), F32),
        scratch_shapes=[pltpu.VMEM((tm, k), BF16)],
        compiler_params=_cparams(("parallel", "arbitrary")),
        name="outproj",
    )(ys, ym, w, res)


def _router_kernel(h_ref, g_ref, wr_ref, wrl_ref, br_ref, hp_ref, idx_ref, wt_ref, rank_ref,
                   cnt_ref, run_ref):
    @pl.when(pl.program_id(0) == 0)
    def _():
        run_ref[...] = jnp.zeros_like(run_ref)

    xf = h_ref[...]
    tm, d = xf.shape
    ms = jnp.mean(xf * xf, axis=-1, keepdims=True)
    hn = xf * lax.rsqrt(ms + EPS) * g_ref[...]

    hp_ref[...] = _pack_bf16_pairs(hn)

    hn_hi = hn.astype(BF16)
    hn_lo = (hn - hn_hi.astype(F32)).astype(BF16)
    logits = (jnp.dot(hn_hi, wr_ref[...], preferred_element_type=F32)
              + jnp.dot(hn_lo, wr_ref[...], preferred_element_type=F32)
              + jnp.dot(hn_hi, wrl_ref[...], preferred_element_type=F32)) + br_ref[...]
    ne = logits.shape[1]
    lane = lax.broadcasted_iota(jnp.int32, logits.shape, 1)
    vals, ids = [], []
    for _ in range(TOP_K):
        mx = jnp.max(logits, axis=1, keepdims=True)
        ix = jnp.min(jnp.where(logits == mx, lane, ne), axis=1, keepdims=True)
        vals.append(mx)
        ids.append(ix)
        logits = jnp.where(lane == ix, -jnp.inf, logits)
    v = jnp.concatenate(vals, axis=1)
    e = jnp.exp(v - v[:, 0:1])
    idx_ref[...] = jnp.concatenate(ids, axis=1)
    wt_ref[...] = e / jnp.sum(e, axis=1, keepdims=True)

    tri = (lax.broadcasted_iota(jnp.int32, (tm, tm), 0)
           > lax.broadcasted_iota(jnp.int32, (tm, tm), 1)).astype(BF16)
    base = run_ref[...]
    ranks = []
    for k in range(TOP_K):
        oh = (lane == ids[k]).astype(F32)
        prefix = jnp.dot(tri, oh.astype(BF16), preferred_element_type=F32)
        ranks.append(jnp.sum(oh * (base + prefix), axis=1, keepdims=True))
        base = base + jnp.sum(oh, axis=0, keepdims=True)
    run_ref[...] = base
    rank_ref[...] = jnp.concatenate(ranks, axis=1).astype(jnp.int32)
    cnt_ref[...] = base.astype(jnp.int32)


def _router(h, gain, w_router, b_router, tm):
    m, d = h.shape
    ne = w_router.shape[1]
    tm = _pick(m, tm, SUBLANES)
    wr_hi = w_router.astype(BF16)
    wr_lo = (w_router.astype(F32) - wr_hi.astype(F32)).astype(BF16)
    return pl.pallas_call(
        _router_kernel,
        grid=(m // tm,),
        in_specs=[
            pl.BlockSpec((tm, d), lambda i: (i, 0)),
            pl.BlockSpec((1, d), lambda i: (0, 0)),
            pl.BlockSpec((d, ne), lambda i: (0, 0)),
            pl.BlockSpec((d, ne), lambda i: (0, 0)),
            pl.BlockSpec((1, ne), lambda i: (0, 0)),
        ],
        out_specs=[
            pl.BlockSpec((tm, d // 2), lambda i: (i, 0)),
            pl.BlockSpec((tm, TOP_K), lambda i: (i, 0)),
            pl.BlockSpec((tm, TOP_K), lambda i: (i, 0)),
            pl.BlockSpec((tm, TOP_K), lambda i: (i, 0)),
            pl.BlockSpec((1, ne), lambda i: (0, 0)),
        ],
        out_shape=[
            jax.ShapeDtypeStruct((m, d // 2), jnp.uint32),
            jax.ShapeDtypeStruct((m, TOP_K), jnp.int32),
            jax.ShapeDtypeStruct((m, TOP_K), F32),
            jax.ShapeDtypeStruct((m, TOP_K), jnp.int32),
            jax.ShapeDtypeStruct((1, ne), jnp.int32),
        ],
        scratch_shapes=[pltpu.VMEM((1, ne), F32)],
        compiler_params=_cparams(("arbitrary",)),
        name="router",
    )(h, gain.reshape(1, d).astype(F32), wr_hi, wr_lo, b_router.reshape(1, ne).astype(F32))


def _dispatch_kernel(dest_ref, zlo_ref, zhi_ref, tot_ref, hp_ref, xs_hbm, stage, zbuf, sem, zsem,
                     *, per_step, npad, zrows, sr):
    i = pl.program_id(0)
    n = pl.num_programs(0)
    ne = zlo_ref.shape[0]
    tt = per_step // TOP_K

    def zero_row(r):
        return pltpu.make_async_copy(zbuf.at[pl.ds(0, sr)], xs_hbm.at[pl.ds(r * sr, sr)],
                                     zsem.at[0])

    def zero_chunk(c):
        return pltpu.make_async_copy(zbuf, xs_hbm.at[pl.ds(c * (zrows * sr), zrows * sr)],
                                     zsem.at[0])

    def for_zero_fills(fn_row, fn_chunk):
        def per_expert(e, _):
            lax.fori_loop(zlo_ref[e], zhi_ref[e], lambda r, _: (fn_row(r), 0)[1], 0)
            return 0
        lax.fori_loop(0, ne, per_expert, 0)
        lax.fori_loop(tot_ref[0] // zrows, npad // zrows, lambda c, _: (fn_chunk(c), 0)[1], 0)

    @pl.when(i == 0)
    def _():
        zbuf[...] = jnp.zeros_like(zbuf)
        for_zero_fills(lambda r: zero_row(r).start(), lambda c: zero_chunk(c).start())

    slot = i % 2
    stage[slot] = hp_ref[...]

    def body(lt, _):
        src = stage.at[slot, pl.ds(lt * sr, sr)]
        for k in range(TOP_K):
            pltpu.make_async_copy(src, xs_hbm.at[pl.ds(dest_ref[i, lt * TOP_K + k] * sr, sr)],
                                  sem.at[slot]).start(priority=k % 2)
        return 0
    lax.fori_loop(0, tt, body, 0)

    def wait_step(s):
        for _ in range(TOP_K):
            pltpu.make_async_copy(stage.at[s], xs_hbm.at[pl.ds(0, tt * sr)], sem.at[s]).wait()

    @pl.when(i > 0)
    def _():
        wait_step((i + 1) % 2)

    @pl.when(i == n - 1)
    def _():
        wait_step(i % 2)
        for_zero_fills(lambda r: zero_row(r).wait(), lambda c: zero_chunk(c).wait())


def _dispatch(hp, dest2d, zero_lo, zero_hi, total, npad, sr):
    nsteps, per_step = dest2d.shape
    w = hp.shape[1]
    zrows = 64
    assert npad % zrows == 0
    return pl.pallas_call(
        functools.partial(_dispatch_kernel, per_step=per_step, npad=npad, zrows=zrows, sr=sr),
        grid_spec=pltpu.PrefetchScalarGridSpec(
            num_scalar_prefetch=4,
            grid=(nsteps,),
            in_specs=[pl.BlockSpec((per_step // TOP_K * sr, w), lambda i, *_: (i, 0))],
            out_specs=pl.BlockSpec(memory_space=pl.ANY),
            scratch_shapes=[pltpu.VMEM((2, per_step // TOP_K * sr, w), hp.dtype),
                            pltpu.VMEM((zrows * sr, w), hp.dtype),
                            pltpu.SemaphoreType.DMA((2,)), pltpu.SemaphoreType.DMA((1,))],
        ),
        out_shape=jax.ShapeDtypeStruct((npad * sr, w), hp.dtype),
        compiler_params=_cparams(("arbitrary",)),
        name="dispatch",
    )(dest2d, zero_lo, zero_hi, total, hp)


def _moe_kernel(be_ref, na_ref, x_ref, wg_ref, wu_ref, bg_ref, bu_ref, wd_ref, bd_ref,
                o_ref, xb_ref, act_ref, *, nc1):
    i = pl.program_id(0)
    c = pl.program_id(1)
    active = i < na_ref[0]
    tn = act_ref.shape[2]

    @pl.when(jnp.logical_and(active, c == 0))
    def _():
        half = x_ref.shape[1]
        lo, hi = _unpack_bf16_pairs(x_ref[...])
        xb_ref[:, :half] = lo.astype(BF16)
        xb_ref[:, half:] = hi.astype(BF16)

    @pl.when(jnp.logical_and(active, c < nc1))
    def _():
        xb = xb_ref[...]
        g = jnp.dot(xb, wg_ref[0], preferred_element_type=F32) + bg_ref[0]
        u = jnp.dot(xb, wu_ref[0], preferred_element_type=F32) + bu_ref[0]
        gate = jnp.minimum(g, SWIGLU_LIMIT)
        up = jnp.clip(u, -SWIGLU_LIMIT, SWIGLU_LIMIT)
        act_ref[c] = (gate * jax.nn.sigmoid(SWIGLU_ALPHA * gate) * (up + 1.0)).astype(BF16)

    @pl.when(jnp.logical_and(active, c >= nc1))
    def _():
        y = bd_ref[0] + jnp.dot(act_ref[0], wd_ref[0, 0:tn, :], preferred_element_type=F32)
        for cc in range(1, nc1):
            y += jnp.dot(act_ref[cc], wd_ref[0, cc * tn:(cc + 1) * tn, :],
                         preferred_element_type=F32)
        o_ref[...] = _pack_bf16_pairs(y)

    @pl.when(jnp.logical_and(jnp.logical_not(active), c >= nc1))
    def _():
        o_ref[...] = jnp.zeros_like(o_ref)


def _moe(xs, blk_expert, n_active, w_gu, b_gu, w_down, b_down, bm, tn, tcol):
    npad, w = xs.shape
    ne, de, d = w_down.shape
    nc1 = de // tn
    nc2 = d // tcol
    nblk = npad // bm

    def blk(i, na):
        return jnp.clip(i, 0, na[0] - 1)

    def x_map(i, c, be, na):
        return (jnp.where(c >= nc1, blk(i + 1, na), blk(i, na)), 0)

    def gu_map(first_next, col0):
        def index(i, c, be, na):
            nxt = c >= first_next
            e = be[jnp.where(nxt, blk(i + 1, na), blk(i, na))]
            return (e, 0, col0 + jnp.where(nxt, 0, jnp.minimum(c, nc1 - 1)))
        return index

    def down_map(i, c, be, na):
        cur = c >= nc1
        e = be[jnp.where(cur, blk(i, na), blk(i - 1, na))]
        return (e, 0, jnp.where(cur, c - nc1, nc2 - 1))

    g_next = nc1 + nc2 - 1
    u_next = nc1 + nc2
    in_specs = [
        pl.BlockSpec((bm, w), x_map),
        pl.BlockSpec((1, d, tn), gu_map(g_next, 0)),
        pl.BlockSpec((1, d, tn), gu_map(u_next, nc1)),
        pl.BlockSpec((1, 1, tn), gu_map(g_next, 0)),
        pl.BlockSpec((1, 1, tn), gu_map(u_next, nc1)),
        pl.BlockSpec((1, de, tcol), down_map),
        pl.BlockSpec((1, 1, tcol), down_map),
    ]
    return pl.pallas_call(
        functools.partial(_moe_kernel, nc1=nc1),
        grid_spec=pltpu.PrefetchScalarGridSpec(
            num_scalar_prefetch=2,
            grid=(nblk, nc1 + nc2),
            in_specs=in_specs,
            out_specs=pl.BlockSpec((bm, tcol // 2),
                                   lambda i, c, be, na: (i, jnp.maximum(c - nc1, 0))),
            scratch_shapes=[pltpu.VMEM((bm, d), BF16), pltpu.VMEM((nc1, bm, tn), BF16)],
        ),
        out_shape=jax.ShapeDtypeStruct((npad, d // 2), jnp.uint32),
        compiler_params=_cparams(("arbitrary", "arbitrary")),
        name="moe",
    )(blk_expert, n_active, xs, w_gu, w_gu, b_gu, b_gu, w_down, b_down)


def _combine_kernel(idx_ref, y_hbm, h_ref, wt_ref, g_ref, o_ref, buf, sem, *, rows, tt, hw):
    i = pl.program_id(0)
    n = pl.num_programs(0)

    def issue(step, slot):
        def body(r4, _):
            for j in range(4):
                r = r4 * 4 + j
                src = idx_ref[step, r]
                pltpu.make_async_copy(y_hbm.at[pl.ds(src, 1)], buf.at[slot, pl.ds(r, 1)],
                                      sem.at[slot]).start(priority=j % 2)
            return 0
        lax.fori_loop(0, rows // 4, body, 0)

    @pl.when(i == 0)
    def _():
        issue(0, 0)

    @pl.when(i + 1 < n)
    def _():
        issue(i + 1, (i + 1) % 2)

    slot = i % 2
    pltpu.make_async_copy(y_hbm.at[pl.ds(0, rows)], buf.at[slot], sem.at[slot]).wait()
    wts = wt_ref[...]
    lo = hi = None
    for k in range(TOP_K):
        tl, th = _unpack_bf16_pairs(buf[slot, k * tt:(k + 1) * tt, :])
        wk = wts[:, k:k + 1]
        lo = wk * tl if lo is None else lo + wk * tl
        hi = wk * th if hi is None else hi + wk * th
    pieces = []
    for w0 in range(0, lo.shape[1], hw):
        pieces += [lo[:, w0:w0 + hw], hi[:, w0:w0 + hw]]
    acc = h_ref[...] + jnp.concatenate(pieces, axis=1)
    ms = jnp.mean(acc * acc, axis=-1, keepdims=True)
    o_ref[...] = acc * lax.rsqrt(ms + EPS) * g_ref[...]


def _combine(ys, idx2d, h, top_w, gain, tt, tcol):
    nsteps, rows = idx2d.shape
    d = h.shape[1]
    w = ys.shape[1]
    return pl.pallas_call(
        functools.partial(_combine_kernel, rows=rows, tt=tt, hw=tcol // 2),
        grid_spec=pltpu.PrefetchScalarGridSpec(
            num_scalar_prefetch=1,
            grid=(nsteps,),
            in_specs=[
                pl.BlockSpec(memory_space=pl.ANY),
                pl.BlockSpec((tt, d), lambda i, idx: (i, 0)),
                pl.BlockSpec((tt, TOP_K), lambda i, idx: (i, 0)),
                pl.BlockSpec((1, d), lambda i, idx: (0, 0)),
            ],
            out_specs=pl.BlockSpec((tt, d), lambda i, idx: (i, 0)),
            scratch_shapes=[pltpu.VMEM((2, rows, w), jnp.uint32),
                            pltpu.SemaphoreType.DMA((2,))],
        ),
        out_shape=jax.ShapeDtypeStruct(h.shape, F32),
        compiler_params=_cparams(("arbitrary",)),
        name="combine",
    )(idx2d, ys, h, top_w, gain.reshape(1, d).astype(F32))


def _dispatch_plan(top_idx, rank, counts, bm, nblk):
    ne = counts.shape[0]
    padded = ((counts + bm - 1) // bm) * bm
    pad_end = jnp.cumsum(padded)
    pad_start = pad_end - padded
    onehot = top_idx[..., None] == jnp.arange(ne, dtype=jnp.int32)
    dest = rank + jnp.sum(jnp.where(onehot, pad_start, 0), axis=-1)
    blk_row0 = jnp.arange(nblk, dtype=jnp.int32) * bm
    blk_expert = jnp.minimum(
        jnp.sum((pad_end[None, :] <= blk_row0[:, None]).astype(jnp.int32), axis=1), ne - 1)
    total = pad_end[-1:].astype(jnp.int32)
    return (dest.astype(jnp.int32), blk_expert, total // bm,
            (pad_start + counts).astype(jnp.int32), pad_end.astype(jnp.int32), total)


def _rope_tables(n_pos):
    pos = jnp.arange(n_pos, dtype=F32)
    inv_freq = ROPE_THETA ** (-jnp.arange(0, QK_ROPE, 2, dtype=F32) / QK_ROPE)
    ang = pos[:, None] * inv_freq[None, :]
    cos, sin = jnp.cos(ang), jnp.sin(ang)
    zero = jnp.zeros((n_pos, LANES - QK_ROPE), F32)
    return jnp.concatenate([cos, cos, zero], axis=1), jnp.concatenate([-sin, sin, zero], axis=1)


def _with_swapped_rope(w_rope):
    half = QK_ROPE // 2
    return jnp.concatenate([w_rope, w_rope[..., half:], w_rope[..., :half]], axis=-1)


def kernel(x, meta_tokens, g_mix, w_in, b_gate, g_q_lat, g_kv_lat, w_uq, w_ukv, s5_a_re, s5_a_im, s5_log_dt, s5_b_re, s5_b_im, s5_c_re, s5_c_im, s5_d, w_glu_a, w_glu_b, w_out, g_ffn, w_router, b_router, w_gu, b_gu, w_down, b_down, g_final):
    assert w_in.shape[0] == 1, "single-layer block"
    nb, seq, d = x.shape
    t = nb * seq
    n_meta = meta_tokens.shape[0]
    ql, kvl = g_q_lat.shape[-1], g_kv_lat.shape[-1]
    sw = s5_d.shape[-1]
    nh = w_uq.shape[-1] // QK_DIM
    ne = w_router.shape[-1]
    assert ql % kvl == 0 and (ql + kvl) % LANES == 0 and sw % LANES == 0 and d % LANES == 0
    assert n_meta % (2 * SUBLANES) == 0 and n_meta <= LANES

    w0 = w_in[0]
    c_kv = ql + kvl
    c_kr = c_kv + QK_ROPE
    w_main = jnp.concatenate([w0[:, :c_kv], w0[:, c_kr:]], axis=1).astype(BF16)
    w_kr = _with_swapped_rope(w0[:, c_kv:c_kr]).astype(BF16)
    o_s5 = c_kv
    o_gs = o_s5 + sw
    o_gm = o_gs + d
    wq3 = w_uq[0].reshape(ql, nh, QK_DIM)
    w_q = jnp.concatenate([wq3[..., :QK_NOPE], _with_swapped_rope(wq3[..., QK_NOPE:])],
                          axis=-1).reshape(ql, nh * HEAD_PAD).astype(BF16)
    w_kv = w_ukv[0].astype(BF16)
    cos_t, sin_t = _rope_tables(n_meta + seq)
    rope_meta = (cos_t[:n_meta], sin_t[:n_meta])
    rope_real = (cos_t[n_meta:], sin_t[n_meta:])
    s5_tabs = _s5_tables(s5_a_re[0], s5_a_im[0], s5_log_dt[0], s5_b_re[0], s5_b_im[0],
                         s5_c_re[0], s5_c_im[0], s5_d[0])
    bw, cw, a_tab, g_tab_fn, d_skip = s5_tabs

    x2 = x.reshape(t, d)
    tm = 512

    proj_m, kr_m = _norm_mm(meta_tokens, 0, g_mix[0], w_main, BF16, n_meta, 512,
                            side=(w_kr, *rope_meta), name="proj_meta")
    kv_m = _norm_mm(proj_m, ql // kvl, g_kv_lat[0], w_kv, BF16, n_meta, 1024, name="kv_meta")
    kvm = jnp.pad(kv_m, ((0, LANES - n_meta), (0, 0)))
    krm = jnp.pad(kr_m, ((0, LANES - n_meta), (0, 0)))
    ntile = sw // LANES
    ns = bw.shape[2] // 2
    _, c_meta = _s5_scan(proj_m, o_s5 // LANES, 1, n_meta, n_meta,
                         (bw, cw, a_tab, g_tab_fn(n_meta // SUBLANES), d_skip),
                         jnp.zeros((ntile, 2, ns), F32), "s5_meta")

    proj, kr = _norm_mm(x2, 0, g_mix[0], w_main, BF16, tm, 1024, side=(w_kr, *rope_real),
                        name="proj")
    q = _norm_mm(proj, 0, g_q_lat[0], w_q, BF16, 1024, 2048, rope=(*rope_real, HEAD_PAD),
                 out_scale=math.log2(math.e) * QK_DIM ** -0.5, name="q")
    kv = _norm_mm(proj, ql // kvl, g_kv_lat[0], w_kv, BF16, 1024, 2048, name="kv")

    tc = _pick(seq, 512, 2 * SUBLANES)
    z, _ = _s5_scan(proj, o_s5 // LANES, nb, seq, tc,
                    (bw, cw, a_tab, g_tab_fn(tc // SUBLANES), d_skip), c_meta[0], "s5")
    bias = b_gate[0].astype(F32).reshape(1, 2 * d)
    ys = _glu(z, w_glu_a[0].astype(BF16), w_glu_b[0].astype(BF16), proj, o_gs, bias[:, :d], 1024, 1024)
    ym = _attention(q, kv, kr, kvm, krm, proj, o_gm, bias[:, d:], nb, seq, nh, n_meta, 4096, 512)
    h = _outproj(ys, ym, w_out[0].astype(BF16), x2, tm, 1024)

    hp, top_idx, top_w, rank, counts = _router(h, g_ffn[0], w_router[0], b_router[0], 256)
    bm = 512 if t * TOP_K >= 512 * ne else 128
    nblk = (t * TOP_K + ne * (bm - 1) + bm - 1) // bm
    dest, blk_expert, n_active, zero_lo, zero_hi, total = _dispatch_plan(
        top_idx, rank, counts[0], bm, nblk)
    per_step = 1024 if t * TOP_K % 1024 == 0 else 512
    xs = _dispatch(hp, dest.reshape(t * TOP_K // per_step, per_step), zero_lo, zero_hi, total,
                   nblk * bm, 1)
    de = w_down.shape[2]
    tcol = _pick(d, 1024)
    ysort = _moe(xs, blk_expert, n_active, w_gu[0].astype(BF16), b_gu[0].reshape(ne, 1, 2 * de),
                 w_down[0].astype(BF16), b_down[0].reshape(ne, 1, d), bm, _pick(de, 512), tcol)
    tt = 128
    idx_c = dest.reshape(t // tt, tt, TOP_K).transpose(0, 2, 1).reshape(t // tt, TOP_K * tt)
    out = _combine(ysort, idx_c, h, top_w, g_final, tt, tcol)
    return out.reshape(nb, seq, d)
```
